```python
import math
import jax, jax.numpy as jnp
from jax import lax
import numpy as np

D_MODEL = 1024
BATCH = 4
SEQ = 8192
DEPTH = 1

GRID_W = 64
CTX_LEN = 256
HEAD_DIM = 64
A_HEADS = 4
R_HEADS = 4
R_DK = 64
R_DV = 128
A_QK = A_HEADS * 2 * HEAD_DIM
A_V = A_HEADS * 2 * HEAD_DIM
R_QK = R_HEADS * R_DK
R_V = R_HEADS * R_DV
D_IN = 2 * A_QK + A_V + 2 * R_QK + 2 * R_V
D_MIX = A_V + R_V
Q_BLOCK = 128
RET_CHUNK = 128
ROPE_BASE = 10000.0
N_EXPERTS = 64
N_GROUPS = 8
TOPK_GROUPS = 4
TOP_K = 8
D_EXPERT = D_MODEL // 4
D_SHARED = D_MODEL // 4
ROUTED_SCALE = 2.5
MOE_BLOCK = 128
NORM_EPS = 1e-6

kernel_name = "hybrid_diffattn_retention_moe_dit_layer"


def rmsnorm(x, g):
    xf = x.astype(jnp.float32)
    y = xf * lax.rsqrt(jnp.mean(xf * xf, axis=-1, keepdims=True) + NORM_EPS)
    return (y * g.astype(jnp.float32)).astype(x.dtype)


def modulate(h, shift, scale):
    return h * (1 + scale[:, None]) + shift[:, None]


def axial_rope_tables(n_tokens):
    n_rows = n_tokens // GRID_W
    rows = jnp.broadcast_to(jnp.arange(n_rows, dtype=jnp.float32)[:, None], (n_rows, GRID_W)).reshape(-1)
    cols = jnp.broadcast_to(jnp.arange(GRID_W, dtype=jnp.float32)[None, :], (n_rows, GRID_W)).reshape(-1)
    n_freq = HEAD_DIM // 4
    inv_freq = ROPE_BASE ** (-jnp.arange(n_freq, dtype=jnp.float32) / n_freq)
    ang = jnp.stack([rows[:, None] * inv_freq, cols[:, None] * inv_freq], axis=1)
    return jnp.cos(ang), jnp.sin(ang)


def apply_axial_rope(x, cos, sin):
    n_freq = HEAD_DIM // 4
    xr = x.reshape(x.shape[:-1] + (2, 2, n_freq)).astype(jnp.float32)
    shp = (cos.shape[0],) + (1,) * (x.ndim - 3) + (2, n_freq)
    c = cos.reshape(shp)
    s = sin.reshape(shp)
    x1, x2 = xr[..., 0, :], xr[..., 1, :]
    out = jnp.stack([x1 * c - x2 * s, x1 * s + x2 * c], axis=-2)
    return out.reshape(x.shape).astype(x.dtype)


def project_heads(h, w_in):
    bn, t, _ = h.shape
    p = h @ w_in
    sizes = (A_QK, A_QK, A_V, R_QK, R_QK, R_V, R_V)
    cuts = [sum(sizes[:i + 1]) for i in range(len(sizes) - 1)]
    dq, dk, dv, rq, rk, rv, rg = jnp.split(p, cuts, axis=-1)
    return (dq.reshape(bn, t, A_HEADS, 2, HEAD_DIM),
            dk.reshape(bn, t, A_HEADS, 2, HEAD_DIM),
            dv.reshape(bn, t, A_HEADS, 2 * HEAD_DIM),
            rq.reshape(bn, t, R_HEADS, R_DK),
            rk.reshape(bn, t, R_HEADS, R_DK),
            rv.reshape(bn, t, R_HEADS, R_DV),
            rg)


def diff_attend(q, k, v, lam):
    s = jnp.einsum('bqhmd,bkhmd->bhmqk', q, k).astype(jnp.float32)
    p = jax.nn.softmax(s, axis=-1)
    a = p[:, :, 0] - lam * p[:, :, 1]
    return jnp.einsum('bhqk,bkhe->bqhe', a.astype(v.dtype), v)


def retention_chunked(q, k, v, log_gamma, state0):
    b, t, h, dk = q.shape
    dv = v.shape[-1]
    n = t // RET_CHUNK
    f32 = jnp.float32
    qc = q.astype(f32).reshape(b, n, RET_CHUNK, h, dk)
    kc = k.astype(f32).reshape(b, n, RET_CHUNK, h, dk)
    vc = v.astype(f32).reshape(b, n, RET_CHUNK, h, dv)
    lg = log_gamma.astype(f32)
    pos = jnp.arange(RET_CHUNK, dtype=f32)
    rel = pos[:, None] - pos[None, :]
    decay = jnp.where(rel[None] >= 0, jnp.exp(jnp.maximum(rel, 0.0)[None] * lg[:, None, None]), 0.0)
    scores = jnp.einsum('bnihd,bnjhd->bnhij', qc, kc) * decay
    intra = jnp.einsum('bnhij,bnjhe->bnihe', scores, vc)
    k_dec = kc * jnp.exp((RET_CHUNK - 1 - pos)[:, None] * lg)[:, :, None]
    chunk_kv = jnp.einsum('bnjhd,bnjhe->nbhde', k_dec, vc)
    g_chunk = jnp.exp(RET_CHUNK * lg)[None, :, None, None]

    def step(s, kv_i):
        return g_chunk * s + kv_i, s

    state_final, state_prev = lax.scan(step, state0.astype(f32), chunk_kv)
    q_dec = qc * jnp.exp((pos + 1)[:, None] * lg)[:, :, None]
    cross = jnp.einsum('bnihd,nbhde->bnihe', q_dec, state_prev)
    out = (intra + cross).reshape(b, t, h, dv)
    return out.astype(v.dtype), state_final


def bidir_retention(q_c, k_c, v_c, q_x, k_x, v_x, logg_f, logg_b):
    b = q_x.shape[0]
    s0 = jnp.zeros((b, R_HEADS, R_DK, R_DV), jnp.float32)
    flip = lambda a: jnp.flip(a, axis=1)
    o_cf, s_cf = retention_chunked(q_c, k_c, v_c, logg_f, s0)
    o_xf, _ = retention_chunked(q_x, k_x, v_x, logg_f, s_cf)
    o_cb, s_cb = retention_chunked(flip(q_c), flip(k_c), flip(v_c), logg_b, s0)
    o_xb, _ = retention_chunked(flip(q_x), flip(k_x), flip(v_x), logg_b, s_cb)
    return o_cf + flip(o_cb), o_xf + flip(o_xb)


def mixer_merge(att, ret, gate, lam_init, dattn_norm_g, ret_norm_g, w_out):
    bn, t = att.shape[:2]
    a = (rmsnorm(att, dattn_norm_g) * (1 - lam_init)).reshape(bn, t, A_V)
    r = jax.nn.silu(gate) * rmsnorm(ret, ret_norm_g).reshape(bn, t, R_V)
    return jnp.concatenate([a, r], axis=-1) @ w_out


def moe_ffn(h, router_w, router_bias, w1, w3, w2, sw1, sw3, sw2):
    t, d = h.shape
    scores = jax.nn.sigmoid((h @ router_w).astype(jnp.float32))
    sel = scores + router_bias.astype(jnp.float32)
    grp = sel.reshape(t, N_GROUPS, N_EXPERTS // N_GROUPS)
    grp_score = lax.top_k(grp, 2)[0].sum(-1)
    _, top_g = lax.top_k(grp_score, TOPK_GROUPS)
    gmask = jax.nn.one_hot(top_g, N_GROUPS, dtype=jnp.float32).sum(axis=1) > 0
    emask = jnp.repeat(gmask, N_EXPERTS // N_GROUPS, axis=1)
    _, top_e = lax.top_k(jnp.where(emask, sel, -jnp.inf), TOP_K)
    wts = jnp.take_along_axis(scores, top_e, axis=1)
    wts = wts / jnp.sum(wts, axis=-1, keepdims=True) * ROUTED_SCALE
    n_assign = t * TOP_K
    e_flat = top_e.reshape(n_assign)
    t_flat = jnp.repeat(jnp.arange(t, dtype=jnp.int32), TOP_K)
    w_flat = wts.reshape(n_assign)
    order = jnp.argsort(e_flat)
    e_s, t_s, w_s = e_flat[order], t_flat[order], w_flat[order]
    counts = jax.ops.segment_sum(jnp.ones((n_assign,), jnp.int32), e_flat, num_segments=N_EXPERTS)
    padded = (counts + MOE_BLOCK - 1) // MOE_BLOCK * MOE_BLOCK
    pad_end = jnp.cumsum(padded)
    pad_start = pad_end - padded
    start = jnp.cumsum(counts) - counts
    dest = pad_start[e_s] + jnp.arange(n_assign, dtype=jnp.int32) - start[e_s]
    n_pad = -(-n_assign // MOE_BLOCK) * MOE_BLOCK + N_EXPERTS * MOE_BLOCK
    n_blk = n_pad // MOE_BLOCK
    tok_buf = jnp.full((n_pad,), t, jnp.int32).at[dest].set(t_s)
    w_buf = jnp.zeros((n_pad,), jnp.float32).at[dest].set(w_s)
    blk_exp = jnp.clip(jnp.searchsorted(pad_end, jnp.arange(n_blk, dtype=jnp.int32) * MOE_BLOCK, side='right'),
                       0, N_EXPERTS - 1).astype(jnp.int32)
    h_pad = jnp.concatenate([h, jnp.zeros((1, d), h.dtype)], axis=0)

    def body(acc, inp):
        tok, wt, e = inp
        xb = h_pad[tok]
        yb = (jax.nn.silu(xb @ w1[e]) * (xb @ w3[e])) @ w2[e]
        return acc.at[tok].add((yb * wt[:, None].astype(yb.dtype)).astype(acc.dtype)), None

    acc0 = jnp.zeros((t + 1, d), h.dtype)
    acc, _ = lax.scan(body, acc0, (tok_buf.reshape(n_blk, MOE_BLOCK), w_buf.reshape(n_blk, MOE_BLOCK), blk_exp))
    shared = (jax.nn.silu(h @ sw1) * (h @ sw3)) @ sw2
    return acc[:t] + shared


def hybrid_layer(x, xc, mod_x, mod_c, cos, sin, lam_init, update_ctx,
                 norm1_g, w_in, lambda_q1, lambda_k1, lambda_q2, lambda_k2, dattn_norm_g,
                 ret_decay_fwd, ret_decay_bwd, ret_norm_g, w_out,
                 norm2_g, router_w, router_bias, exp_w1, exp_w3, exp_w2,
                 shared_w1, shared_w3, shared_w2):
    b, s, d = x.shape
    sh1, sc1, g1, sh2, sc2, g2 = jnp.split(mod_x, 6, axis=-1)
    csh1, csc1, cg1, csh2, csc2, cg2 = jnp.split(mod_c, 6, axis=-1)

    hx = modulate(rmsnorm(x, norm1_g), sh1, sc1)
    hc = modulate(rmsnorm(xc, norm1_g), csh1, csc1)
    dq_x, dk_x, dv_x, rq_x, rk_x, rv_x, rg_x = project_heads(hx, w_in)
    dq_c, dk_c, dv_c, rq_c, rk_c, rv_c, rg_c = project_heads(hc, w_in)

    dq_x = apply_axial_rope(dq_x, cos, sin) * HEAD_DIM ** -0.5
    dk_x = apply_axial_rope(dk_x, cos, sin)
    rq_x = apply_axial_rope(rq_x, cos, sin)
    rk_x = apply_axial_rope(rk_x, cos, sin) * R_DK ** -0.5
    dq_c = dq_c * HEAD_DIM ** -0.5
    rk_c = rk_c * R_DK ** -0.5

    lam = (jnp.exp(jnp.sum(lambda_q1 * lambda_k1)) - jnp.exp(jnp.sum(lambda_q2 * lambda_k2))).astype(jnp.float32) + lam_init

    k_all = jnp.concatenate([dk_c, dk_x], axis=1)
    v_all = jnp.concatenate([dv_c, dv_x], axis=1)
    n_qblk = s // Q_BLOCK
    q_blocks = dq_x.reshape(b, n_qblk, Q_BLOCK, A_HEADS, 2, HEAD_DIM).swapaxes(0, 1)
    att_x = lax.map(lambda qb: diff_attend(qb, k_all, v_all, lam), q_blocks)
    att_x = att_x.swapaxes(0, 1).reshape(b, s, A_HEADS, 2 * HEAD_DIM)

    logg_f = jax.nn.log_sigmoid(ret_decay_fwd.astype(jnp.float32))
    logg_b = jax.nn.log_sigmoid(ret_decay_bwd.astype(jnp.float32))
    ret_c, ret_x = bidir_retention(rq_c, rk_c, rv_c, rq_x, rk_x, rv_x, logg_f, logg_b)

    y_x = mixer_merge(att_x, ret_x, rg_x, lam_init, dattn_norm_g, ret_norm_g, w_out)
    x = x + g1[:, None] * y_x
    h2 = modulate(rmsnorm(x, norm2_g), sh2, sc2)
    ffn_x = moe_ffn(h2.reshape(b * s, d), router_w, router_bias, exp_w1, exp_w3, exp_w2,
                    shared_w1, shared_w3, shared_w2).reshape(b, s, d)
    x = x + g2[:, None] * ffn_x

    if update_ctx:
        att_c = diff_attend(dq_c, dk_c, dv_c, lam)
        y_c = mixer_merge(att_c, ret_c, rg_c, lam_init, dattn_norm_g, ret_norm_g, w_out)
        xc = xc + cg1[:, None] * y_c
        bc, lc, _ = xc.shape
        h2c = modulate(rmsnorm(xc, norm2_g), csh2, csc2)
        ffn_c = moe_ffn(h2c.reshape(bc * lc, d), router_w, router_bias, exp_w1, exp_w3, exp_w2,
                        shared_w1, shared_w3, shared_w2).reshape(bc, lc, d)
        xc = xc + cg2[:, None] * ffn_c
    else:
        xc = None
    return x, xc


def setup_inputs(seed: int = 0) -> dict:
    key = jax.random.key(seed)
    ks = jax.random.split(key, 32)
    f32 = jnp.float32

    def nrm(k, shape, scale):
        return jax.random.normal(k, shape, f32) * scale

    gamma0 = 1.0 - 2.0 ** (-5.0 - jnp.arange(R_HEADS, dtype=f32))
    logit0 = jnp.log(gamma0) - jnp.log1p(-gamma0)
    return {
        "x": nrm(ks[0], (BATCH, SEQ, D_MODEL), 1.0),
        "c": nrm(ks[1], (BATCH, D_MODEL), 1.0),
        "ctx": nrm(ks[2], (BATCH, CTX_LEN, D_MODEL), 1.0),
        "c_ctx": nrm(ks[3], (D_MODEL,), 1.0),
        "ada_w": nrm(ks[4], (DEPTH, D_MODEL, 6 * D_MODEL), 0.5 * D_MODEL ** -0.5),
        "ada_b": nrm(ks[5], (DEPTH, 6 * D_MODEL), 0.02),
        "norm1_g": 1.0 + nrm(ks[6], (DEPTH, D_MODEL), 0.05),
        "w_in": nrm(ks[7], (DEPTH, D_MODEL, D_IN), D_MODEL ** -0.5),
        "lambda_q1": nrm(ks[8], (DEPTH, HEAD_DIM), 0.1),
        "lambda_k1": nrm(ks[9], (DEPTH, HEAD_DIM), 0.1),
        "lambda_q2": nrm(ks[10], (DEPTH, HEAD_DIM), 0.1),
        "lambda_k2": nrm(ks[11], (DEPTH, HEAD_DIM), 0.1),
        "dattn_norm_g": 1.0 + nrm(ks[12], (DEPTH, 2 * HEAD_DIM), 0.05),
        "ret_decay_fwd": logit0[None] + nrm(ks[13], (DEPTH, R_HEADS), 0.1),
        "ret_decay_bwd": logit0[None] + nrm(ks[14], (DEPTH, R_HEADS), 0.1),
        "ret_norm_g": 1.0 + nrm(ks[15], (DEPTH, R_DV), 0.05),
        "w_out": nrm(ks[16], (DEPTH, D_MIX, D_MODEL), D_MIX ** -0.5),
        "norm2_g": 1.0 + nrm(ks[17], (DEPTH, D_MODEL), 0.05),
        "router_w": nrm(ks[18], (DEPTH, D_MODEL, N_EXPERTS), D_MODEL ** -0.5),
        "router_bias": nrm(ks[19], (DEPTH, N_EXPERTS), 0.01),
        "exp_w1": nrm(ks[20], (DEPTH, N_EXPERTS, D_MODEL, D_EXPERT), D_MODEL ** -0.5),
        "exp_w3": nrm(ks[21], (DEPTH, N_EXPERTS, D_MODEL, D_EXPERT), D_MODEL ** -0.5),
        "exp_w2": nrm(ks[22], (DEPTH, N_EXPERTS, D_EXPERT, D_MODEL), D_EXPERT ** -0.5),
        "shared_w1": nrm(ks[23], (DEPTH, D_MODEL, D_SHARED), D_MODEL ** -0.5),
        "shared_w3": nrm(ks[24], (DEPTH, D_MODEL, D_SHARED), D_MODEL ** -0.5),
        "shared_w2": nrm(ks[25], (DEPTH, D_SHARED, D_MODEL), D_SHARED ** -0.5),
        "final_norm_g": 1.0 + nrm(ks[26], (D_MODEL,), 0.05),
    }


def reference(x, c, ctx, c_ctx, ada_w, ada_b, norm1_g, w_in, lambda_q1, lambda_k1, lambda_q2, lambda_k2,
              dattn_norm_g, ret_decay_fwd, ret_decay_bwd, ret_norm_g, w_out, norm2_g, router_w, router_bias,
              exp_w1, exp_w3, exp_w2, shared_w1, shared_w3, shared_w2, final_norm_g):
    s = x.shape[1]
    cos, sin = axial_rope_tables(s)
    xc = ctx
    for l in range(DEPTH):
        lam_init = 0.8 - 0.6 * math.exp(-0.3 * l)
        mod_x = jax.nn.silu(c) @ ada_w[l] + ada_b[l]
        mod_c = (jax.nn.silu(c_ctx) @ ada_w[l] + ada_b[l])[None]
        x, xc = hybrid_layer(x, xc, mod_x, mod_c, cos, sin, lam_init, l < DEPTH - 1,
                             norm1_g[l], w_in[l], lambda_q1[l], lambda_k1[l], lambda_q2[l], lambda_k2[l],
                             dattn_norm_g[l], ret_decay_fwd[l], ret_decay_bwd[l], ret_norm_g[l], w_out[l],
                             norm2_g[l], router_w[l], router_bias[l], exp_w1[l], exp_w3[l], exp_w2[l],
                             shared_w1[l], shared_w3[l], shared_w2[l])
    return rmsnorm(x, final_norm_g)
```

```python
import functools
import math

import jax
import jax.numpy as jnp
from jax import lax
from jax.experimental import pallas as pl
from jax.experimental.pallas import tpu as pltpu

F32 = jnp.float32
BF16 = jnp.bfloat16
U32 = jnp.uint32

D_MODEL = 1024
GRID_W = 64
HEAD_DIM = 64
A_HEADS = 4
R_HEADS = 4
R_DK = 64
R_DV = 128
RET_CHUNK = 128
ROPE_BASE = 10000.0
N_EXPERTS = 64
N_GROUPS = 8
TOPK_GROUPS = 4
TOP_K = 8
D_EXPERT = 256
ROUTED_SCALE = 2.5
NORM_EPS = 1e-6
LANES = 128

P_Q = 0
P_KZ = 512
P_VX = 1536
P_RQ = 2560
P_RK = 2816
P_RV = 3072
P_RG = 3584
P_COLS = 4096

VMEM_LIMIT = 56 * 1024 * 1024


def _cparams(sem):
    return pltpu.CompilerParams(dimension_semantics=sem, vmem_limit_bytes=VMEM_LIMIT)


def _mod_kernel(c_ref, w_ref, b_ref, o_ref):
    c = c_ref[...]
    s = c * jax.nn.sigmoid(c)
    o_ref[...] = jnp.dot(s, w_ref[...], preferred_element_type=F32,
                         precision=lax.Precision.HIGHEST) + b_ref[...]


def _modulation(c8, ada_w, ada_b):
    d = c8.shape[1]
    n = ada_w.shape[1]
    tn = 512
    return pl.pallas_call(
        _mod_kernel,
        grid=(n // tn,),
        in_specs=[pl.BlockSpec((8, d), lambda j: (0, 0)),
                  pl.BlockSpec((d, tn), lambda j: (0, j)),
                  pl.BlockSpec((1, tn), lambda j: (0, j))],
        out_specs=pl.BlockSpec((8, tn), lambda j: (0, j)),
        out_shape=jax.ShapeDtypeStruct((8, n), F32),
        compiler_params=_cparams(("arbitrary",)),
        name="modulation",
    )(c8, ada_w, ada_b.reshape(1, n))


def _rmsnorm_rows(x, g):
    return x * lax.rsqrt(jnp.mean(x * x, axis=-1, keepdims=True) + NORM_EPS) * g


def _inproj_kernel(x_ref, g_ref, sc_ref, sh_ref, cos_ref, sin_ref, w_ref, o_ref):
    x = x_ref[0]
    tm = x.shape[0]
    h = _rmsnorm_rows(x, g_ref[...]) * (1.0 + sc_ref[0]) + sh_ref[0]
    p = jnp.dot(h.astype(BF16), w_ref[...], preferred_element_type=F32)
    cos = cos_ref[...]
    sin = sin_ref[...]
    lane = lax.broadcasted_iota(jnp.int32, (tm, LANES), 1)
    first_half = (lane % 32) < 16
    low64 = lane < 64

    def rope(xs):
        partner = jnp.where(first_half, pltpu.roll(xs, LANES - 16, 1), pltpu.roll(xs, 16, 1))
        return xs * cos + partner * sin

    zero = jnp.zeros((tm, LANES), BF16)
    ones_col = jnp.where(lane == 0, 1.0, 0.0).astype(BF16)
    for hh in range(A_HEADS):
        q = rope(p[:, hh * 128:(hh + 1) * 128]) * (HEAD_DIM ** -0.5)
        o_ref[0, :, P_Q + hh * 128:P_Q + (hh + 1) * 128] = q.astype(BF16)
        k = rope(p[:, 512 + hh * 128:512 + (hh + 1) * 128]).astype(BF16)
        o_ref[0, :, P_KZ + hh * 256:P_KZ + hh * 256 + 128] = jnp.where(low64, k, zero)
        o_ref[0, :, P_KZ + hh * 256 + 128:P_KZ + (hh + 1) * 256] = jnp.where(low64, zero, k)
        v = p[:, 1024 + hh * 128:1024 + (hh + 1) * 128]
        o_ref[0, :, P_VX + hh * 256:P_VX + hh * 256 + 128] = v.astype(BF16)
        o_ref[0, :, P_VX + hh * 256 + 128:P_VX + (hh + 1) * 256] = ones_col
    for cc in range(2):
        rq = rope(p[:, 1536 + cc * 128:1536 + (cc + 1) * 128])
        o_ref[0, :, P_RQ + cc * 128:P_RQ + (cc + 1) * 128] = rq.astype(BF16)
        rk = rope(p[:, 1792 + cc * 128:1792 + (cc + 1) * 128]) * (R_DK ** -0.5)
        o_ref[0, :, P_RK + cc * 128:P_RK + (cc + 1) * 128] = rk.astype(BF16)
    o_ref[0, :, P_RV:P_RV + 512] = p[:, 2048:2560].astype(BF16)
    o_ref[0, :, P_RG:P_RG + 512] = p[:, 2560:3072].astype(BF16)


def _inproj(x, g, sc, sh, cos, sin, w_bf, tm):
    b, s, d = x.shape
    n_in = w_bf.shape[1]
    return pl.pallas_call(
        _inproj_kernel,
        grid=(b, s // tm),
        in_specs=[pl.BlockSpec((1, tm, d), lambda bi, i: (bi, i, 0)),
                  pl.BlockSpec((1, d), lambda bi, i: (0, 0)),
                  pl.BlockSpec((1, 1, d), lambda bi, i: (bi, 0, 0)),
                  pl.BlockSpec((1, 1, d), lambda bi, i: (bi, 0, 0)),
                  pl.BlockSpec((tm, LANES), lambda bi, i: (i, 0)),
                  pl.BlockSpec((tm, LANES), lambda bi, i: (i, 0)),
                  pl.BlockSpec((d, n_in), lambda bi, i: (0, 0))],
        out_specs=pl.BlockSpec((1, tm, P_COLS), lambda bi, i: (bi, i, 0)),
        out_shape=jax.ShapeDtypeStruct((b, s, P_COLS), BF16),
        compiler_params=_cparams(("arbitrary", "arbitrary")),
        name="inproj",
    )(x, g, sc, sh, cos, sin, w_bf)


_NT = (((1,), (1,)), ((), ()))


def _attn_kernel(lam_ref, q_ref, kzc_ref, vxc_ref, kzx_ref, vxx_ref, o_ref, acc1, acc2, *, tkc):
    q = q_ref[0]
    tq = q.shape[0]
    n_chunks = kzx_ref.shape[1] // tkc

    def one_map(k, vx, m, acc):
        s = lax.dot_general(q, k, _NT, preferred_element_type=F32)
        m_new = jnp.maximum(m, jnp.max(s, axis=1, keepdims=True))
        alpha = jnp.exp(m - m_new)
        e = jnp.exp(s - m_new).astype(BF16)
        acc[...] = alpha * acc[...] + jnp.dot(e, vx, preferred_element_type=F32)
        return m_new

    def step(kz, vx, m1, m2):
        m1 = one_map(kz[:, :LANES], vx, m1, acc1)
        m2 = one_map(kz[:, LANES:], vx, m2, acc2)
        return m1, m2

    acc1[...] = jnp.zeros_like(acc1)
    acc2[...] = jnp.zeros_like(acc2)
    m0 = jnp.full((tq, 1), -jnp.inf, F32)
    m1, m2 = step(kzc_ref[0], vxc_ref[0], m0, m0)

    def body(c, carry):
        ks = pl.ds(pl.multiple_of(c * tkc, tkc), tkc)
        return step(kzx_ref[0, ks, :], vxx_ref[0, ks, :], *carry)

    lax.fori_loop(0, n_chunks, body, (m1, m2))
    a1 = acc1[...]
    a2 = acc2[...]
    o1 = a1[:, :LANES] / a1[:, LANES:LANES + 1]
    o2 = a2[:, :LANES] / a2[:, LANES:LANES + 1]
    o_ref[0] = (o1 - lam_ref[0] * o2).astype(BF16)


def _attention(lam, px, pc, tq, tkc):
    b, s, _ = px.shape
    lc = pc.shape[1]
    kern = functools.partial(_attn_kernel, tkc=tkc)
    return pl.pallas_call(
        kern,
        grid=(b, A_HEADS, s // tq),
        in_specs=[pl.BlockSpec(memory_space=pltpu.SMEM),
                  pl.BlockSpec((1, tq, LANES), lambda bi, h, i: (bi, i, P_Q // LANES + h)),
                  pl.BlockSpec((1, lc, 256), lambda bi, h, i: (bi, 0, P_KZ // 256 + h)),
                  pl.BlockSpec((1, lc, 256), lambda bi, h, i: (bi, 0, P_VX // 256 + h)),
                  pl.BlockSpec((1, s, 256), lambda bi, h, i: (bi, 0, P_KZ // 256 + h)),
                  pl.BlockSpec((1, s, 256), lambda bi, h, i: (bi, 0, P_VX // 256 + h))],
        out_specs=pl.BlockSpec((1, tq, LANES), lambda bi, h, i: (bi, i, h)),
        out_shape=jax.ShapeDtypeStruct((b, s, A_HEADS * LANES), BF16),
        scratch_shapes=[pltpu.VMEM((tq, 256), F32), pltpu.VMEM((tq, 256), F32)],
        compiler_params=_cparams(("arbitrary", "arbitrary", "arbitrary")),
        name="diff_attention",
    )(lam, px, pc, pc, px, px)


_TN = (((0,), (0,)), ((), ()))


def _ret_kernel(qf_ref, kf_ref, vf_ref, qb_ref, kb_ref, vb_ref, kc_ref, vc_ref,
                dmat_ref, qdec_ref, kdec_ref, gch_ref, of_ref, ob_ref, sf, sb):
    c = RET_CHUNK
    n = pl.program_id(1)
    col_head = lax.broadcasted_iota(jnp.int32, (c, R_HEADS * R_DK), 1) // R_DK
    bd_mask = (lax.broadcasted_iota(jnp.int32, (R_HEADS * R_DK, R_HEADS * R_DV), 0) // R_DK
               == lax.broadcasted_iota(jnp.int32, (R_HEADS * R_DK, R_HEADS * R_DV), 1) // R_DV)

    def state_update(s_ref, k, v, kdec, gch):
        kd = (k.astype(F32) * kdec).astype(BF16)
        kv = lax.dot_general(kd, v, _TN, preferred_element_type=F32)
        s_ref[...] = gch * s_ref[...] + jnp.where(bd_mask, kv, 0.0)

    def chunk(s_ref, q, k, v, dmat, qdec, kdec, gch):
        zero = jnp.zeros_like(k)
        kblk = jnp.concatenate([jnp.where(col_head == h, k, zero) for h in range(R_HEADS)], axis=0)
        sc = lax.dot_general(q, kblk, _NT, preferred_element_type=F32) * dmat
        p = sc.astype(BF16)
        intra = jnp.concatenate(
            [jnp.dot(p[:, h * c:(h + 1) * c], v[:, h * R_DV:(h + 1) * R_DV], preferred_element_type=F32)
             for h in range(R_HEADS)], axis=1)
        qd = (q.astype(F32) * qdec).astype(BF16)
        cross = jnp.dot(qd, s_ref[...].astype(BF16), preferred_element_type=F32)
        out = intra + cross
        state_update(s_ref, k, v, kdec, gch)
        return out

    @pl.when(n == 0)
    def _():
        sf[...] = jnp.zeros_like(sf)
        sb[...] = jnp.zeros_like(sb)
        n_ctx = kc_ref.shape[1] // c
        for cc in range(n_ctx):
            state_update(sf, kc_ref[0, cc * c:(cc + 1) * c, :], vc_ref[0, cc * c:(cc + 1) * c, :],
                         kdec_ref[0], gch_ref[0])
        for cc in reversed(range(n_ctx)):
            state_update(sb, kc_ref[0, cc * c:(cc + 1) * c, :], vc_ref[0, cc * c:(cc + 1) * c, :],
                         kdec_ref[1], gch_ref[1])

    of_ref[0] = chunk(sf, qf_ref[0], kf_ref[0], vf_ref[0], dmat_ref[0], qdec_ref[0], kdec_ref[0],
                      gch_ref[0]).astype(BF16)
    ob_ref[0] = chunk(sb, qb_ref[0], kb_ref[0], vb_ref[0], dmat_ref[1], qdec_ref[1], kdec_ref[1],
                      gch_ref[1]).astype(BF16)


def _retention(px, pc, dmat, qdec, kdec, gch):
    b, s, _ = px.shape
    lc = pc.shape[1]
    c = RET_CHUNK
    nch = s // c
    qk_w = R_HEADS * R_DK
    v_w = R_HEADS * R_DV
    fwd = lambda bi, n: (bi, n, 0)
    spec_q = lambda im: pl.BlockSpec((1, c, qk_w), lambda bi, n: im(bi, n)[:2] + (P_RQ // qk_w,))
    spec_k = lambda im: pl.BlockSpec((1, c, qk_w), lambda bi, n: im(bi, n)[:2] + (P_RK // qk_w,))
    spec_v = lambda im: pl.BlockSpec((1, c, v_w), lambda bi, n: im(bi, n)[:2] + (P_RV // v_w,))
    bwd = lambda bi, n: (bi, nch - 1 - n, 0)
    const3 = lambda shape: pl.BlockSpec(shape, lambda bi, n: (0, 0, 0))
    return pl.pallas_call(
        _ret_kernel,
        grid=(b, nch),
        in_specs=[spec_q(fwd), spec_k(fwd), spec_v(fwd), spec_q(bwd), spec_k(bwd), spec_v(bwd),
                  pl.BlockSpec((1, lc, qk_w), lambda bi, n: (bi, 0, P_RK // qk_w)),
                  pl.BlockSpec((1, lc, v_w), lambda bi, n: (bi, 0, P_RV // v_w)),
                  const3((2, c, R_HEADS * c)), const3((2, c, qk_w)), const3((2, c, qk_w)),
                  const3((2, 1, v_w))],
        out_specs=[pl.BlockSpec((1, c, v_w), fwd), pl.BlockSpec((1, c, v_w), bwd)],
        out_shape=[jax.ShapeDtypeStruct((b, s, v_w), BF16), jax.ShapeDtypeStruct((b, s, v_w), BF16)],
        scratch_shapes=[pltpu.VMEM((qk_w, v_w), F32), pltpu.VMEM((qk_w, v_w), F32)],
        compiler_params=_cparams(("arbitrary", "arbitrary")),
        name="retention",
    )(px, px, px, px, px, px, pc, pc, dmat, qdec, kdec, gch)


def _retention_tables(decay_fwd, decay_bwd):
    c = RET_CHUNK
    pos = jnp.arange(c, dtype=F32)
    rel = pos[:, None] - pos[None, :]

    def tables(logits, backward):
        lg = jax.nn.log_sigmoid(logits.astype(F32))
        r = -rel if backward else rel
        dm = jnp.where(r[None] >= 0, jnp.exp(jnp.maximum(r, 0.0)[None] * lg[:, None, None]), 0.0)
        dm = jnp.transpose(dm, (1, 0, 2)).reshape(c, R_HEADS * c)
        qpow = (c - pos) if backward else (pos + 1)
        kpow = pos if backward else (c - 1 - pos)
        qd = jnp.repeat(jnp.exp(qpow[:, None] * lg[None, :]), R_DK, axis=1)
        kd = jnp.repeat(jnp.exp(kpow[:, None] * lg[None, :]), R_DK, axis=1)
        gc = jnp.repeat(jnp.exp(c * lg), R_DV)[None, :]
        return dm, qd, kd, gc

    tf = tables(decay_fwd, False)
    tb = tables(decay_bwd, True)
    return tuple(jnp.stack([a, bb]) for a, bb in zip(tf, tb))


def _pack_pairs(x):
    k = x.shape[1] // 2
    lo = lax.bitcast_convert_type(x[:, :k].astype(BF16).astype(F32), U32)
    hi = lax.bitcast_convert_type(x[:, k:].astype(BF16).astype(F32), U32)
    return (lo >> 16) | (hi & jnp.uint32(0xFFFF0000))


def _unpack_pairs(w):
    lo = lax.bitcast_convert_type(w << 16, F32)
    hi = lax.bitcast_convert_type(w & jnp.uint32(0xFFFF0000), F32)
    return lo, hi


def _merge_kernel(att_ref, rf_ref, rb_ref, gate_ref, x_ref, g1_ref, sh2_ref, sc2_ref, ga_ref, gr_ref,
                  wout_ref, n2g_ref, rw_ref, x1_ref, h2p_ref, lg_ref, *, att_scale):
    att = att_ref[0].astype(F32)
    ret = rf_ref[0].astype(F32) + rb_ref[0].astype(F32)
    gate = gate_ref[0].astype(F32)
    parts = []
    for h in range(A_HEADS):
        a = att[:, h * 128:(h + 1) * 128]
        parts.append(_rmsnorm_rows(a, ga_ref[...]) * att_scale)
    for h in range(R_HEADS):
        r = _rmsnorm_rows(ret[:, h * R_DV:(h + 1) * R_DV], gr_ref[...])
        g = gate[:, h * R_DV:(h + 1) * R_DV]
        parts.append(g * jax.nn.sigmoid(g) * r)
    merged = jnp.concatenate(parts, axis=1).astype(BF16)
    y = jnp.dot(merged, wout_ref[...], preferred_element_type=F32)
    x1 = x_ref[0] + g1_ref[0] * y
    x1_ref[0] = x1
    h2 = _rmsnorm_rows(x1, n2g_ref[...]) * (1.0 + sc2_ref[0]) + sh2_ref[0]
    lg_ref[0] = jnp.dot(h2.astype(BF16), rw_ref[...], preferred_element_type=F32)
    h2p_ref[0] = _pack_pairs(h2)


def _merge(att, rf, rb, px, x, g1, sh2, sc2, ga, gr, wout_bf, n2g, rw_bf, att_scale, tm):
    b, s, d = x.shape
    row = lambda bi, i: (bi, i, 0)
    per_b = pl.BlockSpec((1, 1, d), lambda bi, i: (bi, 0, 0))
    const2 = lambda shape: pl.BlockSpec(shape, lambda bi, i: (0, 0))
    return pl.pallas_call(
        functools.partial(_merge_kernel, att_scale=att_scale),
        grid=(b, s // tm),
        in_specs=[pl.BlockSpec((1, tm, 512), row), pl.BlockSpec((1, tm, 512), row),
                  pl.BlockSpec((1, tm, 512), row),
                  pl.BlockSpec((1, tm, 512), lambda bi, i: (bi, i, P_RG // 512)),
                  pl.BlockSpec((1, tm, d), row), per_b, per_b, per_b,
                  const2((1, 128)), const2((1, R_DV)), const2((d, d)), const2((1, d)),
                  const2((d, N_EXPERTS))],
        out_specs=[pl.BlockSpec((1, tm, d), row), pl.BlockSpec((1, tm, d // 2), row),
                   pl.BlockSpec((1, tm, N_EXPERTS), row)],
        out_shape=[jax.ShapeDtypeStruct((b, s, d), F32), jax.ShapeDtypeStruct((b, s, d // 2), U32),
                   jax.ShapeDtypeStruct((b, s, N_EXPERTS), F32)],
        compiler_params=_cparams(("arbitrary", "arbitrary")),
        name="mixer_merge",
    )(att, rf, rb, px, x, g1, sh2, sc2, ga, gr, wout_bf, n2g, rw_bf)


def _dispatch_kernel(dest_ref, h_ref, xs_ref, buf, sem):
    tm = h_ref.shape[0]
    i = pl.program_id(0)
    n = pl.num_programs(0)
    slot = i % 2

    def wait_slot(s):
        for _ in range(TOP_K):
            pltpu.make_async_copy(buf.at[s], xs_ref.at[pl.ds(0, tm)], sem.at[s]).wait()

    @pl.when(i >= 2)
    def _():
        wait_slot(slot)

    buf[slot] = h_ref[...]

    def body(t, carry):
        for j in range(TOP_K):
            d = dest_ref[0, 0, t * TOP_K + j]
            pltpu.make_async_copy(buf.at[slot, pl.ds(t, 1)], xs_ref.at[pl.ds(d, 1)], sem.at[slot]).start()
        return carry

    lax.fori_loop(0, tm, body, 0)

    @pl.when(i == n - 1)
    def _():
        wait_slot(slot)

    @pl.when(jnp.logical_and(i == n - 1, n >= 2))
    def _():
        wait_slot(1 - slot)


def _dispatch(dest, h2p, tm):
    t, kw = h2p.shape
    n = t // tm
    return pl.pallas_call(
        _dispatch_kernel,
        grid=(n,),
        in_specs=[pl.BlockSpec((1, 1, tm * TOP_K), lambda i: (i, 0, 0), memory_space=pltpu.SMEM),
                  pl.BlockSpec((tm, kw), lambda i: (i, 0))],
        out_specs=pl.BlockSpec(memory_space=pl.ANY),
        out_shape=jax.ShapeDtypeStruct((t * TOP_K, kw), U32),
        scratch_shapes=[pltpu.VMEM((2, tm, kw), U32), pltpu.SemaphoreType.DMA((2,))],
        compiler_params=_cparams(("arbitrary",)),
        name="moe_dispatch",
    )(dest.reshape(n, 1, tm * TOP_K), h2p)


def _expert_kernel(iblk_ref, iexp_ref, ilo_ref, ihi_ref, xs_ref, w1_ref, w3_ref, w2_ref, ys_ref, acc):
    del iexp_ref
    r = xs_ref.shape[0]
    i = pl.program_id(0)
    blk = iblk_ref[i]
    lo = ilo_ref[i]
    hi = ihi_ref[i]
    prev_blk = iblk_ref[jnp.maximum(i - 1, 0)]
    first = jnp.logical_or(i == 0, prev_blk != blk)
    rows = blk * r + lax.broadcasted_iota(jnp.int32, (r, 1), 0)
    valid = jnp.logical_and(rows >= lo, rows < hi)
    xlo, xhi = _unpack_pairs(xs_ref[...])
    x = jnp.where(valid, jnp.concatenate([xlo, xhi], axis=1), 0.0).astype(BF16)
    a = jnp.dot(x, w1_ref[0], preferred_element_type=F32)
    bb = jnp.dot(x, w3_ref[0], preferred_element_type=F32)
    hmid = (a * jax.nn.sigmoid(a) * bb).astype(BF16)
    y = jnp.dot(hmid, w2_ref[0], preferred_element_type=F32)

    @pl.when(first)
    def _():
        acc[...] = y

    @pl.when(jnp.logical_not(first))
    def _():
        acc[...] += y

    ys_ref[...] = _pack_pairs(acc[...])


def _experts(iblk, iexp, ilo, ihi, xs, w1, w3, w2, r):
    na, kw = xs.shape
    n_items = iblk.shape[0]
    d = w1.shape[1]
    de = w1.shape[2]
    grid_spec = pltpu.PrefetchScalarGridSpec(
        num_scalar_prefetch=4,
        grid=(n_items,),
        in_specs=[pl.BlockSpec((r, kw), lambda i, ib, ie, il, ih: (ib[i], 0)),
                  pl.BlockSpec((1, d, de), lambda i, ib, ie, il, ih: (ie[i], 0, 0)),
                  pl.BlockSpec((1, d, de), lambda i, ib, ie, il, ih: (ie[i], 0, 0)),
                  pl.BlockSpec((1, de, d), lambda i, ib, ie, il, ih: (ie[i], 0, 0))],
        out_specs=pl.BlockSpec((r, kw), lambda i, ib, ie, il, ih: (ib[i], 0)),
        scratch_shapes=[pltpu.VMEM((r, d), F32)],
    )
    return pl.pallas_call(
        _expert_kernel,
        grid_spec=grid_spec,
        out_shape=jax.ShapeDtypeStruct((na, kw), U32),
        compiler_params=_cparams(("arbitrary",)),
        name="moe_experts",
    )(iblk, iexp, ilo, ihi, xs, w1, w3, w2)


def _combine_kernel(dest_ref, dnext_ref, x1_ref, h2p_ref, wts_ref, g2_ref, sw1_ref, sw3_ref, sw2_ref,
                    fg_ref, ys_ref, o_ref, gbuf, sem):
    tm = x1_ref.shape[0]
    i = pl.program_id(0)
    n = pl.num_programs(0)
    slot = i % 2

    def issue(dref, s):
        def body(t, carry):
            for j in range(TOP_K):
                d = dref[0, 0, t * TOP_K + j]
                pltpu.make_async_copy(ys_ref.at[pl.ds(d, 1)], gbuf.at[s, pl.ds(j * tm + t, 1)],
                                      sem.at[s]).start()
            return carry
        lax.fori_loop(0, tm, body, 0)

    @pl.when(i == 0)
    def _():
        issue(dest_ref, 0)

    @pl.when(i + 1 < n)
    def _():
        issue(dnext_ref, 1 - slot)

    for _ in range(TOP_K):
        pltpu.make_async_copy(ys_ref.at[pl.ds(0, tm)], gbuf.at[slot, pl.ds(0, tm)], sem.at[slot]).wait()

    wts = wts_ref[...]
    moe_lo = None
    moe_hi = None
    for j in range(TOP_K):
        ylo, yhi = _unpack_pairs(gbuf[slot, j * tm:(j + 1) * tm, :])
        wj = wts[:, j:j + 1]
        moe_lo = ylo * wj if moe_lo is None else moe_lo + ylo * wj
        moe_hi = yhi * wj if moe_hi is None else moe_hi + yhi * wj
    hlo, hhi = _unpack_pairs(h2p_ref[...])
    h = jnp.concatenate([hlo, hhi], axis=1).astype(BF16)
    a = jnp.dot(h, sw1_ref[...], preferred_element_type=F32)
    bb = jnp.dot(h, sw3_ref[...], preferred_element_type=F32)
    shared = jnp.dot((a * jax.nn.sigmoid(a) * bb).astype(BF16), sw2_ref[...], preferred_element_type=F32)
    ffn = jnp.concatenate([moe_lo, moe_hi], axis=1) + shared
    xo = x1_ref[...] + g2_ref[0] * ffn
    o_ref[...] = _rmsnorm_rows(xo, fg_ref[...])


def _combine(dest, x1, h2p, wts, g2, sw1, sw3, sw2, fg, ys, tiles_per_batch, tm):
    t, d = x1.shape
    kw = d // 2
    n = t // tm
    dest3 = dest.reshape(n, 1, tm * TOP_K)
    de = sw1.shape[1]
    const2 = lambda shape: pl.BlockSpec(shape, lambda i: (0, 0))
    return pl.pallas_call(
        _combine_kernel,
        grid=(n,),
        in_specs=[pl.BlockSpec((1, 1, tm * TOP_K), lambda i: (i, 0, 0), memory_space=pltpu.SMEM),
                  pl.BlockSpec((1, 1, tm * TOP_K), lambda i: (jnp.minimum(i + 1, n - 1), 0, 0),
                               memory_space=pltpu.SMEM),
                  pl.BlockSpec((tm, d), lambda i: (i, 0)),
                  pl.BlockSpec((tm, kw), lambda i: (i, 0)),
                  pl.BlockSpec((tm, TOP_K), lambda i: (i, 0)),
                  pl.BlockSpec((1, 1, d), lambda i: (i // tiles_per_batch, 0, 0)),
                  const2((d, de)), const2((d, de)), const2((de, d)), const2((1, d)),
                  pl.BlockSpec(memory_space=pl.ANY)],
        out_specs=pl.BlockSpec((tm, d), lambda i: (i, 0)),
        out_shape=jax.ShapeDtypeStruct((t, d), F32),
        scratch_shapes=[pltpu.VMEM((2, tm * TOP_K, kw), U32), pltpu.SemaphoreType.DMA((2,))],
        compiler_params=_cparams(("arbitrary",)),
        name="moe_combine",
    )(dest3, dest3, x1, h2p, wts, g2, sw1, sw3, sw2, fg, ys)


def _route(logits, router_bias, block_rows):
    t = logits.shape[0]
    scores = jax.nn.sigmoid(logits)
    sel = scores + router_bias.astype(F32)
    per_group = N_EXPERTS // N_GROUPS
    grp_score = lax.top_k(sel.reshape(t, N_GROUPS, per_group), 2)[0].sum(-1)
    _, top_g = lax.top_k(grp_score, TOPK_GROUPS)
    gmask = jax.nn.one_hot(top_g, N_GROUPS, dtype=F32).sum(axis=1) > 0
    emask = jnp.repeat(gmask, per_group, axis=1)
    _, top_e = lax.top_k(jnp.where(emask, sel, -jnp.inf), TOP_K)
    wts = jnp.take_along_axis(scores, top_e, axis=1)
    wts = wts / jnp.sum(wts, axis=-1, keepdims=True) * ROUTED_SCALE

    onehot = jax.nn.one_hot(top_e, N_EXPERTS, dtype=jnp.int32).sum(axis=1)
    csum = jnp.cumsum(onehot, axis=0)
    rank = csum - onehot
    counts = csum[-1]
    ends = jnp.cumsum(counts)
    starts = ends - counts
    dest = (starts[top_e] + jnp.take_along_axis(rank, top_e, axis=1)).astype(jnp.int32)

    n_blk = t * TOP_K // block_rows
    n_items = n_blk + N_EXPERTS - 1
    first_blk = starts // block_rows
    last_blk = (ends - 1) // block_rows
    n_e = jnp.where(counts > 0, last_blk - first_blk + 1, 0)
    item_end = jnp.cumsum(n_e)
    item_off = item_end - n_e
    total = item_end[-1]
    idx = jnp.arange(n_items, dtype=jnp.int32)
    e_of = jnp.minimum(jnp.searchsorted(item_end, idx, side="right"), N_EXPERTS - 1).astype(jnp.int32)
    live = idx < total
    blk = first_blk[e_of] + (idx - item_off[e_of])
    lo = jnp.maximum(starts[e_of], blk * block_rows)
    hi = jnp.minimum(ends[e_of], (blk + 1) * block_rows)
    last_e = e_of[jnp.maximum(total - 1, 0)]
    iblk = jnp.where(live, blk, n_blk - 1).astype(jnp.int32)
    iexp = jnp.where(live, e_of, last_e).astype(jnp.int32)
    ilo = jnp.where(live, lo, 0).astype(jnp.int32)
    ihi = jnp.where(live, hi, 0).astype(jnp.int32)
    return wts, dest, iblk, iexp, ilo, ihi


def _rope_tables(n_tokens):
    n_rows = n_tokens // GRID_W
    rows = jnp.repeat(jnp.arange(n_rows, dtype=F32), GRID_W)
    cols = jnp.tile(jnp.arange(GRID_W, dtype=F32), n_rows)
    n_freq = HEAD_DIM // 4
    inv_freq = ROPE_BASE ** (-jnp.arange(n_freq, dtype=F32) / n_freq)
    ar = rows[:, None] * inv_freq
    ac = cols[:, None] * inv_freq
    cos64 = jnp.concatenate([jnp.cos(ar), jnp.cos(ar), jnp.cos(ac), jnp.cos(ac)], axis=1)
    sin64 = jnp.concatenate([-jnp.sin(ar), jnp.sin(ar), -jnp.sin(ac), jnp.sin(ac)], axis=1)
    return jnp.tile(cos64, (1, 2)), jnp.tile(sin64, (1, 2))


def kernel(x, c, ctx, c_ctx, ada_w, ada_b, norm1_g, w_in, lambda_q1, lambda_k1, lambda_q2, lambda_k2,
           dattn_norm_g, ret_decay_fwd, ret_decay_bwd, ret_norm_g, w_out, norm2_g, router_w, router_bias,
           exp_w1, exp_w3, exp_w2, shared_w1, shared_w3, shared_w2, final_norm_g):
    b, s, d = x.shape
    lc = ctx.shape[1]
    depth = ada_w.shape[0]
    assert depth == 1, "single-layer problem: the context stream is never updated"
    l = 0
    lam_init = 0.8 - 0.6 * math.exp(-0.3 * l)

    c8 = jnp.zeros((8, d), F32).at[:b].set(c).at[b].set(c_ctx)
    mod = _modulation(c8, ada_w[l], ada_b[l])
    sh1, sc1, g1, sh2, sc2, g2 = [mod[:b, k * d:(k + 1) * d].reshape(b, 1, d) for k in range(6)]
    csh1 = jnp.broadcast_to(mod[b, 0:d].reshape(1, 1, d), (b, 1, d))
    csc1 = jnp.broadcast_to(mod[b, d:2 * d].reshape(1, 1, d), (b, 1, d))

    w_in_bf = w_in[l].astype(BF16)
    n1g = norm1_g[l].reshape(1, d)
    cos, sin = _rope_tables(s)
    px = _inproj(x, n1g, sc1, sh1, cos, sin, w_in_bf, tm=512)
    pc = _inproj(ctx, n1g, csc1, csh1, jnp.ones((lc, LANES), F32), jnp.zeros((lc, LANES), F32),
                 w_in_bf, tm=lc)

    lam = (jnp.exp(jnp.sum(lambda_q1[l] * lambda_k1[l])) - jnp.exp(jnp.sum(lambda_q2[l] * lambda_k2[l]))
           ).astype(F32) + lam_init
    att = _attention(lam.reshape(1), px, pc, tq=512, tkc=512)

    dmat, qdec, kdec, gch = _retention_tables(ret_decay_fwd[l], ret_decay_bwd[l])
    ret_f, ret_b = _retention(px, pc, dmat, qdec, kdec, gch)

    x1, h2p, logits = _merge(att, ret_f, ret_b, px, x, g1, sh2, sc2,
                             dattn_norm_g[l].reshape(1, 2 * HEAD_DIM), ret_norm_g[l].reshape(1, R_DV),
                             w_out[l].astype(BF16), norm2_g[l].reshape(1, d), router_w[l].astype(BF16),
                             1.0 - lam_init, tm=512)

    t = b * s
    block_rows = 256
    wts, dest, iblk, iexp, ilo, ihi = _route(logits.reshape(t, N_EXPERTS), router_bias[l], block_rows)
    h2p = h2p.reshape(t, d // 2)
    dest_flat = dest.reshape(t * TOP_K)
    xs = _dispatch(dest_flat, h2p, tm=128)
    ys = _experts(iblk, iexp, ilo, ihi, xs, exp_w1[l].astype(BF16), exp_w3[l].astype(BF16),
                  exp_w2[l].astype(BF16), block_rows)
    tm_c = 128
    out = _combine(dest_flat, x1.reshape(t, d), h2p, wts, g2, shared_w1[l].astype(BF16),
                   shared_w3[l].astype(BF16), shared_w2[l].astype(BF16), final_norm_g.reshape(1, d),
                   ys, s // tm_c, tm_c)
    return out.reshape(b, s, d)
```

```python
import functools
import math

import jax
import jax.numpy as jnp
from jax import lax
from jax.experimental import pallas as pl
from jax.experimental.pallas import tpu as pltpu

F32 = jnp.float32
BF16 = jnp.bfloat16
U32 = jnp.uint32

D_MODEL = 1024
GRID_W = 64
HEAD_DIM = 64
A_HEADS = 4
R_HEADS = 4
R_DK = 64
R_DV = 128
RET_CHUNK = 128
ROPE_BASE = 10000.0
N_EXPERTS = 64
N_GROUPS = 8
TOPK_GROUPS = 4
TOP_K = 8
D_EXPERT = 256
ROUTED_SCALE = 2.5
NORM_EPS = 1e-6
LANES = 128

P_Q = 0
P_KZ = 512
P_VX = 1536
P_RQ = 2560
P_RK = 2816
P_RV = 3072
P_RG = 3584
P_COLS = 4096

VMEM_LIMIT = 56 * 1024 * 1024


def _cparams(sem):
    return pltpu.CompilerParams(dimension_semantics=sem, vmem_limit_bytes=VMEM_LIMIT)


def _mod_kernel(c_ref, w_ref, b_ref, o_ref):
    c = c_ref[...]
    s = c * jax.nn.sigmoid(c)
    o_ref[...] = jnp.dot(s, w_ref[...], preferred_element_type=F32,
                         precision=lax.Precision.HIGHEST) + b_ref[...]


def _modulation(c8, ada_w, ada_b):
    d = c8.shape[1]
    n = ada_w.shape[1]
    tn = 512
    return pl.pallas_call(
        _mod_kernel,
        grid=(n // tn,),
        in_specs=[pl.BlockSpec((8, d), lambda j: (0, 0)),
                  pl.BlockSpec((d, tn), lambda j: (0, j)),
                  pl.BlockSpec((1, tn), lambda j: (0, j))],
        out_specs=pl.BlockSpec((8, tn), lambda j: (0, j)),
        out_shape=jax.ShapeDtypeStruct((8, n), F32),
        compiler_params=_cparams(("arbitrary",)),
        name="modulation",
    )(c8, ada_w, ada_b.reshape(1, n))


def _rmsnorm_rows(x, g):
    return x * lax.rsqrt(jnp.mean(x * x, axis=-1, keepdims=True) + NORM_EPS) * g


def _inproj_kernel(x_ref, g_ref, sc_ref, sh_ref, cos_ref, sin_ref, w_ref, o_ref):
    x = x_ref[0]
    tm = x.shape[0]
    h = _rmsnorm_rows(x, g_ref[...]) * (1.0 + sc_ref[0]) + sh_ref[0]
    p = jnp.dot(h.astype(BF16), w_ref[...], preferred_element_type=F32)
    cos = cos_ref[...]
    sin = sin_ref[...]
    lane = lax.broadcasted_iota(jnp.int32, (tm, LANES), 1)
    first_half = (lane % 32) < 16
    low64 = lane < 64

    def rope(xs):
        partner = jnp.where(first_half, pltpu.roll(xs, LANES - 16, 1), pltpu.roll(xs, 16, 1))
        return xs * cos + partner * sin

    zero = jnp.zeros((tm, LANES), BF16)
    ones_col = jnp.where(lane == 0, 1.0, 0.0).astype(BF16)
    for hh in range(A_HEADS):
        q = rope(p[:, hh * 128:(hh + 1) * 128]) * (HEAD_DIM ** -0.5)
        o_ref[0, :, P_Q + hh * 128:P_Q + (hh + 1) * 128] = q.astype(BF16)
        k = rope(p[:, 512 + hh * 128:512 + (hh + 1) * 128]).astype(BF16)
        o_ref[0, :, P_KZ + hh * 256:P_KZ + hh * 256 + 128] = jnp.where(low64, k, zero)
        o_ref[0, :, P_KZ + hh * 256 + 128:P_KZ + (hh + 1) * 256] = jnp.where(low64, zero, k)
        v = p[:, 1024 + hh * 128:1024 + (hh + 1) * 128]
        o_ref[0, :, P_VX + hh * 256:P_VX + hh * 256 + 128] = v.astype(BF16)
        o_ref[0, :, P_VX + hh * 256 + 128:P_VX + (hh + 1) * 256] = ones_col
    for cc in range(2):
        rq = rope(p[:, 1536 + cc * 128:1536 + (cc + 1) * 128])
        o_ref[0, :, P_RQ + cc * 128:P_RQ + (cc + 1) * 128] = rq.astype(BF16)
        rk = rope(p[:, 1792 + cc * 128:1792 + (cc + 1) * 128]) * (R_DK ** -0.5)
        o_ref[0, :, P_RK + cc * 128:P_RK + (cc + 1) * 128] = rk.astype(BF16)
    o_ref[0, :, P_RV:P_RV + 512] = p[:, 2048:2560].astype(BF16)
    o_ref[0, :, P_RG:P_RG + 512] = p[:, 2560:3072].astype(BF16)


def _inproj(x, g, sc, sh, cos, sin, w_bf, tm):
    b, s, d = x.shape
    n_in = w_bf.shape[1]
    return pl.pallas_call(
        _inproj_kernel,
        grid=(b, s // tm),
        in_specs=[pl.BlockSpec((1, tm, d), lambda bi, i: (bi, i, 0)),
                  pl.BlockSpec((1, d), lambda bi, i: (0, 0)),
                  pl.BlockSpec((1, 1, d), lambda bi, i: (bi, 0, 0)),
                  pl.BlockSpec((1, 1, d), lambda bi, i: (bi, 0, 0)),
                  pl.BlockSpec((tm, LANES), lambda bi, i: (i, 0)),
                  pl.BlockSpec((tm, LANES), lambda bi, i: (i, 0)),
                  pl.BlockSpec((d, n_in), lambda bi, i: (0, 0))],
        out_specs=pl.BlockSpec((1, tm, P_COLS), lambda bi, i: (bi, i, 0)),
        out_shape=jax.ShapeDtypeStruct((b, s, P_COLS), BF16),
        compiler_params=_cparams(("arbitrary", "arbitrary")),
        name="inproj",
    )(x, g, sc, sh, cos, sin, w_bf)


_NT = (((1,), (1,)), ((), ()))


def _attn_kernel(lam_ref, q_ref, kzc_ref, vxc_ref, kzx_ref, vxx_ref, o_ref, acc1, acc2, *, tkc):
    q = q_ref[0]
    tq = q.shape[0]
    n_chunks = kzx_ref.shape[1] // tkc

    def one_map(k, vx, m, acc):
        s = lax.dot_general(q, k, _NT, preferred_element_type=F32)
        m_new = jnp.maximum(m, jnp.max(s, axis=1, keepdims=True))
        alpha = jnp.exp(m - m_new)
        e = jnp.exp(s - m_new).astype(BF16)
        acc[...] = alpha * acc[...] + jnp.dot(e, vx, preferred_element_type=F32)
        return m_new

    def step(kz, vx, m1, m2):
        m1 = one_map(kz[:, :LANES], vx, m1, acc1)
        m2 = one_map(kz[:, LANES:], vx, m2, acc2)
        return m1, m2

    acc1[...] = jnp.zeros_like(acc1)
    acc2[...] = jnp.zeros_like(acc2)
    m0 = jnp.full((tq, 1), -jnp.inf, F32)
    m1, m2 = step(kzc_ref[0], vxc_ref[0], m0, m0)

    def body(c, carry):
        ks = pl.ds(pl.multiple_of(c * tkc, tkc), tkc)
        return step(kzx_ref[0, ks, :], vxx_ref[0, ks, :], *carry)

    lax.fori_loop(0, n_chunks, body, (m1, m2))
    a1 = acc1[...]
    a2 = acc2[...]
    o1 = a1[:, :LANES] / a1[:, LANES:LANES + 1]
    o2 = a2[:, :LANES] / a2[:, LANES:LANES + 1]
    o_ref[0] = (o1 - lam_ref[0] * o2).astype(BF16)


def _attention(lam, px, pc, tq, tkc):
    b, s, _ = px.shape
    lc = pc.shape[1]
    kern = functools.partial(_attn_kernel, tkc=tkc)
    return pl.pallas_call(
        kern,
        grid=(b, A_HEADS, s // tq),
        in_specs=[pl.BlockSpec(memory_space=pltpu.SMEM),
                  pl.BlockSpec((1, tq, LANES), lambda bi, h, i: (bi, i, P_Q // LANES + h)),
                  pl.BlockSpec((1, lc, 256), lambda bi, h, i: (bi, 0, P_KZ // 256 + h)),
                  pl.BlockSpec((1, lc, 256), lambda bi, h, i: (bi, 0, P_VX // 256 + h)),
                  pl.BlockSpec((1, s, 256), lambda bi, h, i: (bi, 0, P_KZ // 256 + h)),
                  pl.BlockSpec((1, s, 256), lambda bi, h, i: (bi, 0, P_VX // 256 + h))],
        out_specs=pl.BlockSpec((1, tq, LANES), lambda bi, h, i: (bi, i, h)),
        out_shape=jax.ShapeDtypeStruct((b, s, A_HEADS * LANES), BF16),
        scratch_shapes=[pltpu.VMEM((tq, 256), F32), pltpu.VMEM((tq, 256), F32)],
        compiler_params=_cparams(("arbitrary", "arbitrary", "arbitrary")),
        name="diff_attention",
    )(lam, px, pc, pc, px, px)


_TN = (((0,), (0,)), ((), ()))


def _ret_kernel(qf_ref, kf_ref, vf_ref, qb_ref, kb_ref, vb_ref, kc_ref, vc_ref,
                dmat_ref, qdec_ref, kdec_ref, gch_ref, of_ref, ob_ref, sf, sb):
    c = RET_CHUNK
    n = pl.program_id(1)
    col_head = lax.broadcasted_iota(jnp.int32, (c, R_HEADS * R_DK), 1) // R_DK
    bd_mask = (lax.broadcasted_iota(jnp.int32, (R_HEADS * R_DK, R_HEADS * R_DV), 0) // R_DK
               == lax.broadcasted_iota(jnp.int32, (R_HEADS * R_DK, R_HEADS * R_DV), 1) // R_DV)

    def state_update(s_ref, k, v, kdec, gch):
        kd = (k.astype(F32) * kdec).astype(BF16)
        kv = lax.dot_general(kd, v, _TN, preferred_element_type=F32)
        s_ref[...] = gch * s_ref[...] + jnp.where(bd_mask, kv, 0.0)

    def chunk(s_ref, q, k, v, dmat, qdec, kdec, gch):
        zero = jnp.zeros_like(k)
        kblk = jnp.concatenate([jnp.where(col_head == h, k, zero) for h in range(R_HEADS)], axis=0)
        sc = lax.dot_general(q, kblk, _NT, preferred_element_type=F32) * dmat
        p = sc.astype(BF16)
        intra = jnp.concatenate(
            [jnp.dot(p[:, h * c:(h + 1) * c], v[:, h * R_DV:(h + 1) * R_DV], preferred_element_type=F32)
             for h in range(R_HEADS)], axis=1)
        qd = (q.astype(F32) * qdec).astype(BF16)
        cross = jnp.dot(qd, s_ref[...].astype(BF16), preferred_element_type=F32)
        out = intra + cross
        state_update(s_ref, k, v, kdec, gch)
        return out

    @pl.when(n == 0)
    def _():
        sf[...] = jnp.zeros_like(sf)
        sb[...] = jnp.zeros_like(sb)
        n_ctx = kc_ref.shape[1] // c
        for cc in range(n_ctx):
            state_update(sf, kc_ref[0, cc * c:(cc + 1) * c, :], vc_ref[0, cc * c:(cc + 1) * c, :],
                         kdec_ref[0], gch_ref[0])
        for cc in reversed(range(n_ctx)):
            state_update(sb, kc_ref[0, cc * c:(cc + 1) * c, :], vc_ref[0, cc * c:(cc + 1) * c, :],
                         kdec_ref[1], gch_ref[1])

    of_ref[0] = chunk(sf, qf_ref[0], kf_ref[0], vf_ref[0], dmat_ref[0], qdec_ref[0], kdec_ref[0],
                      gch_ref[0]).astype(BF16)
    ob_ref[0] = chunk(sb, qb_ref[0], kb_ref[0], vb_ref[0], dmat_ref[1], qdec_ref[1], kdec_ref[1],
                      gch_ref[1]).astype(BF16)


def _retention(px, pc, dmat, qdec, kdec, gch):
    b, s, _ = px.shape
    lc = pc.shape[1]
    c = RET_CHUNK
    nch = s // c
    qk_w = R_HEADS * R_DK
    v_w = R_HEADS * R_DV
    fwd = lambda bi, n: (bi, n, 0)
    spec_q = lambda im: pl.BlockSpec((1, c, qk_w), lambda bi, n: im(bi, n)[:2] + (P_RQ // qk_w,))
    spec_k = lambda im: pl.BlockSpec((1, c, qk_w), lambda bi, n: im(bi, n)[:2] + (P_RK // qk_w,))
    spec_v = lambda im: pl.BlockSpec((1, c, v_w), lambda bi, n: im(bi, n)[:2] + (P_RV // v_w,))
    bwd = lambda bi, n: (bi, nch - 1 - n, 0)
    const3 = lambda shape: pl.BlockSpec(shape, lambda bi, n: (0, 0, 0))
    return pl.pallas_call(
        _ret_kernel,
        grid=(b, nch),
        in_specs=[spec_q(fwd), spec_k(fwd), spec_v(fwd), spec_q(bwd), spec_k(bwd), spec_v(bwd),
                  pl.BlockSpec((1, lc, qk_w), lambda bi, n: (bi, 0, P_RK // qk_w)),
                  pl.BlockSpec((1, lc, v_w), lambda bi, n: (bi, 0, P_RV // v_w)),
                  const3((2, c, R_HEADS * c)), const3((2, c, qk_w)), const3((2, c, qk_w)),
                  const3((2, 1, v_w))],
        out_specs=[pl.BlockSpec((1, c, v_w), fwd), pl.BlockSpec((1, c, v_w), bwd)],
        out_shape=[jax.ShapeDtypeStruct((b, s, v_w), BF16), jax.ShapeDtypeStruct((b, s, v_w), BF16)],
        scratch_shapes=[pltpu.VMEM((qk_w, v_w), F32), pltpu.VMEM((qk_w, v_w), F32)],
        compiler_params=_cparams(("arbitrary", "arbitrary")),
        name="retention",
    )(px, px, px, px, px, px, pc, pc, dmat, qdec, kdec, gch)


def _retention_tables(decay_fwd, decay_bwd):
    c = RET_CHUNK
    pos = jnp.arange(c, dtype=F32)
    rel = pos[:, None] - pos[None, :]

    def tables(logits, backward):
        lg = jax.nn.log_sigmoid(logits.astype(F32))
        r = -rel if backward else rel
        dm = jnp.where(r[None] >= 0, jnp.exp(jnp.maximum(r, 0.0)[None] * lg[:, None, None]), 0.0)
        dm = jnp.transpose(dm, (1, 0, 2)).reshape(c, R_HEADS * c)
        qpow = (c - pos) if backward else (pos + 1)
        kpow = pos if backward else (c - 1 - pos)
        qd = jnp.repeat(jnp.exp(qpow[:, None] * lg[None, :]), R_DK, axis=1)
        kd = jnp.repeat(jnp.exp(kpow[:, None] * lg[None, :]), R_DK, axis=1)
        gc = jnp.repeat(jnp.exp(c * lg), R_DV)[None, :]
        return dm, qd, kd, gc

    tf = tables(decay_fwd, False)
    tb = tables(decay_bwd, True)
    return tuple(jnp.stack([a, bb]) for a, bb in zip(tf, tb))


def _pack_pairs(x):
    k = x.shape[1] // 2
    lo = lax.bitcast_convert_type(x[:, :k].astype(BF16).astype(F32), U32)
    hi = lax.bitcast_convert_type(x[:, k:].astype(BF16).astype(F32), U32)
    return (lo >> 16) | (hi & jnp.uint32(0xFFFF0000))


def _unpack_pairs(w):
    lo = lax.bitcast_convert_type(w << 16, F32)
    hi = lax.bitcast_convert_type(w & jnp.uint32(0xFFFF0000), F32)
    return lo, hi


def _route_tile(logits_t, bias_col, carry_ref):
    n_e, tm = logits_t.shape
    per = n_e // N_GROUPS
    neg = -jnp.inf
    scores = jax.nn.sigmoid(logits_t)
    sel = scores + bias_col
    sub = lax.broadcasted_iota(jnp.int32, (per, tm), 0).astype(F32)
    sel_g = [sel[g * per:(g + 1) * per, :] for g in range(N_GROUPS)]
    flat_g = [sub + float(g * per) for g in range(N_GROUPS)]

    gs = []
    for g in range(N_GROUPS):
        top1 = jnp.max(sel_g[g], axis=0, keepdims=True)
        idx1 = jnp.min(jnp.where(sel_g[g] == top1, sub, float(per)), axis=0, keepdims=True)
        top2 = jnp.max(jnp.where(sub == idx1, neg, sel_g[g]), axis=0, keepdims=True)
        gs.append(top1 + top2)
    gsel = [jnp.zeros((1, tm), F32) for _ in range(N_GROUPS)]
    for _ in range(TOPK_GROUPS):
        m = functools.reduce(jnp.maximum, gs)
        found = jnp.zeros((1, tm), F32)
        for g in range(N_GROUPS):
            hit = jnp.logical_and(gs[g] == m, found == 0.0)
            found = jnp.where(hit, 1.0, found)
            gsel[g] = jnp.where(hit, 1.0, gsel[g])
            gs[g] = jnp.where(hit, neg, gs[g])

    masked = [jnp.where(gsel[g] > 0.0, sel_g[g], neg) for g in range(N_GROUPS)]
    chosen = [jnp.zeros((per, tm), F32) for _ in range(N_GROUPS)]
    for _ in range(TOP_K):
        m = functools.reduce(jnp.maximum, [jnp.max(mg, axis=0, keepdims=True) for mg in masked])
        idx = functools.reduce(jnp.minimum, [
            jnp.min(jnp.where(masked[g] == m, flat_g[g], float(n_e)), axis=0, keepdims=True)
            for g in range(N_GROUPS)])
        for g in range(N_GROUPS):
            hit = flat_g[g] == idx
            chosen[g] = jnp.where(hit, 1.0, chosen[g])
            masked[g] = jnp.where(hit, neg, masked[g])
    chosen = jnp.concatenate(chosen, axis=0)
    flat = jnp.concatenate(flat_g, axis=0)

    w_dense = scores * chosen
    w_dense = w_dense / jnp.sum(w_dense, axis=0, keepdims=True) * ROUTED_SCALE

    chosen_bf = chosen.astype(BF16)
    lower = (lax.broadcasted_iota(jnp.int32, (n_e, n_e), 1)
             < lax.broadcasted_iota(jnp.int32, (n_e, n_e), 0)).astype(BF16)
    slot_idx = jnp.dot(lower, chosen_bf, preferred_element_type=F32)
    upper = (lax.broadcasted_iota(jnp.int32, (tm, tm), 0)
             < lax.broadcasted_iota(jnp.int32, (tm, tm), 1)).astype(BF16)
    rank = jnp.dot(chosen_bf, upper, preferred_element_type=F32) + carry_ref[...]
    carry_ref[...] += jnp.sum(chosen, axis=1, keepdims=True)

    slots = []
    for j in range(TOP_K):
        ind = jnp.where(slot_idx == float(j), chosen, 0.0)
        slots.append((jnp.sum(ind * flat, axis=0, keepdims=True),
                      jnp.sum(ind * w_dense, axis=0, keepdims=True),
                      jnp.sum(ind * rank, axis=0, keepdims=True)))
    return slots


def _merge_kernel(att_ref, rf_ref, rb_ref, gate_ref, x_ref, g1_ref, sh2_ref, sc2_ref, ga_ref, gr_ref,
                  wout_ref, n2g_ref, rwt_ref, rbias_ref, x1_ref, h2p_ref, se_ref, sw_ref, sr_ref, cnt_ref,
                  carry, *, att_scale):
    @pl.when(jnp.logical_and(pl.program_id(0) == 0, pl.program_id(1) == 0))
    def _():
        carry[...] = jnp.zeros_like(carry)

    att = att_ref[0].astype(F32)
    ret = rf_ref[0].astype(F32) + rb_ref[0].astype(F32)
    gate = gate_ref[0].astype(F32)
    parts = []
    for h in range(A_HEADS):
        a = att[:, h * 128:(h + 1) * 128]
        parts.append(_rmsnorm_rows(a, ga_ref[...]) * att_scale)
    for h in range(R_HEADS):
        r = _rmsnorm_rows(ret[:, h * R_DV:(h + 1) * R_DV], gr_ref[...])
        g = gate[:, h * R_DV:(h + 1) * R_DV]
        parts.append(g * jax.nn.sigmoid(g) * r)
    merged = jnp.concatenate(parts, axis=1).astype(BF16)
    y = jnp.dot(merged, wout_ref[...], preferred_element_type=F32)
    x1 = x_ref[0] + g1_ref[0] * y
    x1_ref[0] = x1
    h2 = _rmsnorm_rows(x1, n2g_ref[...]) * (1.0 + sc2_ref[0]) + sh2_ref[0]
    h2p_ref[0] = _pack_pairs(h2)
    logits_t = lax.dot_general(rwt_ref[...], h2.astype(BF16), _NT, preferred_element_type=F32)
    slots = _route_tile(logits_t, rbias_ref[...], carry)
    for j, (e_j, w_j, r_j) in enumerate(slots):
        se_ref[j:j + 1, :] = e_j.astype(jnp.int32)
        sw_ref[j:j + 1, :] = w_j
        sr_ref[j:j + 1, :] = r_j.astype(jnp.int32)
    cnt_ref[...] = carry[...]


def _merge(att, rf, rb, px, x, g1, sh2, sc2, ga, gr, wout_bf, n2g, rwt_bf, rbias, att_scale, tm):
    b, s, d = x.shape
    nt = s // tm
    row = lambda bi, i: (bi, i, 0)
    tok = lambda bi, i: (0, bi * nt + i)
    per_b = pl.BlockSpec((1, 1, d), lambda bi, i: (bi, 0, 0))
    const2 = lambda shape: pl.BlockSpec(shape, lambda bi, i: (0, 0))
    return pl.pallas_call(
        functools.partial(_merge_kernel, att_scale=att_scale),
        grid=(b, nt),
        in_specs=[pl.BlockSpec((1, tm, 512), row), pl.BlockSpec((1, tm, 512), row),
                  pl.BlockSpec((1, tm, 512), row),
                  pl.BlockSpec((1, tm, 512), lambda bi, i: (bi, i, P_RG // 512)),
                  pl.BlockSpec((1, tm, d), row), per_b, per_b, per_b,
                  const2((1, 128)), const2((1, R_DV)), const2((d, d)), const2((1, d)),
                  const2((N_EXPERTS, d)), const2((N_EXPERTS, 1))],
        out_specs=[pl.BlockSpec((1, tm, d), row), pl.BlockSpec((1, tm, d // 2), row),
                   pl.BlockSpec((TOP_K, tm), tok), pl.BlockSpec((TOP_K, tm), tok),
                   pl.BlockSpec((TOP_K, tm), tok), const2((N_EXPERTS, 1))],
        out_shape=[jax.ShapeDtypeStruct((b, s, d), F32), jax.ShapeDtypeStruct((b, s, d // 2), U32),
                   jax.ShapeDtypeStruct((TOP_K, b * s), jnp.int32),
                   jax.ShapeDtypeStruct((TOP_K, b * s), F32),
                   jax.ShapeDtypeStruct((TOP_K, b * s), jnp.int32),
                   jax.ShapeDtypeStruct((N_EXPERTS, 1), F32)],
        scratch_shapes=[pltpu.VMEM((N_EXPERTS, 1), F32)],
        compiler_params=_cparams(("arbitrary", "arbitrary")),
        name="mixer_merge",
    )(att, rf, rb, px, x, g1, sh2, sc2, ga, gr, wout_bf, n2g, rwt_bf, rbias)


def _dispatch_kernel(dest_ref, h_ref, xs_ref, buf, sem):
    tm = h_ref.shape[0]
    i = pl.program_id(0)
    n = pl.num_programs(0)
    slot = i % 2

    def wait_slot(s):
        for _ in range(TOP_K):
            pltpu.make_async_copy(buf.at[s], xs_ref.at[pl.ds(0, tm)], sem.at[s]).wait()

    @pl.when(i >= 2)
    def _():
        wait_slot(slot)

    buf[slot] = h_ref[...]

    def body(t, carry):
        for j in range(TOP_K):
            d = dest_ref[j, t]
            pltpu.make_async_copy(buf.at[slot, pl.ds(t, 1)], xs_ref.at[pl.ds(d, 1)], sem.at[slot]).start()
        return carry

    lax.fori_loop(0, tm, body, 0)

    @pl.when(i == n - 1)
    def _():
        wait_slot(slot)

    @pl.when(jnp.logical_and(i == n - 1, n >= 2))
    def _():
        wait_slot(1 - slot)


def _dispatch(dest_t, h2p, tm):
    t, kw = h2p.shape
    n = t // tm
    return pl.pallas_call(
        _dispatch_kernel,
        grid=(n,),
        in_specs=[pl.BlockSpec((TOP_K, tm), lambda i: (0, i), memory_space=pltpu.SMEM),
                  pl.BlockSpec((tm, kw), lambda i: (i, 0))],
        out_specs=pl.BlockSpec(memory_space=pl.ANY),
        out_shape=jax.ShapeDtypeStruct((t * TOP_K, kw), U32),
        scratch_shapes=[pltpu.VMEM((2, tm, kw), U32), pltpu.SemaphoreType.DMA((2,))],
        compiler_params=_cparams(("arbitrary",)),
        name="moe_dispatch",
    )(dest_t, h2p)


def _expert_kernel(iblk_ref, iexp_ref, ilo_ref, ihi_ref, xs_ref, w1_ref, w3_ref, w2_ref, ys_ref,
                   acc, w1b, w3b, w2b):
    r = xs_ref.shape[0]
    i = pl.program_id(0)
    blk = iblk_ref[i]
    lo = ilo_ref[i]
    hi = ihi_ref[i]
    prev = jnp.maximum(i - 1, 0)
    first = jnp.logical_or(i == 0, iblk_ref[prev] != blk)

    @pl.when(jnp.logical_or(i == 0, iexp_ref[prev] != iexp_ref[i]))
    def _():
        w1b[...] = w1_ref[0].astype(BF16)
        w3b[...] = w3_ref[0].astype(BF16)
        w2b[...] = w2_ref[0].astype(BF16)

    rows = blk * r + lax.broadcasted_iota(jnp.int32, (r, 1), 0)
    valid = jnp.logical_and(rows >= lo, rows < hi)
    xlo, xhi = _unpack_pairs(xs_ref[...])
    x = jnp.where(valid, jnp.concatenate([xlo, xhi], axis=1), 0.0).astype(BF16)
    a = jnp.dot(x, w1b[...], preferred_element_type=F32)
    bb = jnp.dot(x, w3b[...], preferred_element_type=F32)
    hmid = (a * jax.nn.sigmoid(a) * bb).astype(BF16)
    y = jnp.dot(hmid, w2b[...], preferred_element_type=F32)

    @pl.when(first)
    def _():
        acc[...] = y

    @pl.when(jnp.logical_not(first))
    def _():
        acc[...] += y

    ys_ref[...] = _pack_pairs(acc[...])


def _experts(iblk, iexp, ilo, ihi, xs, w1, w3, w2, r):
    na, kw = xs.shape
    n_items = iblk.shape[0]
    d = w1.shape[1]
    de = w1.shape[2]
    grid_spec = pltpu.PrefetchScalarGridSpec(
        num_scalar_prefetch=4,
        grid=(n_items,),
        in_specs=[pl.BlockSpec((r, kw), lambda i, ib, ie, il, ih: (ib[i], 0)),
                  pl.BlockSpec((1, d, de), lambda i, ib, ie, il, ih: (ie[i], 0, 0)),
                  pl.BlockSpec((1, d, de), lambda i, ib, ie, il, ih: (ie[i], 0, 0)),
                  pl.BlockSpec((1, de, d), lambda i, ib, ie, il, ih: (ie[i], 0, 0))],
        out_specs=pl.BlockSpec((r, kw), lambda i, ib, ie, il, ih: (ib[i], 0)),
        scratch_shapes=[pltpu.VMEM((r, d), F32), pltpu.VMEM((d, de), BF16), pltpu.VMEM((d, de), BF16),
                        pltpu.VMEM((de, d), BF16)],
    )
    return pl.pallas_call(
        _expert_kernel,
        grid_spec=grid_spec,
        out_shape=jax.ShapeDtypeStruct((na, kw), U32),
        compiler_params=_cparams(("arbitrary",)),
        name="moe_experts",
    )(iblk, iexp, ilo, ihi, xs, w1, w3, w2)


def _combine_kernel(dest_ref, dnext_ref, x1_ref, h2p_ref, wts_ref, g2_ref, sw1_ref, sw3_ref, sw2_ref,
                    fg_ref, ys_ref, o_ref, gbuf, sem):
    tm = x1_ref.shape[0]
    i = pl.program_id(0)
    n = pl.num_programs(0)
    slot = i % 2

    def issue(dref, s):
        def body(t, carry):
            for j in range(TOP_K):
                d = dref[j, t]
                pltpu.make_async_copy(ys_ref.at[pl.ds(d, 1)], gbuf.at[s, pl.ds(j * tm + t, 1)],
                                      sem.at[s]).start()
            return carry
        lax.fori_loop(0, tm, body, 0)

    @pl.when(i == 0)
    def _():
        issue(dest_ref, 0)

    @pl.when(i + 1 < n)
    def _():
        issue(dnext_ref, 1 - slot)

    for _ in range(TOP_K):
        pltpu.make_async_copy(ys_ref.at[pl.ds(0, tm)], gbuf.at[slot, pl.ds(0, tm)], sem.at[slot]).wait()

    wts = wts_ref[...]
    moe_lo = None
    moe_hi = None
    for j in range(TOP_K):
        ylo, yhi = _unpack_pairs(gbuf[slot, j * tm:(j + 1) * tm, :])
        wj = wts[:, j:j + 1]
        moe_lo = ylo * wj if moe_lo is None else moe_lo + ylo * wj
        moe_hi = yhi * wj if moe_hi is None else moe_hi + yhi * wj
    hlo, hhi = _unpack_pairs(h2p_ref[...])
    h = jnp.concatenate([hlo, hhi], axis=1).astype(BF16)
    a = jnp.dot(h, sw1_ref[...], preferred_element_type=F32)
    bb = jnp.dot(h, sw3_ref[...], preferred_element_type=F32)
    shared = jnp.dot((a * jax.nn.sigmoid(a) * bb).astype(BF16), sw2_ref[...], preferred_element_type=F32)
    ffn = jnp.concatenate([moe_lo, moe_hi], axis=1) + shared
    xo = x1_ref[...] + g2_ref[0] * ffn
    o_ref[...] = _rmsnorm_rows(xo, fg_ref[...])


def _combine(dest_t, x1, h2p, wts, g2, sw1, sw3, sw2, fg, ys, tiles_per_batch, tm):
    t, d = x1.shape
    kw = d // 2
    n = t // tm
    de = sw1.shape[1]
    const2 = lambda shape: pl.BlockSpec(shape, lambda i: (0, 0))
    return pl.pallas_call(
        _combine_kernel,
        grid=(n,),
        in_specs=[pl.BlockSpec((TOP_K, tm), lambda i: (0, i), memory_space=pltpu.SMEM),
                  pl.BlockSpec((TOP_K, tm), lambda i: (0, jnp.minimum(i + 1, n - 1)),
                               memory_space=pltpu.SMEM),
                  pl.BlockSpec((tm, d), lambda i: (i, 0)),
                  pl.BlockSpec((tm, kw), lambda i: (i, 0)),
                  pl.BlockSpec((tm, TOP_K), lambda i: (i, 0)),
                  pl.BlockSpec((1, 1, d), lambda i: (i // tiles_per_batch, 0, 0)),
                  const2((d, de)), const2((d, de)), const2((de, d)), const2((1, d)),
                  pl.BlockSpec(memory_space=pl.ANY)],
        out_specs=pl.BlockSpec((tm, d), lambda i: (i, 0)),
        out_shape=jax.ShapeDtypeStruct((t, d), F32),
        scratch_shapes=[pltpu.VMEM((2, tm * TOP_K, kw), U32), pltpu.SemaphoreType.DMA((2,))],
        compiler_params=_cparams(("arbitrary",)),
        name="moe_combine",
    )(dest_t, dest_t, x1, h2p, wts, g2, sw1, sw3, sw2, fg, ys)


def _route_meta(counts, slot_e, slot_r, block_rows):
    t = slot_e.shape[1]
    ends = jnp.cumsum(counts)
    starts = ends - counts
    onehot = slot_e[:, :, None] == jnp.arange(N_EXPERTS, dtype=jnp.int32)
    dest_t = slot_r + jnp.sum(jnp.where(onehot, starts, 0), axis=-1).astype(jnp.int32)

    n_blk = t * TOP_K // block_rows
    n_items = n_blk + N_EXPERTS - 1
    first_blk = starts // block_rows
    last_blk = (ends - 1) // block_rows
    n_e = jnp.where(counts > 0, last_blk - first_blk + 1, 0)
    item_end = jnp.cumsum(n_e)
    item_off = item_end - n_e
    total = item_end[-1]
    idx = jnp.arange(n_items, dtype=jnp.int32)
    e_of = jnp.minimum(jnp.searchsorted(item_end, idx, side="right"), N_EXPERTS - 1).astype(jnp.int32)
    live = idx < total
    blk = first_blk[e_of] + (idx - item_off[e_of])
    lo = jnp.maximum(starts[e_of], blk * block_rows)
    hi = jnp.minimum(ends[e_of], (blk + 1) * block_rows)
    last_e = e_of[jnp.maximum(total - 1, 0)]
    iblk = jnp.where(live, blk, n_blk - 1).astype(jnp.int32)
    iexp = jnp.where(live, e_of, last_e).astype(jnp.int32)
    ilo = jnp.where(live, lo, 0).astype(jnp.int32)
    ihi = jnp.where(live, hi, 0).astype(jnp.int32)
    return dest_t, iblk, iexp, ilo, ihi


def _rope_tables(n_tokens):
    n_rows = n_tokens // GRID_W
    rows = jnp.repeat(jnp.arange(n_rows, dtype=F32), GRID_W)
    cols = jnp.tile(jnp.arange(GRID_W, dtype=F32), n_rows)
    n_freq = HEAD_DIM // 4
    inv_freq = ROPE_BASE ** (-jnp.arange(n_freq, dtype=F32) / n_freq)
    ar = rows[:, None] * inv_freq
    ac = cols[:, None] * inv_freq
    cos64 = jnp.concatenate([jnp.cos(ar), jnp.cos(ar), jnp.cos(ac), jnp.cos(ac)], axis=1)
    sin64 = jnp.concatenate([-jnp.sin(ar), jnp.sin(ar), -jnp.sin(ac), jnp.sin(ac)], axis=1)
    return jnp.tile(cos64, (1, 2)), jnp.tile(sin64, (1, 2))


def kernel(x, c, ctx, c_ctx, ada_w, ada_b, norm1_g, w_in, lambda_q1, lambda_k1, lambda_q2, lambda_k2,
           dattn_norm_g, ret_decay_fwd, ret_decay_bwd, ret_norm_g, w_out, norm2_g, router_w, router_bias,
           exp_w1, exp_w3, exp_w2, shared_w1, shared_w3, shared_w2, final_norm_g):
    b, s, d = x.shape
    lc = ctx.shape[1]
    depth = ada_w.shape[0]
    assert depth == 1, "single-layer problem: the context stream is never updated"
    l = 0
    lam_init = 0.8 - 0.6 * math.exp(-0.3 * l)

    c8 = jnp.zeros((8, d), F32).at[:b].set(c).at[b].set(c_ctx)
    mod = _modulation(c8, ada_w[l], ada_b[l])
    sh1, sc1, g1, sh2, sc2, g2 = [mod[:b, k * d:(k + 1) * d].reshape(b, 1, d) for k in range(6)]
    csh1 = jnp.broadcast_to(mod[b, 0:d].reshape(1, 1, d), (b, 1, d))
    csc1 = jnp.broadcast_to(mod[b, d:2 * d].reshape(1, 1, d), (b, 1, d))

    w_in_bf = w_in[l].astype(BF16)
    n1g = norm1_g[l].reshape(1, d)
    cos, sin = _rope_tables(s)
    px = _inproj(x, n1g, sc1, sh1, cos, sin, w_in_bf, tm=512)
    pc = _inproj(ctx, n1g, csc1, csh1, jnp.ones((lc, LANES), F32), jnp.zeros((lc, LANES), F32),
                 w_in_bf, tm=lc)

    lam = (jnp.exp(jnp.sum(lambda_q1[l] * lambda_k1[l])) - jnp.exp(jnp.sum(lambda_q2[l] * lambda_k2[l]))
           ).astype(F32) + lam_init
    att = _attention(lam.reshape(1), px, pc, tq=512, tkc=512)

    dmat, qdec, kdec, gch = _retention_tables(ret_decay_fwd[l], ret_decay_bwd[l])
    ret_f, ret_b = _retention(px, pc, dmat, qdec, kdec, gch)

    x1, h2p, slot_e, slot_w, slot_r, counts = _merge(
        att, ret_f, ret_b, px, x, g1, sh2, sc2,
        dattn_norm_g[l].reshape(1, 2 * HEAD_DIM), ret_norm_g[l].reshape(1, R_DV),
        w_out[l].astype(BF16), norm2_g[l].reshape(1, d), router_w[l].T.astype(BF16),
        router_bias[l].astype(F32).reshape(N_EXPERTS, 1), 1.0 - lam_init, tm=512)

    t = b * s
    block_rows = 256
    dest_t, iblk, iexp, ilo, ihi = _route_meta(counts[:, 0].astype(jnp.int32), slot_e, slot_r, block_rows)
    h2p = h2p.reshape(t, d // 2)
    xs = _dispatch(dest_t, h2p, tm=128)
    ys = _experts(iblk, iexp, ilo, ihi, xs, exp_w1[l], exp_w3[l], exp_w2[l], block_rows)
    tm_c = 128
    out = _combine(dest_t, x1.reshape(t, d), h2p, slot_w.T, g2, shared_w1[l].astype(BF16),
                   shared_w3[l].astype(BF16), shared_w2[l].astype(BF16), final_norm_g.reshape(1, d),
                   ys, s // tm_c, tm_c)
    return out.reshape(b, s, d)
```

```python
import functools
import math

import jax
import jax.numpy as jnp
from jax import lax
from jax.experimental import pallas as pl
from jax.experimental.pallas import tpu as pltpu

F32 = jnp.float32
BF16 = jnp.bfloat16
U32 = jnp.uint32

D_MODEL = 1024
GRID_W = 64
HEAD_DIM = 64
A_HEADS = 4
R_HEADS = 4
R_DK = 64
R_DV = 128
RET_CHUNK = 128
ROPE_BASE = 10000.0
N_EXPERTS = 64
N_GROUPS = 8
TOPK_GROUPS = 4
TOP_K = 8
D_EXPERT = 256
ROUTED_SCALE = 2.5
NORM_EPS = 1e-6
LANES = 128

P_KZ = 0
P_RQ = 1024
P_RK = 1280
P_RV = 1536
P_RG = 2048
P_COLS = 2560
VT_ROWS = 144
LOG2E = 1.4426950408889634

VMEM_LIMIT = 56 * 1024 * 1024


def _cparams(sem):
    return pltpu.CompilerParams(dimension_semantics=sem, vmem_limit_bytes=VMEM_LIMIT)


def _mod_kernel(c_ref, w_ref, b_ref, o_ref):
    c = c_ref[...]
    s = c * jax.nn.sigmoid(c)
    o_ref[...] = jnp.dot(s, w_ref[...], preferred_element_type=F32,
                         precision=lax.Precision.HIGHEST) + b_ref[...]


def _modulation(c8, ada_w, ada_b):
    d = c8.shape[1]
    n = ada_w.shape[1]
    tn = 512
    return pl.pallas_call(
        _mod_kernel,
        grid=(n // tn,),
        in_specs=[pl.BlockSpec((8, d), lambda j: (0, 0)),
                  pl.BlockSpec((d, tn), lambda j: (0, j)),
                  pl.BlockSpec((1, tn), lambda j: (0, j))],
        out_specs=pl.BlockSpec((8, tn), lambda j: (0, j)),
        out_shape=jax.ShapeDtypeStruct((8, n), F32),
        compiler_params=_cparams(("arbitrary",)),
        name="modulation",
    )(c8, ada_w, ada_b.reshape(1, n))


def _rmsnorm_rows(x, g):
    return x * lax.rsqrt(jnp.mean(x * x, axis=-1, keepdims=True) + NORM_EPS) * g


def _inproj_kernel(x_ref, g_ref, sc_ref, sh_ref, cos_ref, sin_ref, cost_ref, sint_ref, w_ref, wt_ref,
                   p_ref, qt_ref, vt_ref):
    x = x_ref[0]
    tm = x.shape[0]
    h = _rmsnorm_rows(x, g_ref[...]) * (1.0 + sc_ref[0]) + sh_ref[0]
    hb = h.astype(BF16)
    p = jnp.dot(hb, w_ref[...], preferred_element_type=F32)
    pt = lax.dot_general(wt_ref[...], hb, _NT, preferred_element_type=F32)

    cos = cos_ref[...]
    sin = sin_ref[...]
    lane = lax.broadcasted_iota(jnp.int32, (tm, LANES), 1)
    first_half = (lane % 32) < 16
    low64 = lane < 64

    def rope(xs):
        partner = jnp.where(first_half, pltpu.roll(xs, LANES - 16, 1), pltpu.roll(xs, 16, 1))
        return xs * cos + partner * sin

    zero = jnp.zeros((tm, LANES), BF16)
    for hh in range(A_HEADS):
        k = rope(p[:, hh * 128:(hh + 1) * 128]).astype(BF16)
        p_ref[0, :, P_KZ + hh * 256:P_KZ + hh * 256 + 128] = jnp.where(low64, k, zero)
        p_ref[0, :, P_KZ + hh * 256 + 128:P_KZ + (hh + 1) * 256] = jnp.where(low64, zero, k)
    for cc in range(2):
        rq = rope(p[:, 512 + cc * 128:512 + (cc + 1) * 128])
        p_ref[0, :, P_RQ + cc * 128:P_RQ + (cc + 1) * 128] = rq.astype(BF16)
        rk = rope(p[:, 768 + cc * 128:768 + (cc + 1) * 128]) * (R_DK ** -0.5)
        p_ref[0, :, P_RK + cc * 128:P_RK + (cc + 1) * 128] = rk.astype(BF16)
    p_ref[0, :, P_RV:P_RV + 512] = p[:, 1024:1536].astype(BF16)
    p_ref[0, :, P_RG:P_RG + 512] = p[:, 1536:2048].astype(BF16)

    cost = cost_ref[...]
    sint = sint_ref[...]
    q_scale = (HEAD_DIM ** -0.5) * LOG2E
    ones_rows = jnp.where(lax.broadcasted_iota(jnp.int32, (VT_ROWS - 128, tm), 0) == 0, 1.0, 0.0)
    for hh in range(A_HEADS):
        xq = pt[hh * 128:(hh + 1) * 128, :]
        groups = [xq[16 * gi:16 * (gi + 1), :] for gi in range(8)]
        partner = jnp.concatenate([groups[gi ^ 1] for gi in range(8)], axis=0)
        qt_ref[0, hh] = ((xq * cost + partner * sint) * q_scale).astype(BF16)
        vt_ref[0, hh, 0:128, :] = pt[512 + hh * 128:512 + (hh + 1) * 128, :].astype(BF16)
        vt_ref[0, hh, 128:VT_ROWS, :] = ones_rows.astype(BF16)


def _inproj(x, g, sc, sh, cos, sin, cost, sint, w_bf, wt_bf, tm):
    b, s, d = x.shape
    n_tok = w_bf.shape[1]
    n_t = wt_bf.shape[0]
    const2 = lambda shape: pl.BlockSpec(shape, lambda bi, i: (0, 0))
    per_b = pl.BlockSpec((1, 1, d), lambda bi, i: (bi, 0, 0))
    return pl.pallas_call(
        _inproj_kernel,
        grid=(b, s // tm),
        in_specs=[pl.BlockSpec((1, tm, d), lambda bi, i: (bi, i, 0)),
                  const2((1, d)), per_b, per_b,
                  pl.BlockSpec((tm, LANES), lambda bi, i: (i, 0)),
                  pl.BlockSpec((tm, LANES), lambda bi, i: (i, 0)),
                  pl.BlockSpec((LANES, tm), lambda bi, i: (0, i)),
                  pl.BlockSpec((LANES, tm), lambda bi, i: (0, i)),
                  const2((d, n_tok)), const2((n_t, d))],
        out_specs=[pl.BlockSpec((1, tm, P_COLS), lambda bi, i: (bi, i, 0)),
                   pl.BlockSpec((1, A_HEADS, LANES, tm), lambda bi, i: (bi, 0, 0, i)),
                   pl.BlockSpec((1, A_HEADS, VT_ROWS, tm), lambda bi, i: (bi, 0, 0, i))],
        out_shape=[jax.ShapeDtypeStruct((b, s, P_COLS), BF16),
                   jax.ShapeDtypeStruct((b, A_HEADS, LANES, s), BF16),
                   jax.ShapeDtypeStruct((b, A_HEADS, VT_ROWS, s), BF16)],
        compiler_params=_cparams(("arbitrary", "arbitrary")),
        name="inproj",
    )(x, g, sc, sh, cos, sin, cost, sint, w_bf, wt_bf)


_NT = (((1,), (1,)), ((), ()))


def _attn_kernel(lam_ref, qt_ref, kzc_ref, vtc_ref, kzx_ref, vtx_ref, o_ref,
                 acc, s_a, s_b, e_a, e_b, *, tkc):
    qt = qt_ref[0, 0]
    tq = qt.shape[1]
    n_chunks = kzx_ref.shape[1] // tkc
    assert n_chunks % 2 == 0 and n_chunks >= 4

    def scores(kz, s_ref):
        for mp in range(2):
            s_ref[mp] = jnp.dot(kz[:, mp * LANES:(mp + 1) * LANES], qt, preferred_element_type=F32)

    def softmax(s_ref, e_ref, ms):
        new_ms, alphas = [], []
        for mp in range(2):
            st = s_ref[mp]
            m_new = jnp.maximum(ms[mp], jnp.max(st, axis=0, keepdims=True))
            alphas.append(jnp.exp2(ms[mp] - m_new))
            e_ref[mp] = jnp.exp2(st - m_new).astype(BF16)
            new_ms.append(m_new)
        return tuple(new_ms), tuple(alphas)

    def values(vt, e_ref, alphas):
        for mp in range(2):
            acc[mp] = alphas[mp] * acc[mp] + jnp.dot(vt, e_ref[mp], preferred_element_type=F32)

    def kchunk(c):
        return kzx_ref[0, pl.ds(pl.multiple_of(c * tkc, tkc), tkc), :]

    def vchunk(c):
        return vtx_ref[0, 0, :, pl.ds(pl.multiple_of(c * tkc, tkc), tkc)]

    ms = []
    for mp in range(2):
        st = jnp.dot(kzc_ref[0, :, mp * LANES:(mp + 1) * LANES], qt, preferred_element_type=F32)
        m_new = jnp.max(st, axis=0, keepdims=True)
        acc[mp] = jnp.dot(vtc_ref[0, 0], jnp.exp2(st - m_new).astype(BF16), preferred_element_type=F32)
        ms.append(m_new)
    ms = tuple(ms)

    scores(kchunk(0), s_a)
    scores(kchunk(1), s_b)
    ms, al_a = softmax(s_a, e_a, ms)

    def body(j, carry):
        ms, al_a = carry
        scores(kchunk(2 * j + 2), s_a)
        ms, al_b = softmax(s_b, e_b, ms)
        values(vchunk(2 * j), e_a, al_a)
        scores(kchunk(2 * j + 3), s_b)
        ms, al_a = softmax(s_a, e_a, ms)
        values(vchunk(2 * j + 1), e_b, al_b)
        return ms, al_a

    ms, al_a = lax.fori_loop(0, n_chunks // 2 - 1, body, (ms, al_a))
    ms, al_b = softmax(s_b, e_b, ms)
    values(vchunk(n_chunks - 2), e_a, al_a)
    values(vchunk(n_chunks - 1), e_b, al_b)

    a1 = acc[0]
    a2 = acc[1]
    o1 = a1[:LANES, :] / a1[LANES:LANES + 1, :]
    o2 = a2[:LANES, :] / a2[LANES:LANES + 1, :]
    o_ref[0] = (o1 - lam_ref[0] * o2).T.astype(BF16)


def _attention(lam, qt, px, vtx, pc, vtc, tq, tkc):
    b, s, _ = px.shape
    lc = pc.shape[1]
    kern = functools.partial(_attn_kernel, tkc=tkc)
    return pl.pallas_call(
        kern,
        grid=(b, A_HEADS, s // tq),
        in_specs=[pl.BlockSpec(memory_space=pltpu.SMEM),
                  pl.BlockSpec((1, 1, LANES, tq), lambda bi, h, i: (bi, h, 0, i)),
                  pl.BlockSpec((1, lc, 256), lambda bi, h, i: (bi, 0, P_KZ // 256 + h)),
                  pl.BlockSpec((1, 1, VT_ROWS, lc), lambda bi, h, i: (bi, h, 0, 0)),
                  pl.BlockSpec((1, s, 256), lambda bi, h, i: (bi, 0, P_KZ // 256 + h)),
                  pl.BlockSpec((1, 1, VT_ROWS, s), lambda bi, h, i: (bi, h, 0, 0))],
        out_specs=pl.BlockSpec((1, tq, LANES), lambda bi, h, i: (bi, i, h)),
        out_shape=jax.ShapeDtypeStruct((b, s, A_HEADS * LANES), BF16),
        scratch_shapes=[pltpu.VMEM((2, VT_ROWS, tq), F32),
                        pltpu.VMEM((2, tkc, tq), F32), pltpu.VMEM((2, tkc, tq), F32),
                        pltpu.VMEM((2, tkc, tq), BF16), pltpu.VMEM((2, tkc, tq), BF16)],
        compiler_params=_cparams(("arbitrary", "arbitrary", "arbitrary")),
        name="diff_attention",
    )(lam, qt, pc, vtc, px, vtx)


_TN = (((0,), (0,)), ((), ()))


def _ret_kernel(qf_ref, kf_ref, vf_ref, qb_ref, kb_ref, vb_ref, kc_ref, vc_ref,
                dmat_ref, qdec_ref, kdec_ref, gch_ref, of_ref, ob_ref, sf, sb):
    c = RET_CHUNK
    n = pl.program_id(1)
    col_head = lax.broadcasted_iota(jnp.int32, (c, R_HEADS * R_DK), 1) // R_DK
    bd_mask = (lax.broadcasted_iota(jnp.int32, (R_HEADS * R_DK, R_HEADS * R_DV), 0) // R_DK
               == lax.broadcasted_iota(jnp.int32, (R_HEADS * R_DK, R_HEADS * R_DV), 1) // R_DV)

    def state_update(s_ref, k, v, kdec, gch):
        kd = (k.astype(F32) * kdec).astype(BF16)
        kv = lax.dot_general(kd, v, _TN, preferred_element_type=F32)
        s_ref[...] = gch * s_ref[...] + jnp.where(bd_mask, kv, 0.0)

    def chunk(s_ref, q, k, v, dmat, qdec, kdec, gch):
        zero = jnp.zeros_like(k)
        kblk = jnp.concatenate([jnp.where(col_head == h, k, zero) for h in range(R_HEADS)], axis=0)
        sc = lax.dot_general(q, kblk, _NT, preferred_element_type=F32) * dmat
        p = sc.astype(BF16)
        intra = jnp.concatenate(
            [jnp.dot(p[:, h * c:(h + 1) * c], v[:, h * R_DV:(h + 1) * R_DV], preferred_element_type=F32)
             for h in range(R_HEADS)], axis=1)
        qd = (q.astype(F32) * qdec).astype(BF16)
        cross = jnp.dot(qd, s_ref[...].astype(BF16), preferred_element_type=F32)
        out = intra + cross
        state_update(s_ref, k, v, kdec, gch)
        return out

    @pl.when(n == 0)
    def _():
        sf[...] = jnp.zeros_like(sf)
        sb[...] = jnp.zeros_like(sb)
        n_ctx = kc_ref.shape[1] // c
        for cc in range(n_ctx):
            state_update(sf, kc_ref[0, cc * c:(cc + 1) * c, :], vc_ref[0, cc * c:(cc + 1) * c, :],
                         kdec_ref[0], gch_ref[0])
        for cc in reversed(range(n_ctx)):
            state_update(sb, kc_ref[0, cc * c:(cc + 1) * c, :], vc_ref[0, cc * c:(cc + 1) * c, :],
                         kdec_ref[1], gch_ref[1])

    of_ref[0] = chunk(sf, qf_ref[0], kf_ref[0], vf_ref[0], dmat_ref[0], qdec_ref[0], kdec_ref[0],
                      gch_ref[0]).astype(BF16)
    ob_ref[0] = chunk(sb, qb_ref[0], kb_ref[0], vb_ref[0], dmat_ref[1], qdec_ref[1], kdec_ref[1],
                      gch_ref[1]).astype(BF16)


def _retention(px, pc, dmat, qdec, kdec, gch):
    b, s, _ = px.shape
    lc = pc.shape[1]
    c = RET_CHUNK
    nch = s // c
    qk_w = R_HEADS * R_DK
    v_w = R_HEADS * R_DV
    fwd = lambda bi, n: (bi, n, 0)
    spec_q = lambda im: pl.BlockSpec((1, c, qk_w), lambda bi, n: im(bi, n)[:2] + (P_RQ // qk_w,))
    spec_k = lambda im: pl.BlockSpec((1, c, qk_w), lambda bi, n: im(bi, n)[:2] + (P_RK // qk_w,))
    spec_v = lambda im: pl.BlockSpec((1, c, v_w), lambda bi, n: im(bi, n)[:2] + (P_RV // v_w,))
    bwd = lambda bi, n: (bi, nch - 1 - n, 0)
    const3 = lambda shape: pl.BlockSpec(shape, lambda bi, n: (0, 0, 0))
    return pl.pallas_call(
        _ret_kernel,
        grid=(b, nch),
        in_specs=[spec_q(fwd), spec_k(fwd), spec_v(fwd), spec_q(bwd), spec_k(bwd), spec_v(bwd),
                  pl.BlockSpec((1, lc, qk_w), lambda bi, n: (bi, 0, P_RK // qk_w)),
                  pl.BlockSpec((1, lc, v_w), lambda bi, n: (bi, 0, P_RV // v_w)),
                  const3((2, c, R_HEADS * c)), const3((2, c, qk_w)), const3((2, c, qk_w)),
                  const3((2, 1, v_w))],
        out_specs=[pl.BlockSpec((1, c, v_w), fwd), pl.BlockSpec((1, c, v_w), bwd)],
        out_shape=[jax.ShapeDtypeStruct((b, s, v_w), BF16), jax.ShapeDtypeStruct((b, s, v_w), BF16)],
        scratch_shapes=[pltpu.VMEM((qk_w, v_w), F32), pltpu.VMEM((qk_w, v_w), F32)],
        compiler_params=_cparams(("arbitrary", "arbitrary")),
        name="retention",
    )(px, px, px, px, px, px, pc, pc, dmat, qdec, kdec, gch)


def _retention_tables(decay_fwd, decay_bwd):
    c = RET_CHUNK
    pos = jnp.arange(c, dtype=F32)
    rel = pos[:, None] - pos[None, :]

    def tables(logits, backward):
        lg = jax.nn.log_sigmoid(logits.astype(F32))
        r = -rel if backward else rel
        dm = jnp.where(r[None] >= 0, jnp.exp(jnp.maximum(r, 0.0)[None] * lg[:, None, None]), 0.0)
        dm = jnp.transpose(dm, (1, 0, 2)).reshape(c, R_HEADS * c)
        qpow = (c - pos) if backward else (pos + 1)
        kpow = pos if backward else (c - 1 - pos)
        qd = jnp.repeat(jnp.exp(qpow[:, None] * lg[None, :]), R_DK, axis=1)
        kd = jnp.repeat(jnp.exp(kpow[:, None] * lg[None, :]), R_DK, axis=1)
        gc = jnp.repeat(jnp.exp(c * lg), R_DV)[None, :]
        return dm, qd, kd, gc

    tf = tables(decay_fwd, False)
    tb = tables(decay_bwd, True)
    return tuple(jnp.stack([a, bb]) for a, bb in zip(tf, tb))


def _pack_pairs(x):
    k = x.shape[1] // 2
    lo = lax.bitcast_convert_type(x[:, :k].astype(BF16).astype(F32), U32)
    hi = lax.bitcast_convert_type(x[:, k:].astype(BF16).astype(F32), U32)
    return (lo >> 16) | (hi & jnp.uint32(0xFFFF0000))


def _unpack_pairs(w):
    lo = lax.bitcast_convert_type(w << 16, F32)
    hi = lax.bitcast_convert_type(w & jnp.uint32(0xFFFF0000), F32)
    return lo, hi


def _route_tile(logits_t, bias_col, carry_ref):
    n_e, tm = logits_t.shape
    per = n_e // N_GROUPS
    neg = -jnp.inf
    scores = jax.nn.sigmoid(logits_t)
    sel = scores + bias_col
    sub = lax.broadcasted_iota(jnp.int32, (per, tm), 0).astype(F32)
    sel_g = [sel[g * per:(g + 1) * per, :] for g in range(N_GROUPS)]
    flat_g = [sub + float(g * per) for g in range(N_GROUPS)]

    gs = []
    for g in range(N_GROUPS):
        top1 = jnp.max(sel_g[g], axis=0, keepdims=True)
        idx1 = jnp.min(jnp.where(sel_g[g] == top1, sub, float(per)), axis=0, keepdims=True)
        top2 = jnp.max(jnp.where(sub == idx1, neg, sel_g[g]), axis=0, keepdims=True)
        gs.append(top1 + top2)
    gsel = [jnp.zeros((1, tm), F32) for _ in range(N_GROUPS)]
    for _ in range(TOPK_GROUPS):
        m = functools.reduce(jnp.maximum, gs)
        found = jnp.zeros((1, tm), F32)
        for g in range(N_GROUPS):
            hit = jnp.logical_and(gs[g] == m, found == 0.0)
            found = jnp.where(hit, 1.0, found)
            gsel[g] = jnp.where(hit, 1.0, gsel[g])
            gs[g] = jnp.where(hit, neg, gs[g])

    masked = [jnp.where(gsel[g] > 0.0, sel_g[g], neg) for g in range(N_GROUPS)]
    chosen = [jnp.zeros((per, tm), F32) for _ in range(N_GROUPS)]
    for _ in range(TOP_K):
        m = functools.reduce(jnp.maximum, [jnp.max(mg, axis=0, keepdims=True) for mg in masked])
        idx = functools.reduce(jnp.minimum, [
            jnp.min(jnp.where(masked[g] == m, flat_g[g], float(n_e)), axis=0, keepdims=True)
            for g in range(N_GROUPS)])
        for g in range(N_GROUPS):
            hit = flat_g[g] == idx
            chosen[g] = jnp.where(hit, 1.0, chosen[g])
            masked[g] = jnp.where(hit, neg, masked[g])
    chosen = jnp.concatenate(chosen, axis=0)
    flat = jnp.concatenate(flat_g, axis=0)

    w_dense = scores * chosen
    w_dense = w_dense / jnp.sum(w_dense, axis=0, keepdims=True) * ROUTED_SCALE

    chosen_bf = chosen.astype(BF16)
    lower = (lax.broadcasted_iota(jnp.int32, (n_e, n_e), 1)
             < lax.broadcasted_iota(jnp.int32, (n_e, n_e), 0)).astype(BF16)
    slot_idx = jnp.dot(lower, chosen_bf, preferred_element_type=F32)
    upper = (lax.broadcasted_iota(jnp.int32, (tm, tm), 0)
             < lax.broadcasted_iota(jnp.int32, (tm, tm), 1)).astype(BF16)
    rank = jnp.dot(chosen_bf, upper, preferred_element_type=F32) + carry_ref[...]
    carry_ref[...] += jnp.sum(chosen, axis=1, keepdims=True)

    slots = []
    for j in range(TOP_K):
        ind = jnp.where(slot_idx == float(j), chosen, 0.0)
        slots.append((jnp.sum(ind * flat, axis=0, keepdims=True),
                      jnp.sum(ind * w_dense, axis=0, keepdims=True),
                      jnp.sum(ind * rank, axis=0, keepdims=True)))
    return slots


def _merge_kernel(att_ref, rf_ref, rb_ref, gate_ref, x_ref, g1_ref, sh2_ref, sc2_ref, ga_ref, gr_ref,
                  wout_ref, n2g_ref, rwt_ref, rbias_ref, x1_ref, h2p_ref, se_ref, sw_ref, sr_ref, cnt_ref,
                  carry, *, att_scale):
    @pl.when(jnp.logical_and(pl.program_id(0) == 0, pl.program_id(1) == 0))
    def _():
        carry[...] = jnp.zeros_like(carry)

    att = att_ref[0].astype(F32)
    ret = rf_ref[0].astype(F32) + rb_ref[0].astype(F32)
    gate = gate_ref[0].astype(F32)
    parts = []
    for h in range(A_HEADS):
        a = att[:, h * 128:(h + 1) * 128]
        parts.append(_rmsnorm_rows(a, ga_ref[...]) * att_scale)
    for h in range(R_HEADS):
        r = _rmsnorm_rows(ret[:, h * R_DV:(h + 1) * R_DV], gr_ref[...])
        g = gate[:, h * R_DV:(h + 1) * R_DV]
        parts.append(g * jax.nn.sigmoid(g) * r)
    merged = jnp.concatenate(parts, axis=1).astype(BF16)
    y = jnp.dot(merged, wout_ref[...], preferred_element_type=F32)
    x1 = x_ref[0] + g1_ref[0] * y
    x1_ref[0] = x1
    h2 = _rmsnorm_rows(x1, n2g_ref[...]) * (1.0 + sc2_ref[0]) + sh2_ref[0]
    h2p_ref[0] = _pack_pairs(h2)
    logits_t = lax.dot_general(rwt_ref[...], h2.astype(BF16), _NT, preferred_element_type=F32)
    slots = _route_tile(logits_t, rbias_ref[...], carry)
    for j, (e_j, w_j, r_j) in enumerate(slots):
        se_ref[j:j + 1, :] = e_j.astype(jnp.int32)
        sw_ref[j:j + 1, :] = w_j
        sr_ref[j:j + 1, :] = r_j.astype(jnp.int32)
    cnt_ref[...] = carry[...]


def _merge(att, rf, rb, px, x, g1, sh2, sc2, ga, gr, wout_bf, n2g, rwt_bf, rbias, att_scale, tm):
    b, s, d = x.shape
    nt = s // tm
    row = lambda bi, i: (bi, i, 0)
    tok = lambda bi, i: (0, bi * nt + i)
    per_b = pl.BlockSpec((1, 1, d), lambda bi, i: (bi, 0, 0))
    const2 = lambda shape: pl.BlockSpec(shape, lambda bi, i: (0, 0))
    return pl.pallas_call(
        functools.partial(_merge_kernel, att_scale=att_scale),
        grid=(b, nt),
        in_specs=[pl.BlockSpec((1, tm, 512), row), pl.BlockSpec((1, tm, 512), row),
                  pl.BlockSpec((1, tm, 512), row),
                  pl.BlockSpec((1, tm, 512), lambda bi, i: (bi, i, P_RG // 512)),
                  pl.BlockSpec((1, tm, d), row), per_b, per_b, per_b,
                  const2((1, 128)), const2((1, R_DV)), const2((d, d)), const2((1, d)),
                  const2((N_EXPERTS, d)), const2((N_EXPERTS, 1))],
        out_specs=[pl.BlockSpec((1, tm, d), row), pl.BlockSpec((1, tm, d // 2), row),
                   pl.BlockSpec((TOP_K, tm), tok), pl.BlockSpec((TOP_K, tm), tok),
                   pl.BlockSpec((TOP_K, tm), tok), const2((N_EXPERTS, 1))],
        out_shape=[jax.ShapeDtypeStruct((b, s, d), F32), jax.ShapeDtypeStruct((b, s, d // 2), U32),
                   jax.ShapeDtypeStruct((TOP_K, b * s), jnp.int32),
                   jax.ShapeDtypeStruct((TOP_K, b * s), F32),
                   jax.ShapeDtypeStruct((TOP_K, b * s), jnp.int32),
                   jax.ShapeDtypeStruct((N_EXPERTS, 1), F32)],
        scratch_shapes=[pltpu.VMEM((N_EXPERTS, 1), F32)],
        compiler_params=_cparams(("arbitrary", "arbitrary")),
        name="mixer_merge",
    )(att, rf, rb, px, x, g1, sh2, sc2, ga, gr, wout_bf, n2g, rwt_bf, rbias)


def _dispatch_kernel(dest_ref, h_ref, xs_ref, buf, sem):
    tm = h_ref.shape[0]
    i = pl.program_id(0)
    n = pl.num_programs(0)
    slot = i % 2

    def wait_slot(s):
        for _ in range(TOP_K):
            pltpu.make_async_copy(buf.at[s], xs_ref.at[pl.ds(0, tm)], sem.at[s]).wait()

    @pl.when(i >= 2)
    def _():
        wait_slot(slot)

    buf[slot] = h_ref[...]

    def body(t, carry):
        for j in range(TOP_K):
            d = dest_ref[j, t]
            pltpu.make_async_copy(buf.at[slot, pl.ds(t, 1)], xs_ref.at[pl.ds(d, 1)], sem.at[slot]).start()
        return carry

    lax.fori_loop(0, tm, body, 0)

    @pl.when(i == n - 1)
    def _():
        wait_slot(slot)

    @pl.when(jnp.logical_and(i == n - 1, n >= 2))
    def _():
        wait_slot(1 - slot)


def _dispatch(dest_t, h2p, tm):
    t, kw = h2p.shape
    n = t // tm
    return pl.pallas_call(
        _dispatch_kernel,
        grid=(n,),
        in_specs=[pl.BlockSpec((TOP_K, tm), lambda i: (0, i), memory_space=pltpu.SMEM),
                  pl.BlockSpec((tm, kw), lambda i: (i, 0))],
        out_specs=pl.BlockSpec(memory_space=pl.ANY),
        out_shape=jax.ShapeDtypeStruct((t * TOP_K, kw), U32),
        scratch_shapes=[pltpu.VMEM((2, tm, kw), U32), pltpu.SemaphoreType.DMA((2,))],
        compiler_params=_cparams(("arbitrary",)),
        name="moe_dispatch",
    )(dest_t, h2p)


def _expert_kernel(iblk_ref, iexp_ref, ilo_ref, ihi_ref, xs_ref, w1_ref, w3_ref, w2_ref, ys_ref,
                   acc, w1b, w3b, w2b):
    r = xs_ref.shape[0]
    i = pl.program_id(0)
    blk = iblk_ref[i]
    lo = ilo_ref[i]
    hi = ihi_ref[i]
    prev = jnp.maximum(i - 1, 0)
    first = jnp.logical_or(i == 0, iblk_ref[prev] != blk)

    @pl.when(jnp.logical_or(i == 0, iexp_ref[prev] != iexp_ref[i]))
    def _():
        w1b[...] = w1_ref[0].astype(BF16)
        w3b[...] = w3_ref[0].astype(BF16)
        w2b[...] = w2_ref[0].astype(BF16)

    @pl.when(i == 0)
    def _():
        acc[...] = jnp.zeros_like(acc)

    n_half = 2
    rh = r // n_half
    for hf in range(n_half):
        sl = slice(hf * rh, (hf + 1) * rh)
        rows = blk * r + hf * rh + lax.broadcasted_iota(jnp.int32, (rh, 1), 0)
        valid = jnp.logical_and(rows >= lo, rows < hi)
        xlo, xhi = _unpack_pairs(xs_ref[sl, :])
        x = jnp.where(valid, jnp.concatenate([xlo, xhi], axis=1), 0.0).astype(BF16)
        a = jnp.dot(x, w1b[...], preferred_element_type=F32)
        bb = jnp.dot(x, w3b[...], preferred_element_type=F32)
        hmid = (a * jax.nn.sigmoid(a) * bb).astype(BF16)
        y = jnp.dot(hmid, w2b[...], preferred_element_type=F32)
        y = jnp.where(first, y, acc[sl, :] + y)
        acc[sl, :] = y
        ys_ref[sl, :] = _pack_pairs(y)


def _experts(iblk, iexp, ilo, ihi, xs, w1, w3, w2, r):
    na, kw = xs.shape
    n_items = iblk.shape[0]
    d = w1.shape[1]
    de = w1.shape[2]
    grid_spec = pltpu.PrefetchScalarGridSpec(
        num_scalar_prefetch=4,
        grid=(n_items,),
        in_specs=[pl.BlockSpec((r, kw), lambda i, ib, ie, il, ih: (ib[i], 0)),
                  pl.BlockSpec((1, d, de), lambda i, ib, ie, il, ih: (ie[i], 0, 0)),
                  pl.BlockSpec((1, d, de), lambda i, ib, ie, il, ih: (ie[i], 0, 0)),
                  pl.BlockSpec((1, de, d), lambda i, ib, ie, il, ih: (ie[i], 0, 0))],
        out_specs=pl.BlockSpec((r, kw), lambda i, ib, ie, il, ih: (ib[i], 0)),
        scratch_shapes=[pltpu.VMEM((r, d), F32), pltpu.VMEM((d, de), BF16), pltpu.VMEM((d, de), BF16),
                        pltpu.VMEM((de, d), BF16)],
    )
    return pl.pallas_call(
        _expert_kernel,
        grid_spec=grid_spec,
        out_shape=jax.ShapeDtypeStruct((na, kw), U32),
        compiler_params=_cparams(("arbitrary",)),
        name="moe_experts",
    )(iblk, iexp, ilo, ihi, xs, w1, w3, w2)


def _combine_kernel(dest_ref, dnext_ref, x1_ref, h2p_ref, wts_ref, g2_ref, sw1_ref, sw3_ref, sw2_ref,
                    fg_ref, ys_ref, o_ref, gbuf, sem):
    tm = x1_ref.shape[0]
    i = pl.program_id(0)
    n = pl.num_programs(0)
    slot = i % 2

    def issue(dref, s):
        def body(t, carry):
            for j in range(TOP_K):
                d = dref[j, t]
                pltpu.make_async_copy(ys_ref.at[pl.ds(d, 1)], gbuf.at[s, pl.ds(j * tm + t, 1)],
                                      sem.at[s]).start()
            return carry
        lax.fori_loop(0, tm, body, 0)

    @pl.when(i == 0)
    def _():
        issue(dest_ref, 0)

    @pl.when(i + 1 < n)
    def _():
        issue(dnext_ref, 1 - slot)

    for _ in range(TOP_K):
        pltpu.make_async_copy(ys_ref.at[pl.ds(0, tm)], gbuf.at[slot, pl.ds(0, tm)], sem.at[slot]).wait()

    wts = wts_ref[...]
    moe_lo = None
    moe_hi = None
    for j in range(TOP_K):
        ylo, yhi = _unpack_pairs(gbuf[slot, j * tm:(j + 1) * tm, :])
        wj = wts[:, j:j + 1]
        moe_lo = ylo * wj if moe_lo is None else moe_lo + ylo * wj
        moe_hi = yhi * wj if moe_hi is None else moe_hi + yhi * wj
    hlo, hhi = _unpack_pairs(h2p_ref[...])
    h = jnp.concatenate([hlo, hhi], axis=1).astype(BF16)
    a = jnp.dot(h, sw1_ref[...], preferred_element_type=F32)
    bb = jnp.dot(h, sw3_ref[...], preferred_element_type=F32)
    shared = jnp.dot((a * jax.nn.sigmoid(a) * bb).astype(BF16), sw2_ref[...], preferred_element_type=F32)
    ffn = jnp.concatenate([moe_lo, moe_hi], axis=1) + shared
    xo = x1_ref[...] + g2_ref[0] * ffn
    o_ref[...] = _rmsnorm_rows(xo, fg_ref[...])


def _combine(dest_t, x1, h2p, wts, g2, sw1, sw3, sw2, fg, ys, tiles_per_batch, tm):
    t, d = x1.shape
    kw = d // 2
    n = t // tm
    de = sw1.shape[1]
    const2 = lambda shape: pl.BlockSpec(shape, lambda i: (0, 0))
    return pl.pallas_call(
        _combine_kernel,
        grid=(n,),
        in_specs=[pl.BlockSpec((TOP_K, tm), lambda i: (0, i), memory_space=pltpu.SMEM),
                  pl.BlockSpec((TOP_K, tm), lambda i: (0, jnp.minimum(i + 1, n - 1)),
                               memory_space=pltpu.SMEM),
                  pl.BlockSpec((tm, d), lambda i: (i, 0)),
                  pl.BlockSpec((tm, kw), lambda i: (i, 0)),
                  pl.BlockSpec((tm, TOP_K), lambda i: (i, 0)),
                  pl.BlockSpec((1, 1, d), lambda i: (i // tiles_per_batch, 0, 0)),
                  const2((d, de)), const2((d, de)), const2((de, d)), const2((1, d)),
                  pl.BlockSpec(memory_space=pl.ANY)],
        out_specs=pl.BlockSpec((tm, d), lambda i: (i, 0)),
        out_shape=jax.ShapeDtypeStruct((t, d), F32),
        scratch_shapes=[pltpu.VMEM((2, tm * TOP_K, kw), U32), pltpu.SemaphoreType.DMA((2,))],
        compiler_params=_cparams(("arbitrary",)),
        name="moe_combine",
    )(dest_t, dest_t, x1, h2p, wts, g2, sw1, sw3, sw2, fg, ys)


def _route_meta(counts, slot_e, slot_r, block_rows):
    t = slot_e.shape[1]
    ends = jnp.cumsum(counts)
    starts = ends - counts
    onehot = slot_e[:, :, None] == jnp.arange(N_EXPERTS, dtype=jnp.int32)
    dest_t = slot_r + jnp.sum(jnp.where(onehot, starts, 0), axis=-1).astype(jnp.int32)

    n_blk = t * TOP_K // block_rows
    n_items = n_blk + N_EXPERTS - 1
    first_blk = starts // block_rows
    last_blk = (ends - 1) // block_rows
    n_e = jnp.where(counts > 0, last_blk - first_blk + 1, 0)
    item_end = jnp.cumsum(n_e)
    item_off = item_end - n_e
    total = item_end[-1]
    idx = jnp.arange(n_items, dtype=jnp.int32)
    e_of = jnp.minimum(jnp.sum(item_end[None, :] <= idx[:, None], axis=1), N_EXPERTS - 1).astype(jnp.int32)
    live = idx < total
    blk = first_blk[e_of] + (idx - item_off[e_of])
    lo = jnp.maximum(starts[e_of], blk * block_rows)
    hi = jnp.minimum(ends[e_of], (blk + 1) * block_rows)
    last_e = e_of[jnp.maximum(total - 1, 0)]
    iblk = jnp.where(live, blk, n_blk - 1).astype(jnp.int32)
    iexp = jnp.where(live, e_of, last_e).astype(jnp.int32)
    ilo = jnp.where(live, lo, 0).astype(jnp.int32)
    ihi = jnp.where(live, hi, 0).astype(jnp.int32)
    return dest_t, iblk, iexp, ilo, ihi


def _rope_tables(n_tokens):
    n_rows = n_tokens // GRID_W
    rows = jnp.repeat(jnp.arange(n_rows, dtype=F32), GRID_W)
    cols = jnp.tile(jnp.arange(GRID_W, dtype=F32), n_rows)
    n_freq = HEAD_DIM // 4
    inv_freq = ROPE_BASE ** (-jnp.arange(n_freq, dtype=F32) / n_freq)
    ar = rows[:, None] * inv_freq
    ac = cols[:, None] * inv_freq
    cos64 = jnp.concatenate([jnp.cos(ar), jnp.cos(ar), jnp.cos(ac), jnp.cos(ac)], axis=1)
    sin64 = jnp.concatenate([-jnp.sin(ar), jnp.sin(ar), -jnp.sin(ac), jnp.sin(ac)], axis=1)
    return jnp.tile(cos64, (1, 2)), jnp.tile(sin64, (1, 2))


def kernel(x, c, ctx, c_ctx, ada_w, ada_b, norm1_g, w_in, lambda_q1, lambda_k1, lambda_q2, lambda_k2,
           dattn_norm_g, ret_decay_fwd, ret_decay_bwd, ret_norm_g, w_out, norm2_g, router_w, router_bias,
           exp_w1, exp_w3, exp_w2, shared_w1, shared_w3, shared_w2, final_norm_g):
    b, s, d = x.shape
    lc = ctx.shape[1]
    depth = ada_w.shape[0]
    assert depth == 1, "single-layer problem: the context stream is never updated"
    l = 0
    lam_init = 0.8 - 0.6 * math.exp(-0.3 * l)

    c8 = jnp.zeros((8, d), F32).at[:b].set(c).at[b].set(c_ctx)
    mod = _modulation(c8, ada_w[l], ada_b[l])
    sh1, sc1, g1, sh2, sc2, g2 = [mod[:b, k * d:(k + 1) * d].reshape(b, 1, d) for k in range(6)]
    csh1 = jnp.broadcast_to(mod[b, 0:d].reshape(1, 1, d), (b, 1, d))
    csc1 = jnp.broadcast_to(mod[b, d:2 * d].reshape(1, 1, d), (b, 1, d))

    w_l = w_in[l]
    w_tok = jnp.concatenate([w_l[:, 512:1024], w_l[:, 1536:3072]], axis=1).astype(BF16)
    w_feat = jnp.concatenate([w_l[:, 0:512], w_l[:, 1024:1536]], axis=1).T.astype(BF16)
    n1g = norm1_g[l].reshape(1, d)
    cos, sin = _rope_tables(s)
    ones_c = jnp.ones((lc, LANES), F32)
    zeros_c = jnp.zeros((lc, LANES), F32)
    px, qt, vtx = _inproj(x, n1g, sc1, sh1, cos, sin, cos.T, sin.T, w_tok, w_feat, tm=512)
    pc, _, vtc = _inproj(ctx, n1g, csc1, csh1, ones_c, zeros_c, ones_c.T, zeros_c.T, w_tok, w_feat, tm=lc)

    lam = (jnp.exp(jnp.sum(lambda_q1[l] * lambda_k1[l])) - jnp.exp(jnp.sum(lambda_q2[l] * lambda_k2[l]))
           ).astype(F32) + lam_init
    att = _attention(lam.reshape(1), qt, px, vtx, pc, vtc, tq=512, tkc=min(512, s // 4))

    dmat, qdec, kdec, gch = _retention_tables(ret_decay_fwd[l], ret_decay_bwd[l])
    ret_f, ret_b = _retention(px, pc, dmat, qdec, kdec, gch)

    x1, h2p, slot_e, slot_w, slot_r, counts = _merge(
        att, ret_f, ret_b, px, x, g1, sh2, sc2,
        dattn_norm_g[l].reshape(1, 2 * HEAD_DIM), ret_norm_g[l].reshape(1, R_DV),
        w_out[l].astype(BF16), norm2_g[l].reshape(1, d), router_w[l].T.astype(BF16),
        router_bias[l].astype(F32).reshape(N_EXPERTS, 1), 1.0 - lam_init, tm=512)

    t = b * s
    block_rows = 512
    dest_t, iblk, iexp, ilo, ihi = _route_meta(counts[:, 0].astype(jnp.int32), slot_e, slot_r, block_rows)
    h2p = h2p.reshape(t, d // 2)
    xs = _dispatch(dest_t, h2p, tm=128)
    ys = _experts(iblk, iexp, ilo, ihi, xs, exp_w1[l], exp_w3[l], exp_w2[l], block_rows)
    tm_c = 128
    out = _combine(dest_t, x1.reshape(t, d), h2p, slot_w.T, g2, shared_w1[l].astype(BF16),
                   shared_w3[l].astype(BF16), shared_w2[l].astype(BF16), final_norm_g.reshape(1, d),
                   ys, s // tm_c, tm_c)
    return out.reshape(b, s, d)
```

```python
import functools
import math

import jax
import jax.numpy as jnp
from jax import lax
from jax.experimental import pallas as pl
from jax.experimental.pallas import tpu as pltpu

F32 = jnp.float32
BF16 = jnp.bfloat16
U32 = jnp.uint32

D_MODEL = 1024
GRID_W = 64
HEAD_DIM = 64
A_HEADS = 4
R_HEADS = 4
R_DK = 64
R_DV = 128
RET_CHUNK = 128
ROPE_BASE = 10000.0
N_EXPERTS = 64
N_GROUPS = 8
TOPK_GROUPS = 4
TOP_K = 8
D_EXPERT = 256
ROUTED_SCALE = 2.5
NORM_EPS = 1e-6
LANES = 128

P_KZ = 0
P_RQ = 1024
P_RK = 1280
P_RV = 1536
P_RG = 2048
P_COLS = 2560
VT_ROWS = 144
LOG2E = 1.4426950408889634

VMEM_LIMIT = 56 * 1024 * 1024


def _cparams(sem):
    return pltpu.CompilerParams(dimension_semantics=sem, vmem_limit_bytes=VMEM_LIMIT)


def _mod_kernel(c_ref, w_ref, b_ref, o_ref):
    c = c_ref[...]
    s = c * jax.nn.sigmoid(c)
    o_ref[...] = jnp.dot(s, w_ref[...], preferred_element_type=F32,
                         precision=lax.Precision.HIGHEST) + b_ref[...]


def _modulation(c8, ada_w, ada_b):
    d = c8.shape[1]
    n = ada_w.shape[1]
    tn = 512
    return pl.pallas_call(
        _mod_kernel,
        grid=(n // tn,),
        in_specs=[pl.BlockSpec((8, d), lambda j: (0, 0)),
                  pl.BlockSpec((d, tn), lambda j: (0, j)),
                  pl.BlockSpec((1, tn), lambda j: (0, j))],
        out_specs=pl.BlockSpec((8, tn), lambda j: (0, j)),
        out_shape=jax.ShapeDtypeStruct((8, n), F32),
        compiler_params=_cparams(("arbitrary",)),
        name="modulation",
    )(c8, ada_w, ada_b.reshape(1, n))


def _rmsnorm_rows(x, g):
    return x * lax.rsqrt(jnp.mean(x * x, axis=-1, keepdims=True) + NORM_EPS) * g


def _inproj_kernel(x_ref, g_ref, sc_ref, sh_ref, cos_ref, sin_ref, cost_ref, sint_ref, w_ref, wt_ref,
                   p_ref, qt_ref, vt_ref):
    x = x_ref[0]
    tm = x.shape[0]
    h = _rmsnorm_rows(x, g_ref[...]) * (1.0 + sc_ref[0]) + sh_ref[0]
    hb = h.astype(BF16)
    p = jnp.dot(hb, w_ref[...], preferred_element_type=F32)
    pt = lax.dot_general(wt_ref[...], hb, _NT, preferred_element_type=F32)

    cos = cos_ref[...]
    sin = sin_ref[...]
    lane = lax.broadcasted_iota(jnp.int32, (tm, LANES), 1)
    first_half = (lane % 32) < 16
    low64 = lane < 64

    def rope(xs):
        partner = jnp.where(first_half, pltpu.roll(xs, LANES - 16, 1), pltpu.roll(xs, 16, 1))
        return xs * cos + partner * sin

    zero = jnp.zeros((tm, LANES), BF16)
    for hh in range(A_HEADS):
        k = rope(p[:, hh * 128:(hh + 1) * 128]).astype(BF16)
        p_ref[0, :, P_KZ + hh * 256:P_KZ + hh * 256 + 128] = jnp.where(low64, k, zero)
        p_ref[0, :, P_KZ + hh * 256 + 128:P_KZ + (hh + 1) * 256] = jnp.where(low64, zero, k)
    for cc in range(2):
        rq = rope(p[:, 512 + cc * 128:512 + (cc + 1) * 128])
        p_ref[0, :, P_RQ + cc * 128:P_RQ + (cc + 1) * 128] = rq.astype(BF16)
        rk = rope(p[:, 768 + cc * 128:768 + (cc + 1) * 128]) * (R_DK ** -0.5)
        p_ref[0, :, P_RK + cc * 128:P_RK + (cc + 1) * 128] = rk.astype(BF16)
    p_ref[0, :, P_RV:P_RV + 512] = p[:, 1024:1536].astype(BF16)
    p_ref[0, :, P_RG:P_RG + 512] = p[:, 1536:2048].astype(BF16)

    cost = cost_ref[...]
    sint = sint_ref[...]
    q_scale = (HEAD_DIM ** -0.5) * LOG2E
    ones_rows = jnp.where(lax.broadcasted_iota(jnp.int32, (VT_ROWS - 128, tm), 0) == 0, 1.0, 0.0)
    for hh in range(A_HEADS):
        xq = pt[hh * 128:(hh + 1) * 128, :]
        groups = [xq[16 * gi:16 * (gi + 1), :] for gi in range(8)]
        partner = jnp.concatenate([groups[gi ^ 1] for gi in range(8)], axis=0)
        qt_ref[0, hh] = ((xq * cost + partner * sint) * q_scale).astype(BF16)
        vt_ref[0, hh, 0:128, :] = pt[512 + hh * 128:512 + (hh + 1) * 128, :].astype(BF16)
        vt_ref[0, hh, 128:VT_ROWS, :] = ones_rows.astype(BF16)


def _inproj(x, g, sc, sh, cos, sin, cost, sint, w_bf, wt_bf, tm):
    b, s, d = x.shape
    n_tok = w_bf.shape[1]
    n_t = wt_bf.shape[0]
    const2 = lambda shape: pl.BlockSpec(shape, lambda bi, i: (0, 0))
    per_b = pl.BlockSpec((1, 1, d), lambda bi, i: (bi, 0, 0))
    return pl.pallas_call(
        _inproj_kernel,
        grid=(b, s // tm),
        in_specs=[pl.BlockSpec((1, tm, d), lambda bi, i: (bi, i, 0)),
                  const2((1, d)), per_b, per_b,
                  pl.BlockSpec((tm, LANES), lambda bi, i: (i, 0)),
                  pl.BlockSpec((tm, LANES), lambda bi, i: (i, 0)),
                  pl.BlockSpec((LANES, tm), lambda bi, i: (0, i)),
                  pl.BlockSpec((LANES, tm), lambda bi, i: (0, i)),
                  const2((d, n_tok)), const2((n_t, d))],
        out_specs=[pl.BlockSpec((1, tm, P_COLS), lambda bi, i: (bi, i, 0)),
                   pl.BlockSpec((1, A_HEADS, LANES, tm), lambda bi, i: (bi, 0, 0, i)),
                   pl.BlockSpec((1, A_HEADS, VT_ROWS, tm), lambda bi, i: (bi, 0, 0, i))],
        out_shape=[jax.ShapeDtypeStruct((b, s, P_COLS), BF16),
                   jax.ShapeDtypeStruct((b, A_HEADS, LANES, s), BF16),
                   jax.ShapeDtypeStruct((b, A_HEADS, VT_ROWS, s), BF16)],
        compiler_params=_cparams(("arbitrary", "arbitrary")),
        name="inproj",
    )(x, g, sc, sh, cos, sin, cost, sint, w_bf, wt_bf)


_NT = (((1,), (1,)), ((), ()))


def _attn_kernel(lam_ref, qt_ref, kzc_ref, vtc_ref, kzx_ref, vtx_ref, o_ref,
                 acc, s_a, s_b, e_a, e_b, *, tkc):
    qt = qt_ref[0, 0]
    tq = qt.shape[1]
    n_chunks = kzx_ref.shape[1] // tkc
    assert n_chunks % 2 == 0 and n_chunks >= 4

    def scores(kz, s_ref):
        for mp in range(2):
            s_ref[mp] = jnp.dot(kz[:, mp * LANES:(mp + 1) * LANES], qt, preferred_element_type=F32)

    def softmax(s_ref, e_ref, ms):
        new_ms, alphas = [], []
        for mp in range(2):
            st = s_ref[mp]
            m_new = jnp.maximum(ms[mp], jnp.max(st, axis=0, keepdims=True))
            alphas.append(jnp.exp2(ms[mp] - m_new))
            e_ref[mp] = jnp.exp2(st - m_new).astype(BF16)
            new_ms.append(m_new)
        return tuple(new_ms), tuple(alphas)

    def values(vt, e_ref, alphas):
        for mp in range(2):
            acc[mp] = alphas[mp] * acc[mp] + jnp.dot(vt, e_ref[mp], preferred_element_type=F32)

    def kchunk(c):
        return kzx_ref[0, pl.ds(pl.multiple_of(c * tkc, tkc), tkc), :]

    def vchunk(c):
        return vtx_ref[0, 0, :, pl.ds(pl.multiple_of(c * tkc, tkc), tkc)]

    ms = []
    for mp in range(2):
        st = jnp.dot(kzc_ref[0, :, mp * LANES:(mp + 1) * LANES], qt, preferred_element_type=F32)
        m_new = jnp.max(st, axis=0, keepdims=True)
        acc[mp] = jnp.dot(vtc_ref[0, 0], jnp.exp2(st - m_new).astype(BF16), preferred_element_type=F32)
        ms.append(m_new)
    ms = tuple(ms)

    scores(kchunk(0), s_a)
    scores(kchunk(1), s_b)
    ms, al_a = softmax(s_a, e_a, ms)

    def body(j, carry):
        ms, al_a = carry
        scores(kchunk(2 * j + 2), s_a)
        ms, al_b = softmax(s_b, e_b, ms)
        values(vchunk(2 * j), e_a, al_a)
        scores(kchunk(2 * j + 3), s_b)
        ms, al_a = softmax(s_a, e_a, ms)
        values(vchunk(2 * j + 1), e_b, al_b)
        return ms, al_a

    ms, al_a = lax.fori_loop(0, n_chunks // 2 - 1, body, (ms, al_a))
    ms, al_b = softmax(s_b, e_b, ms)
    values(vchunk(n_chunks - 2), e_a, al_a)
    values(vchunk(n_chunks - 1), e_b, al_b)

    a1 = acc[0]
    a2 = acc[1]
    o1 = a1[:LANES, :] / a1[LANES:LANES + 1, :]
    o2 = a2[:LANES, :] / a2[LANES:LANES + 1, :]
    o_ref[0] = (o1 - lam_ref[0] * o2).T.astype(BF16)


def _attention(lam, qt, px, vtx, pc, vtc, tq, tkc):
    b, s, _ = px.shape
    lc = pc.shape[1]
    kern = functools.partial(_attn_kernel, tkc=tkc)
    return pl.pallas_call(
        kern,
        grid=(b, A_HEADS, s // tq),
        in_specs=[pl.BlockSpec(memory_space=pltpu.SMEM),
                  pl.BlockSpec((1, 1, LANES, tq), lambda bi, h, i: (bi, h, 0, i)),
                  pl.BlockSpec((1, lc, 256), lambda bi, h, i: (bi, 0, P_KZ // 256 + h)),
                  pl.BlockSpec((1, 1, VT_ROWS, lc), lambda bi, h, i: (bi, h, 0, 0)),
                  pl.BlockSpec((1, s, 256), lambda bi, h, i: (bi, 0, P_KZ // 256 + h)),
                  pl.BlockSpec((1, 1, VT_ROWS, s), lambda bi, h, i: (bi, h, 0, 0))],
        out_specs=pl.BlockSpec((1, tq, LANES), lambda bi, h, i: (bi, i, h)),
        out_shape=jax.ShapeDtypeStruct((b, s, A_HEADS * LANES), BF16),
        scratch_shapes=[pltpu.VMEM((2, VT_ROWS, tq), F32),
                        pltpu.VMEM((2, tkc, tq), F32), pltpu.VMEM((2, tkc, tq), F32),
                        pltpu.VMEM((2, tkc, tq), BF16), pltpu.VMEM((2, tkc, tq), BF16)],
        compiler_params=_cparams(("arbitrary", "arbitrary", "arbitrary")),
        name="diff_attention",
    )(lam, qt, pc, vtc, px, vtx)


_TN = (((0,), (0,)), ((), ()))


def _ret_kernel(qf_ref, kf_ref, vf_ref, qb_ref, kb_ref, vb_ref, kc_ref, vc_ref,
                dmat_ref, qdec_ref, kdec_ref, gch_ref, of_ref, ob_ref, sf, sb):
    c = RET_CHUNK
    n = pl.program_id(1)
    col_head = lax.broadcasted_iota(jnp.int32, (c, R_HEADS * R_DK), 1) // R_DK
    bd_mask = (lax.broadcasted_iota(jnp.int32, (R_HEADS * R_DK, R_HEADS * R_DV), 0) // R_DK
               == lax.broadcasted_iota(jnp.int32, (R_HEADS * R_DK, R_HEADS * R_DV), 1) // R_DV)

    def state_update(s_ref, k, v, kdec, gch):
        kd = (k.astype(F32) * kdec).astype(BF16)
        kv = lax.dot_general(kd, v, _TN, preferred_element_type=F32)
        s_ref[...] = gch * s_ref[...] + jnp.where(bd_mask, kv, 0.0)

    def chunk(s_ref, q, k, v, dmat, qdec, kdec, gch):
        zero = jnp.zeros_like(k)
        kblk = jnp.concatenate([jnp.where(col_head == h, k, zero) for h in range(R_HEADS)], axis=0)
        sc = lax.dot_general(q, kblk, _NT, preferred_element_type=F32) * dmat
        p = sc.astype(BF16)
        intra = jnp.concatenate(
            [jnp.dot(p[:, h * c:(h + 1) * c], v[:, h * R_DV:(h + 1) * R_DV], preferred_element_type=F32)
             for h in range(R_HEADS)], axis=1)
        qd = (q.astype(F32) * qdec).astype(BF16)
        cross = jnp.dot(qd, s_ref[...].astype(BF16), preferred_element_type=F32)
        out = intra + cross
        state_update(s_ref, k, v, kdec, gch)
        return out

    @pl.when(n == 0)
    def _():
        sf[...] = jnp.zeros_like(sf)
        sb[...] = jnp.zeros_like(sb)
        n_ctx = kc_ref.shape[1] // c
        for cc in range(n_ctx):
            state_update(sf, kc_ref[0, cc * c:(cc + 1) * c, :], vc_ref[0, cc * c:(cc + 1) * c, :],
                         kdec_ref[0], gch_ref[0])
        for cc in reversed(range(n_ctx)):
            state_update(sb, kc_ref[0, cc * c:(cc + 1) * c, :], vc_ref[0, cc * c:(cc + 1) * c, :],
                         kdec_ref[1], gch_ref[1])

    of_ref[0] = chunk(sf, qf_ref[0], kf_ref[0], vf_ref[0], dmat_ref[0], qdec_ref[0], kdec_ref[0],
                      gch_ref[0]).astype(BF16)
    ob_ref[0] = chunk(sb, qb_ref[0], kb_ref[0], vb_ref[0], dmat_ref[1], qdec_ref[1], kdec_ref[1],
                      gch_ref[1]).astype(BF16)


def _retention(px, pc, dmat, qdec, kdec, gch):
    b, s, _ = px.shape
    lc = pc.shape[1]
    c = RET_CHUNK
    nch = s // c
    qk_w = R_HEADS * R_DK
    v_w = R_HEADS * R_DV
    fwd = lambda bi, n: (bi, n, 0)
    spec_q = lambda im: pl.BlockSpec((1, c, qk_w), lambda bi, n: im(bi, n)[:2] + (P_RQ // qk_w,))
    spec_k = lambda im: pl.BlockSpec((1, c, qk_w), lambda bi, n: im(bi, n)[:2] + (P_RK // qk_w,))
    spec_v = lambda im: pl.BlockSpec((1, c, v_w), lambda bi, n: im(bi, n)[:2] + (P_RV // v_w,))
    bwd = lambda bi, n: (bi, nch - 1 - n, 0)
    const3 = lambda shape: pl.BlockSpec(shape, lambda bi, n: (0, 0, 0))
    return pl.pallas_call(
        _ret_kernel,
        grid=(b, nch),
        in_specs=[spec_q(fwd), spec_k(fwd), spec_v(fwd), spec_q(bwd), spec_k(bwd), spec_v(bwd),
                  pl.BlockSpec((1, lc, qk_w), lambda bi, n: (bi, 0, P_RK // qk_w)),
                  pl.BlockSpec((1, lc, v_w), lambda bi, n: (bi, 0, P_RV // v_w)),
                  const3((2, c, R_HEADS * c)), const3((2, c, qk_w)), const3((2, c, qk_w)),
                  const3((2, 1, v_w))],
        out_specs=[pl.BlockSpec((1, c, v_w), fwd), pl.BlockSpec((1, c, v_w), bwd)],
        out_shape=[jax.ShapeDtypeStruct((b, s, v_w), BF16), jax.ShapeDtypeStruct((b, s, v_w), BF16)],
        scratch_shapes=[pltpu.VMEM((qk_w, v_w), F32), pltpu.VMEM((qk_w, v_w), F32)],
        compiler_params=_cparams(("arbitrary", "arbitrary")),
        name="retention",
    )(px, px, px, px, px, px, pc, pc, dmat, qdec, kdec, gch)


def _retention_tables(decay_fwd, decay_bwd):
    c = RET_CHUNK
    pos = jnp.arange(c, dtype=F32)
    rel = pos[:, None] - pos[None, :]

    def tables(logits, backward):
        lg = jax.nn.log_sigmoid(logits.astype(F32))
        r = -rel if backward else rel
        dm = jnp.where(r[None] >= 0, jnp.exp(jnp.maximum(r, 0.0)[None] * lg[:, None, None]), 0.0)
        dm = jnp.transpose(dm, (1, 0, 2)).reshape(c, R_HEADS * c)
        qpow = (c - pos) if backward else (pos + 1)
        kpow = pos if backward else (c - 1 - pos)
        qd = jnp.repeat(jnp.exp(qpow[:, None] * lg[None, :]), R_DK, axis=1)
        kd = jnp.repeat(jnp.exp(kpow[:, None] * lg[None, :]), R_DK, axis=1)
        gc = jnp.repeat(jnp.exp(c * lg), R_DV)[None, :]
        return dm, qd, kd, gc

    tf = tables(decay_fwd, False)
    tb = tables(decay_bwd, True)
    return tuple(jnp.stack([a, bb]) for a, bb in zip(tf, tb))


def _pack_pairs(x):
    k = x.shape[1] // 2
    lo = lax.bitcast_convert_type(x[:, :k].astype(BF16).astype(F32), U32)
    hi = lax.bitcast_convert_type(x[:, k:].astype(BF16).astype(F32), U32)
    return (lo >> 16) | (hi & jnp.uint32(0xFFFF0000))


def _unpack_pairs(w):
    lo = lax.bitcast_convert_type(w << 16, F32)
    hi = lax.bitcast_convert_type(w & jnp.uint32(0xFFFF0000), F32)
    return lo, hi


ROW_CHUNKS = (D_MODEL // 2) // LANES


def _store_token_rows(ref, first_token, n_tokens, x):
    packed = _pack_pairs(x)
    for c in range(ROW_CHUNKS):
        ref[pl.ds(first_token * ROW_CHUNKS + c, n_tokens, stride=ROW_CHUNKS), :] = (
            packed[:, c * LANES:(c + 1) * LANES])


def _load_token_rows(ref, first_token, n_tokens):
    return [_unpack_pairs(ref[pl.ds(first_token * ROW_CHUNKS + c, n_tokens, stride=ROW_CHUNKS), :])
            for c in range(ROW_CHUNKS)]


def _rows_to_features(chunks):
    return jnp.concatenate([lo for lo, _ in chunks] + [hi for _, hi in chunks], axis=1)


def _route_tile(logits_t, bias_col, carry_ref):
    n_e, tm = logits_t.shape
    per = n_e // N_GROUPS
    neg = -jnp.inf
    scores = jax.nn.sigmoid(logits_t)
    sel = scores + bias_col
    sub = lax.broadcasted_iota(jnp.int32, (per, tm), 0).astype(F32)
    sel_g = [sel[g * per:(g + 1) * per, :] for g in range(N_GROUPS)]
    flat_g = [sub + float(g * per) for g in range(N_GROUPS)]

    gs = []
    for g in range(N_GROUPS):
        top1 = jnp.max(sel_g[g], axis=0, keepdims=True)
        idx1 = jnp.min(jnp.where(sel_g[g] == top1, sub, float(per)), axis=0, keepdims=True)
        top2 = jnp.max(jnp.where(sub == idx1, neg, sel_g[g]), axis=0, keepdims=True)
        gs.append(top1 + top2)
    gsel = [jnp.zeros((1, tm), F32) for _ in range(N_GROUPS)]
    for _ in range(TOPK_GROUPS):
        m = functools.reduce(jnp.maximum, gs)
        found = jnp.zeros((1, tm), F32)
        for g in range(N_GROUPS):
            hit = jnp.logical_and(gs[g] == m, found == 0.0)
            found = jnp.where(hit, 1.0, found)
            gsel[g] = jnp.where(hit, 1.0, gsel[g])
            gs[g] = jnp.where(hit, neg, gs[g])

    masked = [jnp.where(gsel[g] > 0.0, sel_g[g], neg) for g in range(N_GROUPS)]
    chosen = [jnp.zeros((per, tm), F32) for _ in range(N_GROUPS)]
    for _ in range(TOP_K):
        m = functools.reduce(jnp.maximum, [jnp.max(mg, axis=0, keepdims=True) for mg in masked])
        idx = functools.reduce(jnp.minimum, [
            jnp.min(jnp.where(masked[g] == m, flat_g[g], float(n_e)), axis=0, keepdims=True)
            for g in range(N_GROUPS)])
        for g in range(N_GROUPS):
            hit = flat_g[g] == idx
            chosen[g] = jnp.where(hit, 1.0, chosen[g])
            masked[g] = jnp.where(hit, neg, masked[g])
    chosen = jnp.concatenate(chosen, axis=0)
    flat = jnp.concatenate(flat_g, axis=0)

    w_dense = scores * chosen
    w_dense = w_dense / jnp.sum(w_dense, axis=0, keepdims=True) * ROUTED_SCALE

    chosen_bf = chosen.astype(BF16)
    lower = (lax.broadcasted_iota(jnp.int32, (n_e, n_e), 1)
             < lax.broadcasted_iota(jnp.int32, (n_e, n_e), 0)).astype(BF16)
    slot_idx = jnp.dot(lower, chosen_bf, preferred_element_type=F32)
    upper = (lax.broadcasted_iota(jnp.int32, (tm, tm), 0)
             < lax.broadcasted_iota(jnp.int32, (tm, tm), 1)).astype(BF16)
    rank = jnp.dot(chosen_bf, upper, preferred_element_type=F32) + carry_ref[...]
    carry_ref[...] += jnp.sum(chosen, axis=1, keepdims=True)

    slots = []
    for j in range(TOP_K):
        ind = jnp.where(slot_idx == float(j), chosen, 0.0)
        slots.append((jnp.sum(ind * flat, axis=0, keepdims=True),
                      jnp.sum(ind * w_dense, axis=0, keepdims=True),
                      jnp.sum(ind * rank, axis=0, keepdims=True)))
    return slots


def _merge_kernel(att_ref, rf_ref, rb_ref, gate_ref, x_ref, g1_ref, sh2_ref, sc2_ref, ga_ref, gr_ref,
                  wout_ref, n2g_ref, rwt_ref, rbias_ref, x1_ref, h2p_ref, se_ref, sw_ref, sr_ref, cnt_ref,
                  carry, *, att_scale):
    @pl.when(jnp.logical_and(pl.program_id(0) == 0, pl.program_id(1) == 0))
    def _():
        carry[...] = jnp.zeros_like(carry)

    att = att_ref[0].astype(F32)
    ret = rf_ref[0].astype(F32) + rb_ref[0].astype(F32)
    gate = gate_ref[0].astype(F32)
    parts = []
    for h in range(A_HEADS):
        a = att[:, h * 128:(h + 1) * 128]
        parts.append(_rmsnorm_rows(a, ga_ref[...]) * att_scale)
    for h in range(R_HEADS):
        r = _rmsnorm_rows(ret[:, h * R_DV:(h + 1) * R_DV], gr_ref[...])
        g = gate[:, h * R_DV:(h + 1) * R_DV]
        parts.append(g * jax.nn.sigmoid(g) * r)
    merged = jnp.concatenate(parts, axis=1).astype(BF16)
    y = jnp.dot(merged, wout_ref[...], preferred_element_type=F32)
    x1 = x_ref[0] + g1_ref[0] * y
    x1_ref[0] = x1
    h2 = _rmsnorm_rows(x1, n2g_ref[...]) * (1.0 + sc2_ref[0]) + sh2_ref[0]
    _store_token_rows(h2p_ref, 0, h2.shape[0], h2)
    logits_t = lax.dot_general(rwt_ref[...], h2.astype(BF16), _NT, preferred_element_type=F32)
    slots = _route_tile(logits_t, rbias_ref[...], carry)
    for j, (e_j, w_j, r_j) in enumerate(slots):
        se_ref[j:j + 1, :] = e_j.astype(jnp.int32)
        sw_ref[j:j + 1, :] = w_j
        sr_ref[j:j + 1, :] = r_j.astype(jnp.int32)
    cnt_ref[...] = carry[...]


def _merge(att, rf, rb, px, x, g1, sh2, sc2, ga, gr, wout_bf, n2g, rwt_bf, rbias, att_scale, tm):
    b, s, d = x.shape
    nt = s // tm
    row = lambda bi, i: (bi, i, 0)
    tok = lambda bi, i: (0, bi * nt + i)
    per_b = pl.BlockSpec((1, 1, d), lambda bi, i: (bi, 0, 0))
    const2 = lambda shape: pl.BlockSpec(shape, lambda bi, i: (0, 0))
    return pl.pallas_call(
        functools.partial(_merge_kernel, att_scale=att_scale),
        grid=(b, nt),
        in_specs=[pl.BlockSpec((1, tm, 512), row), pl.BlockSpec((1, tm, 512), row),
                  pl.BlockSpec((1, tm, 512), row),
                  pl.BlockSpec((1, tm, 512), lambda bi, i: (bi, i, P_RG // 512)),
                  pl.BlockSpec((1, tm, d), row), per_b, per_b, per_b,
                  const2((1, 128)), const2((1, R_DV)), const2((d, d)), const2((1, d)),
                  const2((N_EXPERTS, d)), const2((N_EXPERTS, 1))],
        out_specs=[pl.BlockSpec((1, tm, d), row),
                   pl.BlockSpec((ROW_CHUNKS * tm, LANES), lambda bi, i: (bi * nt + i, 0)),
                   pl.BlockSpec((TOP_K, tm), tok), pl.BlockSpec((TOP_K, tm), tok),
                   pl.BlockSpec((TOP_K, tm), tok), const2((N_EXPERTS, 1))],
        out_shape=[jax.ShapeDtypeStruct((b, s, d), F32),
                   jax.ShapeDtypeStruct((b * s * ROW_CHUNKS, LANES), U32),
                   jax.ShapeDtypeStruct((TOP_K, b * s), jnp.int32),
                   jax.ShapeDtypeStruct((TOP_K, b * s), F32),
                   jax.ShapeDtypeStruct((TOP_K, b * s), jnp.int32),
                   jax.ShapeDtypeStruct((N_EXPERTS, 1), F32)],
        scratch_shapes=[pltpu.VMEM((N_EXPERTS, 1), F32)],
        compiler_params=_cparams(("arbitrary", "arbitrary")),
        name="mixer_merge",
    )(att, rf, rb, px, x, g1, sh2, sc2, ga, gr, wout_bf, n2g, rwt_bf, rbias)


def _dispatch_kernel(dest_ref, h_ref, xs_ref, buf, sem):
    rows = h_ref.shape[0]
    tm = rows // ROW_CHUNKS
    i = pl.program_id(0)
    n = pl.num_programs(0)
    slot = i % 2

    def wait_slot(s):
        for _ in range(TOP_K):
            pltpu.make_async_copy(buf.at[s], xs_ref.at[pl.ds(0, rows)], sem.at[s]).wait()

    @pl.when(i >= 2)
    def _():
        wait_slot(slot)

    buf[slot] = h_ref[...]

    def body(t, carry):
        src = buf.at[slot, pl.ds(pl.multiple_of(t * ROW_CHUNKS, ROW_CHUNKS), ROW_CHUNKS)]
        for j in range(TOP_K):
            d = pl.multiple_of(dest_ref[t * TOP_K + j], ROW_CHUNKS)
            pltpu.make_async_copy(src, xs_ref.at[pl.ds(d, ROW_CHUNKS)], sem.at[slot]).start(priority=j % 2)
        return carry

    lax.fori_loop(0, tm, body, 0)

    @pl.when(i == n - 1)
    def _():
        wait_slot(slot)

    @pl.when(jnp.logical_and(i == n - 1, n >= 2))
    def _():
        wait_slot(1 - slot)


def _dispatch(dest_rows, h2p, tm):
    n_rows = h2p.shape[0]
    n = n_rows // (ROW_CHUNKS * tm)
    return pl.pallas_call(
        _dispatch_kernel,
        grid=(n,),
        in_specs=[pl.BlockSpec((tm * TOP_K,), lambda i: (i,), memory_space=pltpu.SMEM),
                  pl.BlockSpec((ROW_CHUNKS * tm, LANES), lambda i: (i, 0))],
        out_specs=pl.BlockSpec(memory_space=pl.ANY),
        out_shape=jax.ShapeDtypeStruct((n_rows * TOP_K, LANES), U32),
        scratch_shapes=[pltpu.VMEM((2, ROW_CHUNKS * tm, LANES), U32), pltpu.SemaphoreType.DMA((2,))],
        compiler_params=_cparams(("arbitrary",)),
        name="moe_dispatch",
    )(dest_rows, h2p)


def _expert_kernel(iblk_ref, iexp_ref, ilo_ref, ihi_ref, xs_ref, w1_ref, w3_ref, w2_ref, ys_ref,
                   acc, w1b, w3b, w2b):
    r = xs_ref.shape[0] // ROW_CHUNKS
    i = pl.program_id(0)
    blk = iblk_ref[i]
    lo = ilo_ref[i]
    hi = ihi_ref[i]
    prev = jnp.maximum(i - 1, 0)
    first = jnp.logical_or(i == 0, iblk_ref[prev] != blk)

    @pl.when(jnp.logical_or(i == 0, iexp_ref[prev] != iexp_ref[i]))
    def _():
        w1b[...] = w1_ref[0].astype(BF16)
        w3b[...] = w3_ref[0].astype(BF16)
        w2b[...] = w2_ref[0].astype(BF16)

    @pl.when(i == 0)
    def _():
        acc[...] = jnp.zeros_like(acc)

    n_half = 2
    rh = r // n_half
    for hf in range(n_half):
        sl = slice(hf * rh, (hf + 1) * rh)
        rows = blk * r + hf * rh + lax.broadcasted_iota(jnp.int32, (rh, 1), 0)
        valid = jnp.logical_and(rows >= lo, rows < hi)
        x = _rows_to_features(_load_token_rows(xs_ref, hf * rh, rh))
        x = jnp.where(valid, x, 0.0).astype(BF16)
        a = jnp.dot(x, w1b[...], preferred_element_type=F32)
        bb = jnp.dot(x, w3b[...], preferred_element_type=F32)
        hmid = (a * jax.nn.sigmoid(a) * bb).astype(BF16)
        y = jnp.dot(hmid, w2b[...], preferred_element_type=F32)
        y = jnp.where(first, y, acc[sl, :] + y)
        acc[sl, :] = y
        _store_token_rows(ys_ref, hf * rh, rh, y)


def _experts(iblk, iexp, ilo, ihi, xs, w1, w3, w2, r):
    na, kw = xs.shape
    rows = r * ROW_CHUNKS
    n_items = iblk.shape[0]
    d = w1.shape[1]
    de = w1.shape[2]
    grid_spec = pltpu.PrefetchScalarGridSpec(
        num_scalar_prefetch=4,
        grid=(n_items,),
        in_specs=[pl.BlockSpec((rows, kw), lambda i, ib, ie, il, ih: (ib[i], 0)),
                  pl.BlockSpec((1, d, de), lambda i, ib, ie, il, ih: (ie[i], 0, 0)),
                  pl.BlockSpec((1, d, de), lambda i, ib, ie, il, ih: (ie[i], 0, 0)),
                  pl.BlockSpec((1, de, d), lambda i, ib, ie, il, ih: (ie[i], 0, 0))],
        out_specs=pl.BlockSpec((rows, kw), lambda i, ib, ie, il, ih: (ib[i], 0)),
        scratch_shapes=[pltpu.VMEM((r, d), F32), pltpu.VMEM((d, de), BF16), pltpu.VMEM((d, de), BF16),
                        pltpu.VMEM((de, d), BF16)],
    )
    return pl.pallas_call(
        _expert_kernel,
        grid_spec=grid_spec,
        out_shape=jax.ShapeDtypeStruct((na, kw), U32),
        compiler_params=_cparams(("arbitrary",)),
        name="moe_experts",
    )(iblk, iexp, ilo, ihi, xs, w1, w3, w2)


def _combine_kernel(dest_ref, dnext_ref, x1_ref, h2p_ref, wts_ref, g2_ref, sw1_ref, sw3_ref, sw2_ref,
                    fg_ref, ys_ref, o_ref, gbuf, sem):
    tm = x1_ref.shape[0]
    i = pl.program_id(0)
    n = pl.num_programs(0)
    slot = i % 2

    rows = ROW_CHUNKS * tm

    def issue(dref, s):
        def body(t, carry):
            off = pl.multiple_of(t * ROW_CHUNKS, ROW_CHUNKS)
            for j in range(TOP_K):
                d = pl.multiple_of(dref[t * TOP_K + j], ROW_CHUNKS)
                pltpu.make_async_copy(ys_ref.at[pl.ds(d, ROW_CHUNKS)],
                                      gbuf.at[s, j, pl.ds(off, ROW_CHUNKS)], sem.at[s]).start(priority=j % 2)
            return carry
        lax.fori_loop(0, tm, body, 0)

    @pl.when(i == 0)
    def _():
        issue(dest_ref, 0)

    @pl.when(i + 1 < n)
    def _():
        issue(dnext_ref, 1 - slot)

    for _ in range(TOP_K):
        pltpu.make_async_copy(ys_ref.at[pl.ds(0, rows)], gbuf.at[slot, 0], sem.at[slot]).wait()

    wts = wts_ref[...]
    moe = None
    for j in range(TOP_K):
        wj = wts[:, j:j + 1]
        yj = [(lo * wj, hi * wj) for lo, hi in _load_token_rows(gbuf.at[slot, j], 0, tm)]
        moe = yj if moe is None else [(a + lo, b + hi) for (a, b), (lo, hi) in zip(moe, yj)]
    h = _rows_to_features(_load_token_rows(h2p_ref, 0, tm)).astype(BF16)
    a = jnp.dot(h, sw1_ref[...], preferred_element_type=F32)
    bb = jnp.dot(h, sw3_ref[...], preferred_element_type=F32)
    shared = jnp.dot((a * jax.nn.sigmoid(a) * bb).astype(BF16), sw2_ref[...], preferred_element_type=F32)
    ffn = _rows_to_features(moe) + shared
    xo = x1_ref[...] + g2_ref[0] * ffn
    o_ref[...] = _rmsnorm_rows(xo, fg_ref[...])


def _combine(dest_rows, x1, h2p, wts, g2, sw1, sw3, sw2, fg, ys, tiles_per_batch, tm):
    t, d = x1.shape
    n = t // tm
    de = sw1.shape[1]
    const2 = lambda shape: pl.BlockSpec(shape, lambda i: (0, 0))
    return pl.pallas_call(
        _combine_kernel,
        grid=(n,),
        in_specs=[pl.BlockSpec((tm * TOP_K,), lambda i: (i,), memory_space=pltpu.SMEM),
                  pl.BlockSpec((tm * TOP_K,), lambda i: (jnp.minimum(i + 1, n - 1),),
                               memory_space=pltpu.SMEM),
                  pl.BlockSpec((tm, d), lambda i: (i, 0)),
                  pl.BlockSpec((ROW_CHUNKS * tm, LANES), lambda i: (i, 0)),
                  pl.BlockSpec((tm, TOP_K), lambda i: (i, 0)),
                  pl.BlockSpec((1, 1, d), lambda i: (i // tiles_per_batch, 0, 0)),
                  const2((d, de)), const2((d, de)), const2((de, d)), const2((1, d)),
                  pl.BlockSpec(memory_space=pl.ANY)],
        out_specs=pl.BlockSpec((tm, d), lambda i: (i, 0)),
        out_shape=jax.ShapeDtypeStruct((t, d), F32),
        scratch_shapes=[pltpu.VMEM((2, TOP_K, ROW_CHUNKS * tm, LANES), U32),
                        pltpu.SemaphoreType.DMA((2,))],
        compiler_params=_cparams(("arbitrary",)),
        name="moe_combine",
    )(dest_rows, dest_rows, x1, h2p, wts, g2, sw1, sw3, sw2, fg, ys)


def _route_meta(counts, slot_e, slot_r, block_rows):
    t = slot_e.shape[1]
    ends = jnp.cumsum(counts)
    starts = ends - counts
    onehot = slot_e[:, :, None] == jnp.arange(N_EXPERTS, dtype=jnp.int32)
    dest_t = slot_r + jnp.sum(jnp.where(onehot, starts, 0), axis=-1).astype(jnp.int32)

    n_blk = t * TOP_K // block_rows
    n_items = n_blk + N_EXPERTS - 1
    first_blk = starts // block_rows
    last_blk = (ends - 1) // block_rows
    n_e = jnp.where(counts > 0, last_blk - first_blk + 1, 0)
    item_end = jnp.cumsum(n_e)
    item_off = item_end - n_e
    total = item_end[-1]
    idx = jnp.arange(n_items, dtype=jnp.int32)
    e_of = jnp.minimum(jnp.sum(item_end[None, :] <= idx[:, None], axis=1), N_EXPERTS - 1).astype(jnp.int32)
    live = idx < total
    blk = first_blk[e_of] + (idx - item_off[e_of])
    lo = jnp.maximum(starts[e_of], blk * block_rows)
    hi = jnp.minimum(ends[e_of], (blk + 1) * block_rows)
    last_e = e_of[jnp.maximum(total - 1, 0)]
    iblk = jnp.where(live, blk, n_blk - 1).astype(jnp.int32)
    iexp = jnp.where(live, e_of, last_e).astype(jnp.int32)
    ilo = jnp.where(live, lo, 0).astype(jnp.int32)
    ihi = jnp.where(live, hi, 0).astype(jnp.int32)
    return dest_t, iblk, iexp, ilo, ihi


def _rope_tables(n_tokens):
    n_rows = n_tokens // GRID_W
    rows = jnp.repeat(jnp.arange(n_rows, dtype=F32), GRID_W)
    cols = jnp.tile(jnp.arange(GRID_W, dtype=F32), n_rows)
    n_freq = HEAD_DIM // 4
    inv_freq = ROPE_BASE ** (-jnp.arange(n_freq, dtype=F32) / n_freq)
    ar = rows[:, None] * inv_freq
    ac = cols[:, None] * inv_freq
    cos64 = jnp.concatenate([jnp.cos(ar), jnp.cos(ar), jnp.cos(ac), jnp.cos(ac)], axis=1)
    sin64 = jnp.concatenate([-jnp.sin(ar), jnp.sin(ar), -jnp.sin(ac), jnp.sin(ac)], axis=1)
    return jnp.tile(cos64, (1, 2)), jnp.tile(sin64, (1, 2))


def kernel(x, c, ctx, c_ctx, ada_w, ada_b, norm1_g, w_in, lambda_q1, lambda_k1, lambda_q2, lambda_k2,
           dattn_norm_g, ret_decay_fwd, ret_decay_bwd, ret_norm_g, w_out, norm2_g, router_w, router_bias,
           exp_w1, exp_w3, exp_w2, shared_w1, shared_w3, shared_w2, final_norm_g):
    b, s, d = x.shape
    lc = ctx.shape[1]
    depth = ada_w.shape[0]
    assert depth == 1, "single-layer problem: the context stream is never updated"
    l = 0
    lam_init = 0.8 - 0.6 * math.exp(-0.3 * l)

    c8 = jnp.zeros((8, d), F32).at[:b].set(c).at[b].set(c_ctx)
    mod = _modulation(c8, ada_w[l], ada_b[l])
    sh1, sc1, g1, sh2, sc2, g2 = [mod[:b, k * d:(k + 1) * d].reshape(b, 1, d) for k in range(6)]
    csh1 = jnp.broadcast_to(mod[b, 0:d].reshape(1, 1, d), (b, 1, d))
    csc1 = jnp.broadcast_to(mod[b, d:2 * d].reshape(1, 1, d), (b, 1, d))

    w_l = w_in[l]
    w_tok = jnp.concatenate([w_l[:, 512:1024], w_l[:, 1536:3072]], axis=1).astype(BF16)
    w_feat = jnp.concatenate([w_l[:, 0:512], w_l[:, 1024:1536]], axis=1).T.astype(BF16)
    n1g = norm1_g[l].reshape(1, d)
    cos, sin = _rope_tables(s)
    ones_c = jnp.ones((lc, LANES), F32)
    zeros_c = jnp.zeros((lc, LANES), F32)
    px, qt, vtx = _inproj(x, n1g, sc1, sh1, cos, sin, cos.T, sin.T, w_tok, w_feat, tm=512)
    pc, _, vtc = _inproj(ctx, n1g, csc1, csh1, ones_c, zeros_c, ones_c.T, zeros_c.T, w_tok, w_feat, tm=lc)

    lam = (jnp.exp(jnp.sum(lambda_q1[l] * lambda_k1[l])) - jnp.exp(jnp.sum(lambda_q2[l] * lambda_k2[l]))
           ).astype(F32) + lam_init
    att = _attention(lam.reshape(1), qt, px, vtx, pc, vtc, tq=min(2048, s), tkc=min(512, s // 4))

    dmat, qdec, kdec, gch = _retention_tables(ret_decay_fwd[l], ret_decay_bwd[l])
    ret_f, ret_b = _retention(px, pc, dmat, qdec, kdec, gch)

    x1, h2p, slot_e, slot_w, slot_r, counts = _merge(
        att, ret_f, ret_b, px, x, g1, sh2, sc2,
        dattn_norm_g[l].reshape(1, 2 * HEAD_DIM), ret_norm_g[l].reshape(1, R_DV),
        w_out[l].astype(BF16), norm2_g[l].reshape(1, d), router_w[l].T.astype(BF16),
        router_bias[l].astype(F32).reshape(N_EXPERTS, 1), 1.0 - lam_init, tm=512)

    t = b * s
    block_rows = 512
    dest_t, iblk, iexp, ilo, ihi = _route_meta(counts[:, 0].astype(jnp.int32), slot_e, slot_r, block_rows)
    dest_rows = (dest_t.T * ROW_CHUNKS).reshape(t * TOP_K)
    xs = _dispatch(dest_rows, h2p, tm=128)
    ys = _experts(iblk, iexp, ilo, ihi, xs, exp_w1[l], exp_w3[l], exp_w2[l], block_rows)
    tm_c = 128
    out = _combine(dest_rows, x1.reshape(t, d), h2p, slot_w.T, g2, shared_w1[l].astype(BF16),
                   shared_w3[l].astype(BF16), shared_w2[l].astype(BF16), final_norm_g.reshape(1, d),
                   ys, s // tm_c, tm_c)
    return out.reshape(b, s, d)
```

```python
import functools
import math

import jax
import jax.numpy as jnp
import numpy as np
from jax import lax
from jax.experimental import pallas as pl
from jax.experimental.pallas import tpu as pltpu

F32 = jnp.float32
BF16 = jnp.bfloat16
U32 = jnp.uint32

D_MODEL = 1024
GRID_W = 64
HEAD_DIM = 64
A_HEADS = 4
R_HEADS = 4
R_DK = 64
R_DV = 128
RET_CHUNK = 128
ROPE_BASE = 10000.0
N_EXPERTS = 64
N_GROUPS = 8
TOPK_GROUPS = 4
TOP_K = 8
D_EXPERT = 256
ROUTED_SCALE = 2.5
NORM_EPS = 1e-6
LANES = 128

P_KZ = 0
P_RQ = 1024
P_RK = 1280
P_RV = 1536
P_RG = 2048
P_COLS = 2560
VT_ROWS = 144
LOG2E = 1.4426950408889634

VMEM_LIMIT = 56 * 1024 * 1024


def _cparams(sem):
    return pltpu.CompilerParams(dimension_semantics=sem, vmem_limit_bytes=VMEM_LIMIT)


def _mod_kernel(c_ref, w_ref, b_ref, o_ref):
    c = c_ref[...]
    s = c * jax.nn.sigmoid(c)
    o_ref[...] = jnp.dot(s, w_ref[...], preferred_element_type=F32,
                         precision=lax.Precision.HIGHEST) + b_ref[...]


def _modulation(c8, ada_w, ada_b):
    d = c8.shape[1]
    n = ada_w.shape[1]
    tn = 512
    return pl.pallas_call(
        _mod_kernel,
        grid=(n // tn,),
        in_specs=[pl.BlockSpec((8, d), lambda j: (0, 0)),
                  pl.BlockSpec((d, tn), lambda j: (0, j)),
                  pl.BlockSpec((1, tn), lambda j: (0, j))],
        out_specs=pl.BlockSpec((8, tn), lambda j: (0, j)),
        out_shape=jax.ShapeDtypeStruct((8, n), F32),
        compiler_params=_cparams(("arbitrary",)),
        name="modulation",
    )(c8, ada_w, ada_b.reshape(1, n))


def _rmsnorm_rows(x, g):
    return x * lax.rsqrt(jnp.mean(x * x, axis=-1, keepdims=True) + NORM_EPS) * g


def _inproj_kernel(x_ref, g_ref, sc_ref, sh_ref, cos_ref, sin_ref, cost_ref, sint_ref, w_ref, wt_ref,
                   p_ref, qt_ref, vt_ref):
    x = x_ref[0]
    tm = x.shape[0]
    h = _rmsnorm_rows(x, g_ref[...]) * (1.0 + sc_ref[0]) + sh_ref[0]
    hb = h.astype(BF16)
    p = jnp.dot(hb, w_ref[...], preferred_element_type=F32)
    pt = lax.dot_general(wt_ref[...], hb, _NT, preferred_element_type=F32)

    cos = cos_ref[...]
    sin = sin_ref[...]
    lane = lax.broadcasted_iota(jnp.int32, (tm, LANES), 1)
    first_half = (lane % 32) < 16
    low64 = lane < 64

    def rope(xs):
        partner = jnp.where(first_half, pltpu.roll(xs, LANES - 16, 1), pltpu.roll(xs, 16, 1))
        return xs * cos + partner * sin

    zero = jnp.zeros((tm, LANES), BF16)
    for hh in range(A_HEADS):
        k = rope(p[:, hh * 128:(hh + 1) * 128]).astype(BF16)
        p_ref[0, :, P_KZ + hh * 256:P_KZ + hh * 256 + 128] = jnp.where(low64, k, zero)
        p_ref[0, :, P_KZ + hh * 256 + 128:P_KZ + (hh + 1) * 256] = jnp.where(low64, zero, k)
    for cc in range(2):
        rq = rope(p[:, 512 + cc * 128:512 + (cc + 1) * 128])
        p_ref[0, :, P_RQ + cc * 128:P_RQ + (cc + 1) * 128] = rq.astype(BF16)
        rk = rope(p[:, 768 + cc * 128:768 + (cc + 1) * 128]) * (R_DK ** -0.5)
        p_ref[0, :, P_RK + cc * 128:P_RK + (cc + 1) * 128] = rk.astype(BF16)
    p_ref[0, :, P_RV:P_RV + 512] = p[:, 1024:1536].astype(BF16)
    p_ref[0, :, P_RG:P_RG + 512] = p[:, 1536:2048].astype(BF16)

    cost = cost_ref[...]
    sint = sint_ref[...]
    q_scale = (HEAD_DIM ** -0.5) * LOG2E
    ones_rows = jnp.where(lax.broadcasted_iota(jnp.int32, (VT_ROWS - 128, tm), 0) == 0, 1.0, 0.0)
    for hh in range(A_HEADS):
        xq = pt[hh * 128:(hh + 1) * 128, :]
        groups = [xq[16 * gi:16 * (gi + 1), :] for gi in range(8)]
        partner = jnp.concatenate([groups[gi ^ 1] for gi in range(8)], axis=0)
        qt_ref[0, hh] = ((xq * cost + partner * sint) * q_scale).astype(BF16)
        vt_ref[0, hh, 0:128, :] = pt[512 + hh * 128:512 + (hh + 1) * 128, :].astype(BF16)
        vt_ref[0, hh, 128:VT_ROWS, :] = ones_rows.astype(BF16)


def _inproj(x, g, sc, sh, cos, sin, cost, sint, w_bf, wt_bf, tm):
    b, s, d = x.shape
    n_tok = w_bf.shape[1]
    n_t = wt_bf.shape[0]
    const2 = lambda shape: pl.BlockSpec(shape, lambda bi, i: (0, 0))
    per_b = pl.BlockSpec((1, 1, d), lambda bi, i: (bi, 0, 0))
    return pl.pallas_call(
        _inproj_kernel,
        grid=(b, s // tm),
        in_specs=[pl.BlockSpec((1, tm, d), lambda bi, i: (bi, i, 0)),
                  const2((1, d)), per_b, per_b,
                  pl.BlockSpec((tm, LANES), lambda bi, i: (i, 0)),
                  pl.BlockSpec((tm, LANES), lambda bi, i: (i, 0)),
                  pl.BlockSpec((LANES, tm), lambda bi, i: (0, i)),
                  pl.BlockSpec((LANES, tm), lambda bi, i: (0, i)),
                  const2((d, n_tok)), const2((n_t, d))],
        out_specs=[pl.BlockSpec((1, tm, P_COLS), lambda bi, i: (bi, i, 0)),
                   pl.BlockSpec((1, A_HEADS, LANES, tm), lambda bi, i: (bi, 0, 0, i)),
                   pl.BlockSpec((1, A_HEADS, VT_ROWS, tm), lambda bi, i: (bi, 0, 0, i))],
        out_shape=[jax.ShapeDtypeStruct((b, s, P_COLS), BF16),
                   jax.ShapeDtypeStruct((b, A_HEADS, LANES, s), BF16),
                   jax.ShapeDtypeStruct((b, A_HEADS, VT_ROWS, s), BF16)],
        compiler_params=_cparams(("arbitrary", "arbitrary")),
        name="inproj",
    )(x, g, sc, sh, cos, sin, cost, sint, w_bf, wt_bf)


_NT = (((1,), (1,)), ((), ()))


def _attn_kernel(lam_ref, qt_ref, kzc_ref, vtc_ref, kzx_ref, vtx_ref, o_ref,
                 acc, s_a, s_b, e_a, e_b, *, tkc):
    qt = qt_ref[0, 0]
    tq = qt.shape[1]
    n_chunks = kzx_ref.shape[1] // tkc
    assert n_chunks % 2 == 0 and n_chunks >= 4

    def scores(kz, s_ref):
        for mp in range(2):
            s_ref[mp] = jnp.dot(kz[:, mp * LANES:(mp + 1) * LANES], qt, preferred_element_type=F32)

    def softmax(s_ref, e_ref, ms):
        new_ms, alphas = [], []
        for mp in range(2):
            st = s_ref[mp]
            m_new = jnp.maximum(ms[mp], jnp.max(st, axis=0, keepdims=True))
            alphas.append(jnp.exp2(ms[mp] - m_new))
            e_ref[mp] = jnp.exp2(st - m_new).astype(BF16)
            new_ms.append(m_new)
        return tuple(new_ms), tuple(alphas)

    def values(vt, e_ref, alphas):
        for mp in range(2):
            acc[mp] = alphas[mp] * acc[mp] + jnp.dot(vt, e_ref[mp], preferred_element_type=F32)

    def kchunk(c):
        return kzx_ref[0, pl.ds(pl.multiple_of(c * tkc, tkc), tkc), :]

    def vchunk(c):
        return vtx_ref[0, 0, :, pl.ds(pl.multiple_of(c * tkc, tkc), tkc)]

    ms = []
    for mp in range(2):
        st = jnp.dot(kzc_ref[0, :, mp * LANES:(mp + 1) * LANES], qt, preferred_element_type=F32)
        m_new = jnp.max(st, axis=0, keepdims=True)
        acc[mp] = jnp.dot(vtc_ref[0, 0], jnp.exp2(st - m_new).astype(BF16), preferred_element_type=F32)
        ms.append(m_new)
    ms = tuple(ms)

    scores(kchunk(0), s_a)
    scores(kchunk(1), s_b)
    ms, al_a = softmax(s_a, e_a, ms)

    def body(j, carry):
        ms, al_a = carry
        scores(kchunk(2 * j + 2), s_a)
        ms, al_b = softmax(s_b, e_b, ms)
        values(vchunk(2 * j), e_a, al_a)
        scores(kchunk(2 * j + 3), s_b)
        ms, al_a = softmax(s_a, e_a, ms)
        values(vchunk(2 * j + 1), e_b, al_b)
        return ms, al_a

    ms, al_a = lax.fori_loop(0, n_chunks // 2 - 1, body, (ms, al_a))
    ms, al_b = softmax(s_b, e_b, ms)
    values(vchunk(n_chunks - 2), e_a, al_a)
    values(vchunk(n_chunks - 1), e_b, al_b)

    a1 = acc[0]
    a2 = acc[1]
    o1 = a1[:LANES, :] / a1[LANES:LANES + 1, :]
    o2 = a2[:LANES, :] / a2[LANES:LANES + 1, :]
    o_ref[0] = (o1 - lam_ref[0] * o2).T.astype(BF16)


def _attention(lam, qt, px, vtx, pc, vtc, tq, tkc):
    b, s, _ = px.shape
    lc = pc.shape[1]
    kern = functools.partial(_attn_kernel, tkc=tkc)
    return pl.pallas_call(
        kern,
        grid=(b, A_HEADS, s // tq),
        in_specs=[pl.BlockSpec(memory_space=pltpu.SMEM),
                  pl.BlockSpec((1, 1, LANES, tq), lambda bi, h, i: (bi, h, 0, i)),
                  pl.BlockSpec((1, lc, 256), lambda bi, h, i: (bi, 0, P_KZ // 256 + h)),
                  pl.BlockSpec((1, 1, VT_ROWS, lc), lambda bi, h, i: (bi, h, 0, 0)),
                  pl.BlockSpec((1, s, 256), lambda bi, h, i: (bi, 0, P_KZ // 256 + h)),
                  pl.BlockSpec((1, 1, VT_ROWS, s), lambda bi, h, i: (bi, h, 0, 0))],
        out_specs=pl.BlockSpec((1, tq, LANES), lambda bi, h, i: (bi, i, h)),
        out_shape=jax.ShapeDtypeStruct((b, s, A_HEADS * LANES), BF16),
        scratch_shapes=[pltpu.VMEM((2, VT_ROWS, tq), F32),
                        pltpu.VMEM((2, tkc, tq), F32), pltpu.VMEM((2, tkc, tq), F32),
                        pltpu.VMEM((2, tkc, tq), BF16), pltpu.VMEM((2, tkc, tq), BF16)],
        compiler_params=_cparams(("arbitrary", "arbitrary", "arbitrary")),
        name="diff_attention",
    )(lam, qt, pc, vtc, px, vtx)


_TN = (((0,), (0,)), ((), ()))


def _ret_kernel(qf_ref, kf_ref, vf_ref, qb_ref, kb_ref, vb_ref, kc_ref, vc_ref,
                dmat_ref, qdec_ref, kdec_ref, gch_ref, of_ref, ob_ref, sf, sb):
    c = RET_CHUNK
    n = pl.program_id(1)
    col_head = lax.broadcasted_iota(jnp.int32, (c, R_HEADS * R_DK), 1) // R_DK
    bd_mask = (lax.broadcasted_iota(jnp.int32, (R_HEADS * R_DK, R_HEADS * R_DV), 0) // R_DK
               == lax.broadcasted_iota(jnp.int32, (R_HEADS * R_DK, R_HEADS * R_DV), 1) // R_DV)

    def state_update(s_ref, k, v, kdec, gch):
        kd = (k.astype(F32) * kdec).astype(BF16)
        kv = lax.dot_general(kd, v, _TN, preferred_element_type=F32)
        s_ref[...] = gch * s_ref[...] + jnp.where(bd_mask, kv, 0.0)

    def chunk(s_ref, q, k, v, dmat, qdec, kdec, gch):
        zero = jnp.zeros_like(k)
        kblk = jnp.concatenate([jnp.where(col_head == h, k, zero) for h in range(R_HEADS)], axis=0)
        sc = lax.dot_general(q, kblk, _NT, preferred_element_type=F32) * dmat
        p = sc.astype(BF16)
        intra = jnp.concatenate(
            [jnp.dot(p[:, h * c:(h + 1) * c], v[:, h * R_DV:(h + 1) * R_DV], preferred_element_type=F32)
             for h in range(R_HEADS)], axis=1)
        qd = (q.astype(F32) * qdec).astype(BF16)
        cross = jnp.dot(qd, s_ref[...].astype(BF16), preferred_element_type=F32)
        out = intra + cross
        state_update(s_ref, k, v, kdec, gch)
        return out

    @pl.when(n == 0)
    def _():
        sf[...] = jnp.zeros_like(sf)
        sb[...] = jnp.zeros_like(sb)
        n_ctx = kc_ref.shape[1] // c
        for cc in range(n_ctx):
            state_update(sf, kc_ref[0, cc * c:(cc + 1) * c, :], vc_ref[0, cc * c:(cc + 1) * c, :],
                         kdec_ref[0], gch_ref[0])
        for cc in reversed(range(n_ctx)):
            state_update(sb, kc_ref[0, cc * c:(cc + 1) * c, :], vc_ref[0, cc * c:(cc + 1) * c, :],
                         kdec_ref[1], gch_ref[1])

    of_ref[0] = chunk(sf, qf_ref[0], kf_ref[0], vf_ref[0], dmat_ref[0], qdec_ref[0], kdec_ref[0],
                      gch_ref[0]).astype(BF16)
    ob_ref[0] = chunk(sb, qb_ref[0], kb_ref[0], vb_ref[0], dmat_ref[1], qdec_ref[1], kdec_ref[1],
                      gch_ref[1]).astype(BF16)


def _retention(px, pc, dmat, qdec, kdec, gch):
    b, s, _ = px.shape
    lc = pc.shape[1]
    c = RET_CHUNK
    nch = s // c
    qk_w = R_HEADS * R_DK
    v_w = R_HEADS * R_DV
    fwd = lambda bi, n: (bi, n, 0)
    spec_q = lambda im: pl.BlockSpec((1, c, qk_w), lambda bi, n: im(bi, n)[:2] + (P_RQ // qk_w,))
    spec_k = lambda im: pl.BlockSpec((1, c, qk_w), lambda bi, n: im(bi, n)[:2] + (P_RK // qk_w,))
    spec_v = lambda im: pl.BlockSpec((1, c, v_w), lambda bi, n: im(bi, n)[:2] + (P_RV // v_w,))
    bwd = lambda bi, n: (bi, nch - 1 - n, 0)
    const3 = lambda shape: pl.BlockSpec(shape, lambda bi, n: (0, 0, 0))
    return pl.pallas_call(
        _ret_kernel,
        grid=(b, nch),
        in_specs=[spec_q(fwd), spec_k(fwd), spec_v(fwd), spec_q(bwd), spec_k(bwd), spec_v(bwd),
                  pl.BlockSpec((1, lc, qk_w), lambda bi, n: (bi, 0, P_RK // qk_w)),
                  pl.BlockSpec((1, lc, v_w), lambda bi, n: (bi, 0, P_RV // v_w)),
                  const3((2, c, R_HEADS * c)), const3((2, c, qk_w)), const3((2, c, qk_w)),
                  const3((2, 1, v_w))],
        out_specs=[pl.BlockSpec((1, c, v_w), fwd), pl.BlockSpec((1, c, v_w), bwd)],
        out_shape=[jax.ShapeDtypeStruct((b, s, v_w), BF16), jax.ShapeDtypeStruct((b, s, v_w), BF16)],
        scratch_shapes=[pltpu.VMEM((qk_w, v_w), F32), pltpu.VMEM((qk_w, v_w), F32)],
        compiler_params=_cparams(("arbitrary", "arbitrary")),
        name="retention",
    )(px, px, px, px, px, px, pc, pc, dmat, qdec, kdec, gch)


def _retention_tables(decay_fwd, decay_bwd):
    c = RET_CHUNK
    pos = jnp.arange(c, dtype=F32)
    rel = pos[:, None] - pos[None, :]

    def tables(logits, backward):
        lg = jax.nn.log_sigmoid(logits.astype(F32))
        r = -rel if backward else rel
        dm = jnp.where(r[None] >= 0, jnp.exp(jnp.maximum(r, 0.0)[None] * lg[:, None, None]), 0.0)
        dm = jnp.transpose(dm, (1, 0, 2)).reshape(c, R_HEADS * c)
        qpow = (c - pos) if backward else (pos + 1)
        kpow = pos if backward else (c - 1 - pos)
        qd = jnp.repeat(jnp.exp(qpow[:, None] * lg[None, :]), R_DK, axis=1)
        kd = jnp.repeat(jnp.exp(kpow[:, None] * lg[None, :]), R_DK, axis=1)
        gc = jnp.repeat(jnp.exp(c * lg), R_DV)[None, :]
        return dm, qd, kd, gc

    tf = tables(decay_fwd, False)
    tb = tables(decay_bwd, True)
    return tuple(jnp.stack([a, bb]) for a, bb in zip(tf, tb))


def _pack_pairs(x):
    k = x.shape[1] // 2
    lo = lax.bitcast_convert_type(x[:, :k].astype(BF16).astype(F32), U32)
    hi = lax.bitcast_convert_type(x[:, k:].astype(BF16).astype(F32), U32)
    return (lo >> 16) | (hi & jnp.uint32(0xFFFF0000))


def _unpack_pairs(w):
    lo = lax.bitcast_convert_type(w << 16, F32)
    hi = lax.bitcast_convert_type(w & jnp.uint32(0xFFFF0000), F32)
    return lo, hi


ROW_CHUNKS = (D_MODEL // 2) // LANES


def _store_token_rows(ref, first_token, n_tokens, x):
    packed = _pack_pairs(x)
    for c in range(ROW_CHUNKS):
        ref[pl.ds(first_token * ROW_CHUNKS + c, n_tokens, stride=ROW_CHUNKS), :] = (
            packed[:, c * LANES:(c + 1) * LANES])


def _load_token_rows(ref, first_token, n_tokens):
    return [_unpack_pairs(ref[pl.ds(first_token * ROW_CHUNKS + c, n_tokens, stride=ROW_CHUNKS), :])
            for c in range(ROW_CHUNKS)]


def _rows_to_features(chunks):
    return jnp.concatenate([lo for lo, _ in chunks] + [hi for _, hi in chunks], axis=1)


def _route_tile(logits_t, bias_col, carry_ref):
    n_e, tm = logits_t.shape
    per = n_e // N_GROUPS
    neg = -jnp.inf
    scores = jax.nn.sigmoid(logits_t)
    sel = scores + bias_col
    sub = lax.broadcasted_iota(jnp.int32, (per, tm), 0).astype(F32)
    sel_g = [sel[g * per:(g + 1) * per, :] for g in range(N_GROUPS)]
    flat_g = [sub + float(g * per) for g in range(N_GROUPS)]

    gs = []
    for g in range(N_GROUPS):
        top1 = jnp.max(sel_g[g], axis=0, keepdims=True)
        idx1 = jnp.min(jnp.where(sel_g[g] == top1, sub, float(per)), axis=0, keepdims=True)
        top2 = jnp.max(jnp.where(sub == idx1, neg, sel_g[g]), axis=0, keepdims=True)
        gs.append(top1 + top2)
    gsel = [jnp.zeros((1, tm), F32) for _ in range(N_GROUPS)]
    for _ in range(TOPK_GROUPS):
        m = functools.reduce(jnp.maximum, gs)
        found = jnp.zeros((1, tm), F32)
        for g in range(N_GROUPS):
            hit = jnp.logical_and(gs[g] == m, found == 0.0)
            found = jnp.where(hit, 1.0, found)
            gsel[g] = jnp.where(hit, 1.0, gsel[g])
            gs[g] = jnp.where(hit, neg, gs[g])

    masked = [jnp.where(gsel[g] > 0.0, sel_g[g], neg) for g in range(N_GROUPS)]
    chosen = [jnp.zeros((per, tm), F32) for _ in range(N_GROUPS)]
    for _ in range(TOP_K):
        m = functools.reduce(jnp.maximum, [jnp.max(mg, axis=0, keepdims=True) for mg in masked])
        idx = functools.reduce(jnp.minimum, [
            jnp.min(jnp.where(masked[g] == m, flat_g[g], float(n_e)), axis=0, keepdims=True)
            for g in range(N_GROUPS)])
        for g in range(N_GROUPS):
            hit = flat_g[g] == idx
            chosen[g] = jnp.where(hit, 1.0, chosen[g])
            masked[g] = jnp.where(hit, neg, masked[g])
    chosen = jnp.concatenate(chosen, axis=0)
    flat = jnp.concatenate(flat_g, axis=0)

    w_dense = scores * chosen
    w_dense = w_dense / jnp.sum(w_dense, axis=0, keepdims=True) * ROUTED_SCALE

    chosen_bf = chosen.astype(BF16)
    lower = (lax.broadcasted_iota(jnp.int32, (n_e, n_e), 1)
             < lax.broadcasted_iota(jnp.int32, (n_e, n_e), 0)).astype(BF16)
    slot_idx = jnp.dot(lower, chosen_bf, preferred_element_type=F32)
    upper = (lax.broadcasted_iota(jnp.int32, (tm, tm), 0)
             < lax.broadcasted_iota(jnp.int32, (tm, tm), 1)).astype(BF16)
    rank = jnp.dot(chosen_bf, upper, preferred_element_type=F32) + carry_ref[...]
    carry_ref[...] += jnp.sum(chosen, axis=1, keepdims=True)

    slots = []
    for j in range(TOP_K):
        ind = jnp.where(slot_idx == float(j), chosen, 0.0)
        slots.append((jnp.sum(ind * flat, axis=0, keepdims=True),
                      jnp.sum(ind * w_dense, axis=0, keepdims=True),
                      jnp.sum(ind * rank, axis=0, keepdims=True)))
    return slots


def _merge_kernel(att_ref, rf_ref, rb_ref, gate_ref, x_ref, g1_ref, sh2_ref, sc2_ref, ga_ref, gr_ref,
                  wout_ref, n2g_ref, rwt_ref, rbias_ref, x1_ref, h2p_ref, se_ref, sw_ref, sr_ref, cnt_ref,
                  carry, *, att_scale):
    @pl.when(jnp.logical_and(pl.program_id(0) == 0, pl.program_id(1) == 0))
    def _():
        carry[...] = jnp.zeros_like(carry)

    att = att_ref[0].astype(F32)
    ret = rf_ref[0].astype(F32) + rb_ref[0].astype(F32)
    gate = gate_ref[0].astype(F32)
    parts = []
    for h in range(A_HEADS):
        a = att[:, h * 128:(h + 1) * 128]
        parts.append(_rmsnorm_rows(a, ga_ref[...]) * att_scale)
    for h in range(R_HEADS):
        r = _rmsnorm_rows(ret[:, h * R_DV:(h + 1) * R_DV], gr_ref[...])
        g = gate[:, h * R_DV:(h + 1) * R_DV]
        parts.append(g * jax.nn.sigmoid(g) * r)
    merged = jnp.concatenate(parts, axis=1).astype(BF16)
    y = jnp.dot(merged, wout_ref[...], preferred_element_type=F32)
    x1 = x_ref[0] + g1_ref[0] * y
    x1_ref[0] = x1
    h2 = _rmsnorm_rows(x1, n2g_ref[...]) * (1.0 + sc2_ref[0]) + sh2_ref[0]
    _store_token_rows(h2p_ref, 0, h2.shape[0], h2)
    logits_t = lax.dot_general(rwt_ref[...], h2.astype(BF16), _NT, preferred_element_type=F32)
    slots = _route_tile(logits_t, rbias_ref[...], carry)
    for j, (e_j, w_j, r_j) in enumerate(slots):
        se_ref[j:j + 1, :] = e_j.astype(jnp.int32)
        sw_ref[j:j + 1, :] = w_j
        sr_ref[j:j + 1, :] = r_j.astype(jnp.int32)
    cnt_ref[...] = carry[...]


def _merge(att, rf, rb, px, x, g1, sh2, sc2, ga, gr, wout_bf, n2g, rwt_bf, rbias, att_scale, tm):
    b, s, d = x.shape
    nt = s // tm
    chunked = pl.BlockSpec((ROW_CHUNKS * tm, LANES), lambda bi, i: (bi * nt + i, 0))
    chunked_shape = jax.ShapeDtypeStruct((b * s * ROW_CHUNKS, LANES), U32)
    row = lambda bi, i: (bi, i, 0)
    tok = lambda bi, i: (0, bi * nt + i)
    per_b = pl.BlockSpec((1, 1, d), lambda bi, i: (bi, 0, 0))
    const2 = lambda shape: pl.BlockSpec(shape, lambda bi, i: (0, 0))
    return pl.pallas_call(
        functools.partial(_merge_kernel, att_scale=att_scale),
        grid=(b, nt),
        in_specs=[pl.BlockSpec((1, tm, 512), row), pl.BlockSpec((1, tm, 512), row),
                  pl.BlockSpec((1, tm, 512), row),
                  pl.BlockSpec((1, tm, 512), lambda bi, i: (bi, i, P_RG // 512)),
                  pl.BlockSpec((1, tm, d), row), per_b, per_b, per_b,
                  const2((1, 128)), const2((1, R_DV)), const2((d, d)), const2((1, d)),
                  const2((N_EXPERTS, d)), const2((N_EXPERTS, 1))],
        out_specs=[pl.BlockSpec((1, tm, d), row), chunked,
                   pl.BlockSpec((TOP_K, tm), tok), pl.BlockSpec((TOP_K, tm), tok),
                   pl.BlockSpec((TOP_K, tm), tok), const2((N_EXPERTS, 1))],
        out_shape=[jax.ShapeDtypeStruct((b, s, d), F32), chunked_shape,
                   jax.ShapeDtypeStruct((TOP_K, b * s), jnp.int32),
                   jax.ShapeDtypeStruct((TOP_K, b * s), F32),
                   jax.ShapeDtypeStruct((TOP_K, b * s), jnp.int32),
                   jax.ShapeDtypeStruct((N_EXPERTS, 1), F32)],
        scratch_shapes=[pltpu.VMEM((N_EXPERTS, 1), F32)],
        compiler_params=_cparams(("arbitrary", "arbitrary")),
        name="mixer_merge",
    )(att, rf, rb, px, x, g1, sh2, sc2, ga, gr, wout_bf, n2g, rwt_bf, rbias)


def _dispatch_kernel(dest_ref, h_ref, xs_ref, buf, sem):
    rows = h_ref.shape[0]
    tm = rows // ROW_CHUNKS
    i = pl.program_id(0)
    n = pl.num_programs(0)
    slot = i % 2

    def wait_slot(s):
        for _ in range(TOP_K):
            pltpu.make_async_copy(buf.at[s], xs_ref.at[pl.ds(0, rows)], sem.at[s]).wait()

    @pl.when(i >= 2)
    def _():
        wait_slot(slot)

    buf[slot] = h_ref[...]

    def body(t, carry):
        src = buf.at[slot, pl.ds(pl.multiple_of(t * ROW_CHUNKS, ROW_CHUNKS), ROW_CHUNKS)]
        for j in range(TOP_K):
            d = pl.multiple_of(dest_ref[t * TOP_K + j], ROW_CHUNKS)
            pltpu.make_async_copy(src, xs_ref.at[pl.ds(d, ROW_CHUNKS)], sem.at[slot]).start(priority=j % 2)
        return carry

    lax.fori_loop(0, tm, body, 0)

    @pl.when(i == n - 1)
    def _():
        wait_slot(slot)

    @pl.when(jnp.logical_and(i == n - 1, n >= 2))
    def _():
        wait_slot(1 - slot)


def _dispatch(dest_rows, h2p, tm):
    n_rows = h2p.shape[0]
    n = n_rows // (ROW_CHUNKS * tm)
    return pl.pallas_call(
        _dispatch_kernel,
        grid=(n,),
        in_specs=[pl.BlockSpec((tm * TOP_K,), lambda i: (i,), memory_space=pltpu.SMEM),
                  pl.BlockSpec((ROW_CHUNKS * tm, LANES), lambda i: (i, 0))],
        out_specs=pl.BlockSpec(memory_space=pl.ANY),
        out_shape=jax.ShapeDtypeStruct((n_rows * TOP_K, LANES), U32),
        scratch_shapes=[pltpu.VMEM((2, ROW_CHUNKS * tm, LANES), U32), pltpu.SemaphoreType.DMA((2,))],
        compiler_params=_cparams(("arbitrary",)),
        name="moe_dispatch",
    )(dest_rows, h2p)


def _expert_kernel(iblk_ref, iexp_ref, ilo_ref, ihi_ref, xs_ref, w1_ref, w3_ref, w2_ref, ys_ref,
                   acc, hm, w1b, w3b, w2b):
    r = xs_ref.shape[0] // ROW_CHUNKS
    i = pl.program_id(0)
    n_items = pl.num_programs(0) - 1
    cur = jnp.minimum(i, n_items - 1)
    cur_prev = jnp.maximum(cur - 1, 0)
    dn = jnp.maximum(i - 1, 0)
    dn_prev = jnp.maximum(i - 2, 0)
    blk = iblk_ref[cur]
    lo = ilo_ref[cur]
    hi = ihi_ref[cur]
    dn_first = jnp.logical_or(i <= 1, iblk_ref[dn_prev] != iblk_ref[dn])
    slot = i % 2

    @pl.when(i == 0)
    def _():
        acc[...] = jnp.zeros_like(acc)
        hm[...] = jnp.zeros_like(hm)

    @pl.when(jnp.logical_or(i == 0, iexp_ref[cur_prev] != iexp_ref[cur]))
    def _():
        w1b[...] = w1_ref[0].astype(BF16)
        w3b[...] = w3_ref[0].astype(BF16)

    @pl.when(jnp.logical_or(i <= 1, iexp_ref[dn_prev] != iexp_ref[dn]))
    def _():
        w2b[...] = w2_ref[0].astype(BF16)

    n_half = 2
    rh = r // n_half

    def run(hm_rd, hm_wr):
        for hf in range(n_half):
            sl = slice(hf * rh, (hf + 1) * rh)
            y = jnp.dot(hm_rd[sl, :], w2b[...], preferred_element_type=F32)
            y = jnp.where(dn_first, y, acc[sl, :] + y)
            acc[sl, :] = y
            _store_token_rows(ys_ref, hf * rh, rh, y)

            rows = blk * r + hf * rh + lax.broadcasted_iota(jnp.int32, (rh, 1), 0)
            valid = jnp.logical_and(rows >= lo, rows < hi)
            x = _rows_to_features(_load_token_rows(xs_ref, hf * rh, rh))
            x = jnp.where(valid, x, 0.0).astype(BF16)
            a = jnp.dot(x, w1b[...], preferred_element_type=F32)
            bb = jnp.dot(x, w3b[...], preferred_element_type=F32)
            hm_wr[sl, :] = (a * jax.nn.sigmoid(a) * bb).astype(BF16)

    @pl.when(slot == 0)
    def _():
        run(hm.at[1], hm.at[0])

    @pl.when(slot == 1)
    def _():
        run(hm.at[0], hm.at[1])


def _experts(iblk, iexp, ilo, ihi, xs, w1, w3, w2, r):
    na, kw = xs.shape
    rows = r * ROW_CHUNKS
    n_items = iblk.shape[0]
    d = w1.shape[1]
    de = w1.shape[2]
    up = lambda i: jnp.minimum(i, n_items - 1)
    down = lambda i: jnp.maximum(i - 1, 0)
    grid_spec = pltpu.PrefetchScalarGridSpec(
        num_scalar_prefetch=4,
        grid=(n_items + 1,),
        in_specs=[pl.BlockSpec((rows, kw), lambda i, ib, ie, il, ih: (ib[up(i)], 0)),
                  pl.BlockSpec((1, d, de), lambda i, ib, ie, il, ih: (ie[up(i)], 0, 0)),
                  pl.BlockSpec((1, d, de), lambda i, ib, ie, il, ih: (ie[up(i)], 0, 0)),
                  pl.BlockSpec((1, de, d), lambda i, ib, ie, il, ih: (ie[down(i)], 0, 0))],
        out_specs=pl.BlockSpec((rows, kw), lambda i, ib, ie, il, ih: (ib[down(i)], 0)),
        scratch_shapes=[pltpu.VMEM((r, d), F32), pltpu.VMEM((2, r, de), BF16),
                        pltpu.VMEM((d, de), BF16), pltpu.VMEM((d, de), BF16), pltpu.VMEM((de, d), BF16)],
    )
    return pl.pallas_call(
        _expert_kernel,
        grid_spec=grid_spec,
        out_shape=jax.ShapeDtypeStruct((na, kw), U32),
        compiler_params=_cparams(("arbitrary",)),
        name="moe_experts",
    )(iblk, iexp, ilo, ihi, xs, w1, w3, w2)


def _combine_kernel(dest_ref, dnext_ref, x1_ref, h2p_ref, wts_ref, g2_ref, sw1_ref, sw3_ref, sw2_ref,
                    fg_ref, ys_ref, o_ref, gbuf, sem):
    tm = x1_ref.shape[0]
    i = pl.program_id(0)
    n = pl.num_programs(0)
    slot = i % 2

    rows = ROW_CHUNKS * tm

    def issue(dref, s):
        def body(t, carry):
            off = pl.multiple_of(t * ROW_CHUNKS, ROW_CHUNKS)
            for j in range(TOP_K):
                d = pl.multiple_of(dref[t * TOP_K + j], ROW_CHUNKS)
                pltpu.make_async_copy(ys_ref.at[pl.ds(d, ROW_CHUNKS)],
                                      gbuf.at[s, j, pl.ds(off, ROW_CHUNKS)], sem.at[s]).start(priority=j % 2)
            return carry
        lax.fori_loop(0, tm, body, 0)

    @pl.when(i == 0)
    def _():
        issue(dest_ref, 0)

    @pl.when(i + 1 < n)
    def _():
        issue(dnext_ref, 1 - slot)

    for _ in range(TOP_K):
        pltpu.make_async_copy(ys_ref.at[pl.ds(0, rows)], gbuf.at[slot, 0], sem.at[slot]).wait()

    wts = wts_ref[...]
    moe = None
    for j in range(TOP_K):
        wj = wts[:, j:j + 1]
        yj = [(lo * wj, hi * wj) for lo, hi in _load_token_rows(gbuf.at[slot, j], 0, tm)]
        moe = yj if moe is None else [(a + lo, b + hi) for (a, b), (lo, hi) in zip(moe, yj)]
    h = _rows_to_features(_load_token_rows(h2p_ref, 0, tm)).astype(BF16)
    a = jnp.dot(h, sw1_ref[...], preferred_element_type=F32)
    bb = jnp.dot(h, sw3_ref[...], preferred_element_type=F32)
    shared = jnp.dot((a * jax.nn.sigmoid(a) * bb).astype(BF16), sw2_ref[...], preferred_element_type=F32)
    ffn = _rows_to_features(moe) + shared
    xo = x1_ref[...] + g2_ref[0] * ffn
    o_ref[...] = _rmsnorm_rows(xo, fg_ref[...])


def _combine(dest_rows, x1, h2p, wts, g2, sw1, sw3, sw2, fg, ys, tiles_per_batch, tm):
    t, d = x1.shape
    n = t // tm
    de = sw1.shape[1]
    const2 = lambda shape: pl.BlockSpec(shape, lambda i: (0, 0))
    return pl.pallas_call(
        _combine_kernel,
        grid=(n,),
        in_specs=[pl.BlockSpec((tm * TOP_K,), lambda i: (i,), memory_space=pltpu.SMEM),
                  pl.BlockSpec((tm * TOP_K,), lambda i: (jnp.minimum(i + 1, n - 1),),
                               memory_space=pltpu.SMEM),
                  pl.BlockSpec((tm, d), lambda i: (i, 0)),
                  pl.BlockSpec((ROW_CHUNKS * tm, LANES), lambda i: (i, 0)),
                  pl.BlockSpec((tm, TOP_K), lambda i: (i, 0)),
                  pl.BlockSpec((1, 1, d), lambda i: (i // tiles_per_batch, 0, 0)),
                  const2((d, de)), const2((d, de)), const2((de, d)), const2((1, d)),
                  pl.BlockSpec(memory_space=pl.ANY)],
        out_specs=pl.BlockSpec((tm, d), lambda i: (i, 0)),
        out_shape=jax.ShapeDtypeStruct((t, d), F32),
        scratch_shapes=[pltpu.VMEM((2, TOP_K, ROW_CHUNKS * tm, LANES), U32),
                        pltpu.SemaphoreType.DMA((2,))],
        compiler_params=_cparams(("arbitrary",)),
        name="moe_combine",
    )(dest_rows, dest_rows, x1, h2p, wts, g2, sw1, sw3, sw2, fg, ys)


def _route_meta(counts, slot_e, slot_r, block_rows):
    t = slot_e.shape[1]
    ends = jnp.cumsum(counts)
    starts = ends - counts
    dest_t = slot_r + jnp.take(starts, slot_e, axis=0).astype(jnp.int32)

    n_blk = t * TOP_K // block_rows
    n_items = n_blk + N_EXPERTS - 1
    first_blk = starts // block_rows
    last_blk = (ends - 1) // block_rows
    n_e = jnp.where(counts > 0, last_blk - first_blk + 1, 0)
    item_end = jnp.cumsum(n_e)
    item_off = item_end - n_e
    total = item_end[-1]
    idx = jnp.arange(n_items, dtype=jnp.int32)
    e_of = jnp.minimum(jnp.sum(item_end[None, :] <= idx[:, None], axis=1), N_EXPERTS - 1).astype(jnp.int32)
    live = idx < total
    blk = first_blk[e_of] + (idx - item_off[e_of])
    lo = jnp.maximum(starts[e_of], blk * block_rows)
    hi = jnp.minimum(ends[e_of], (blk + 1) * block_rows)
    last_e = e_of[jnp.maximum(total - 1, 0)]
    iblk = jnp.where(live, blk, n_blk - 1).astype(jnp.int32)
    iexp = jnp.where(live, e_of, last_e).astype(jnp.int32)
    ilo = jnp.where(live, lo, 0).astype(jnp.int32)
    ihi = jnp.where(live, hi, 0).astype(jnp.int32)
    return dest_t, iblk, iexp, ilo, ihi


def _rope_tables(n_tokens):
    n_rows = n_tokens // GRID_W
    rows = np.repeat(np.arange(n_rows, dtype=np.float32), GRID_W)
    cols = np.tile(np.arange(GRID_W, dtype=np.float32), n_rows)
    n_freq = HEAD_DIM // 4
    inv_freq = (np.float32(ROPE_BASE) ** (-np.arange(n_freq, dtype=np.float32) / n_freq)).astype(np.float32)
    ar = rows[:, None] * inv_freq
    ac = cols[:, None] * inv_freq
    cos64 = np.concatenate([np.cos(ar), np.cos(ar), np.cos(ac), np.cos(ac)], axis=1)
    sin64 = np.concatenate([-np.sin(ar), np.sin(ar), -np.sin(ac), np.sin(ac)], axis=1)
    return np.tile(cos64, (1, 2)).astype(np.float32), np.tile(sin64, (1, 2)).astype(np.float32)


def kernel(x, c, ctx, c_ctx, ada_w, ada_b, norm1_g, w_in, lambda_q1, lambda_k1, lambda_q2, lambda_k2,
           dattn_norm_g, ret_decay_fwd, ret_decay_bwd, ret_norm_g, w_out, norm2_g, router_w, router_bias,
           exp_w1, exp_w3, exp_w2, shared_w1, shared_w3, shared_w2, final_norm_g):
    b, s, d = x.shape
    lc = ctx.shape[1]
    depth = ada_w.shape[0]
    assert depth == 1, "single-layer problem: the context stream is never updated"
    l = 0
    lam_init = 0.8 - 0.6 * math.exp(-0.3 * l)

    c8 = jnp.zeros((8, d), F32).at[:b].set(c).at[b].set(c_ctx)
    mod = _modulation(c8, ada_w[l], ada_b[l])
    sh1, sc1, g1, sh2, sc2, g2 = [mod[:b, k * d:(k + 1) * d].reshape(b, 1, d) for k in range(6)]
    csh1 = jnp.broadcast_to(mod[b, 0:d].reshape(1, 1, d), (b, 1, d))
    csc1 = jnp.broadcast_to(mod[b, d:2 * d].reshape(1, 1, d), (b, 1, d))

    w_l = w_in[l]
    w_tok = jnp.concatenate([w_l[:, 512:1024], w_l[:, 1536:3072]], axis=1).astype(BF16)
    w_feat = jnp.concatenate([w_l[:, 0:512], w_l[:, 1024:1536]], axis=1).T.astype(BF16)
    n1g = norm1_g[l].reshape(1, d)
    cos, sin = _rope_tables(s)
    ones_c = jnp.ones((lc, LANES), F32)
    zeros_c = jnp.zeros((lc, LANES), F32)
    px, qt, vtx = _inproj(x, n1g, sc1, sh1, cos, sin, np.ascontiguousarray(cos.T),
                          np.ascontiguousarray(sin.T), w_tok, w_feat, tm=512)
    pc, _, vtc = _inproj(ctx, n1g, csc1, csh1, ones_c, zeros_c, ones_c.T, zeros_c.T, w_tok, w_feat, tm=lc)

    lam = (jnp.exp(jnp.sum(lambda_q1[l] * lambda_k1[l])) - jnp.exp(jnp.sum(lambda_q2[l] * lambda_k2[l]))
           ).astype(F32) + lam_init
    att = _attention(lam.reshape(1), qt, px, vtx, pc, vtc, tq=min(2048, s), tkc=min(512, s // 4))

    dmat, qdec, kdec, gch = _retention_tables(ret_decay_fwd[l], ret_decay_bwd[l])
    ret_f, ret_b = _retention(px, pc, dmat, qdec, kdec, gch)

    x1, h2p, slot_e, slot_w, slot_r, counts = _merge(
        att, ret_f, ret_b, px, x, g1, sh2, sc2,
        dattn_norm_g[l].reshape(1, 2 * HEAD_DIM), ret_norm_g[l].reshape(1, R_DV),
        w_out[l].astype(BF16), norm2_g[l].reshape(1, d), router_w[l].T.astype(BF16),
        router_bias[l].astype(F32).reshape(N_EXPERTS, 1), 1.0 - lam_init, tm=512)

    t = b * s
    block_rows = 512
    dest_t, iblk, iexp, ilo, ihi = _route_meta(counts[:, 0].astype(jnp.int32), slot_e, slot_r, block_rows)
    dest_rows = (dest_t.T * ROW_CHUNKS).reshape(t * TOP_K)
    xs = _dispatch(dest_rows, h2p, tm=128)
    ys = _experts(iblk, iexp, ilo, ihi, xs, exp_w1[l], exp_w3[l], exp_w2[l], block_rows)
    tm_c = 128
    out = _combine(dest_rows, x1.reshape(t, d), h2p, slot_w.T, g2, shared_w1[l].astype(BF16),
                   shared_w3[l].astype(BF16), shared_w2[l].astype(BF16), final_norm_g.reshape(1, d),
                   ys, s // tm_c, tm_c)
    return out.reshape(b, s, d)
```

```python
import functools
import math

import jax
import jax.numpy as jnp
import numpy as np
from jax import lax
from jax.experimental import pallas as pl
from jax.experimental.pallas import tpu as pltpu

F32 = jnp.float32
BF16 = jnp.bfloat16
U32 = jnp.uint32

D_MODEL = 1024
GRID_W = 64
HEAD_DIM = 64
A_HEADS = 4
R_HEADS = 4
R_DK = 64
R_DV = 128
RET_CHUNK = 128
ROPE_BASE = 10000.0
N_EXPERTS = 64
N_GROUPS = 8
TOPK_GROUPS = 4
TOP_K = 8
D_EXPERT = 256
ROUTED_SCALE = 2.5
NORM_EPS = 1e-6
LANES = 128

P_KZ = 0
P_RQ = 1024
P_RK = 1280
P_RV = 1536
P_RG = 2048
P_COLS = 2560
VT_ROWS = 144
LOG2E = 1.4426950408889634

VMEM_LIMIT = 56 * 1024 * 1024


def _cparams(sem):
    return pltpu.CompilerParams(dimension_semantics=sem, vmem_limit_bytes=VMEM_LIMIT)


def _mod_kernel(c_ref, w_ref, b_ref, o_ref):
    c = c_ref[...]
    s = c * jax.nn.sigmoid(c)
    o_ref[...] = jnp.dot(s, w_ref[...], preferred_element_type=F32,
                         precision=lax.Precision.HIGHEST) + b_ref[...]


def _modulation(c8, ada_w, ada_b):
    d = c8.shape[1]
    n = ada_w.shape[1]
    tn = 512
    return pl.pallas_call(
        _mod_kernel,
        grid=(n // tn,),
        in_specs=[pl.BlockSpec((8, d), lambda j: (0, 0)),
                  pl.BlockSpec((d, tn), lambda j: (0, j)),
                  pl.BlockSpec((1, tn), lambda j: (0, j))],
        out_specs=pl.BlockSpec((8, tn), lambda j: (0, j)),
        out_shape=jax.ShapeDtypeStruct((8, n), F32),
        compiler_params=_cparams(("arbitrary",)),
        name="modulation",
    )(c8, ada_w, ada_b.reshape(1, n))


def _rmsnorm_rows(x, g):
    return x * lax.rsqrt(jnp.mean(x * x, axis=-1, keepdims=True) + NORM_EPS) * g


def _inproj_kernel(x_ref, g_ref, sc_ref, sh_ref, cos_ref, sin_ref, cost_ref, sint_ref, w_ref, wt_ref,
                   p_ref, qt_ref, vt_ref):
    x = x_ref[0]
    tm = x.shape[0]
    h = _rmsnorm_rows(x, g_ref[...]) * (1.0 + sc_ref[0]) + sh_ref[0]
    hb = h.astype(BF16)
    p = jnp.dot(hb, w_ref[...], preferred_element_type=F32)
    pt = lax.dot_general(wt_ref[...], hb, _NT, preferred_element_type=F32)

    cos = cos_ref[...]
    sin = sin_ref[...]
    lane = lax.broadcasted_iota(jnp.int32, (tm, LANES), 1)
    first_half = (lane % 32) < 16
    low64 = lane < 64

    def rope(xs):
        partner = jnp.where(first_half, pltpu.roll(xs, LANES - 16, 1), pltpu.roll(xs, 16, 1))
        return xs * cos + partner * sin

    zero = jnp.zeros((tm, LANES), BF16)
    for hh in range(A_HEADS):
        k = rope(p[:, hh * 128:(hh + 1) * 128]).astype(BF16)
        p_ref[0, :, P_KZ + hh * 256:P_KZ + hh * 256 + 128] = jnp.where(low64, k, zero)
        p_ref[0, :, P_KZ + hh * 256 + 128:P_KZ + (hh + 1) * 256] = jnp.where(low64, zero, k)
    for cc in range(2):
        rq = rope(p[:, 512 + cc * 128:512 + (cc + 1) * 128])
        p_ref[0, :, P_RQ + cc * 128:P_RQ + (cc + 1) * 128] = rq.astype(BF16)
        rk = rope(p[:, 768 + cc * 128:768 + (cc + 1) * 128]) * (R_DK ** -0.5)
        p_ref[0, :, P_RK + cc * 128:P_RK + (cc + 1) * 128] = rk.astype(BF16)
    p_ref[0, :, P_RV:P_RV + 512] = p[:, 1024:1536].astype(BF16)
    p_ref[0, :, P_RG:P_RG + 512] = p[:, 1536:2048].astype(BF16)

    cost = cost_ref[...]
    sint = sint_ref[...]
    q_scale = (HEAD_DIM ** -0.5) * LOG2E
    ones_rows = jnp.where(lax.broadcasted_iota(jnp.int32, (VT_ROWS - 128, tm), 0) == 0, 1.0, 0.0)
    for hh in range(A_HEADS):
        xq = pt[hh * 128:(hh + 1) * 128, :]
        groups = [xq[16 * gi:16 * (gi + 1), :] for gi in range(8)]
        partner = jnp.concatenate([groups[gi ^ 1] for gi in range(8)], axis=0)
        qt_ref[0, hh] = ((xq * cost + partner * sint) * q_scale).astype(BF16)
        vt_ref[0, hh, 0:128, :] = pt[512 + hh * 128:512 + (hh + 1) * 128, :].astype(BF16)
        vt_ref[0, hh, 128:VT_ROWS, :] = ones_rows.astype(BF16)


def _inproj(x, g, sc, sh, cos, sin, cost, sint, w_bf, wt_bf, tm):
    b, s, d = x.shape
    n_tok = w_bf.shape[1]
    n_t = wt_bf.shape[0]
    const2 = lambda shape: pl.BlockSpec(shape, lambda bi, i: (0, 0))
    per_b = pl.BlockSpec((1, 1, d), lambda bi, i: (bi, 0, 0))
    return pl.pallas_call(
        _inproj_kernel,
        grid=(b, s // tm),
        in_specs=[pl.BlockSpec((1, tm, d), lambda bi, i: (bi, i, 0)),
                  const2((1, d)), per_b, per_b,
                  pl.BlockSpec((tm, LANES), lambda bi, i: (i, 0)),
                  pl.BlockSpec((tm, LANES), lambda bi, i: (i, 0)),
                  pl.BlockSpec((LANES, tm), lambda bi, i: (0, i)),
                  pl.BlockSpec((LANES, tm), lambda bi, i: (0, i)),
                  const2((d, n_tok)), const2((n_t, d))],
        out_specs=[pl.BlockSpec((1, tm, P_COLS), lambda bi, i: (bi, i, 0)),
                   pl.BlockSpec((1, A_HEADS, LANES, tm), lambda bi, i: (bi, 0, 0, i)),
                   pl.BlockSpec((1, A_HEADS, VT_ROWS, tm), lambda bi, i: (bi, 0, 0, i))],
        out_shape=[jax.ShapeDtypeStruct((b, s, P_COLS), BF16),
                   jax.ShapeDtypeStruct((b, A_HEADS, LANES, s), BF16),
                   jax.ShapeDtypeStruct((b, A_HEADS, VT_ROWS, s), BF16)],
        compiler_params=_cparams(("arbitrary", "arbitrary")),
        name="inproj",
    )(x, g, sc, sh, cos, sin, cost, sint, w_bf, wt_bf)


_NT = (((1,), (1,)), ((), ()))


def _attn_kernel(lam_ref, qt_ref, kzc_ref, vtc_ref, kzx_ref, vtx_ref, o_ref,
                 acc, s_a, s_b, e_a, e_b, *, tkc):
    qt = qt_ref[0, 0]
    tq = qt.shape[1]
    n_chunks = kzx_ref.shape[1] // tkc
    assert n_chunks % 2 == 0 and n_chunks >= 4

    def scores(kz, s_ref):
        for mp in range(2):
            s_ref[mp] = jnp.dot(kz[:, mp * LANES:(mp + 1) * LANES], qt, preferred_element_type=F32)

    def softmax(s_ref, e_ref, ms):
        new_ms, alphas = [], []
        for mp in range(2):
            st = s_ref[mp]
            m_new = jnp.maximum(ms[mp], jnp.max(st, axis=0, keepdims=True))
            alphas.append(jnp.exp2(ms[mp] - m_new))
            e_ref[mp] = jnp.exp2(st - m_new).astype(BF16)
            new_ms.append(m_new)
        return tuple(new_ms), tuple(alphas)

    def values(vt, e_ref, alphas):
        for mp in range(2):
            acc[mp] = alphas[mp] * acc[mp] + jnp.dot(vt, e_ref[mp], preferred_element_type=F32)

    def kchunk(c):
        return kzx_ref[0, pl.ds(pl.multiple_of(c * tkc, tkc), tkc), :]

    def vchunk(c):
        return vtx_ref[0, 0, :, pl.ds(pl.multiple_of(c * tkc, tkc), tkc)]

    ms = []
    for mp in range(2):
        st = jnp.dot(kzc_ref[0, :, mp * LANES:(mp + 1) * LANES], qt, preferred_element_type=F32)
        m_new = jnp.max(st, axis=0, keepdims=True)
        acc[mp] = jnp.dot(vtc_ref[0, 0], jnp.exp2(st - m_new).astype(BF16), preferred_element_type=F32)
        ms.append(m_new)
    ms = tuple(ms)

    scores(kchunk(0), s_a)
    scores(kchunk(1), s_b)
    ms, al_a = softmax(s_a, e_a, ms)

    def body(j, carry):
        ms, al_a = carry
        scores(kchunk(2 * j + 2), s_a)
        ms, al_b = softmax(s_b, e_b, ms)
        values(vchunk(2 * j), e_a, al_a)
        scores(kchunk(2 * j + 3), s_b)
        ms, al_a = softmax(s_a, e_a, ms)
        values(vchunk(2 * j + 1), e_b, al_b)
        return ms, al_a

    ms, al_a = lax.fori_loop(0, n_chunks // 2 - 1, body, (ms, al_a))
    ms, al_b = softmax(s_b, e_b, ms)
    values(vchunk(n_chunks - 2), e_a, al_a)
    values(vchunk(n_chunks - 1), e_b, al_b)

    a1 = acc[0]
    a2 = acc[1]
    o1 = a1[:LANES, :] / a1[LANES:LANES + 1, :]
    o2 = a2[:LANES, :] / a2[LANES:LANES + 1, :]
    o_ref[0] = (o1 - lam_ref[0] * o2).T.astype(BF16)


def _attention(lam, qt, px, vtx, pc, vtc, tq, tkc):
    b, s, _ = px.shape
    lc = pc.shape[1]
    kern = functools.partial(_attn_kernel, tkc=tkc)
    return pl.pallas_call(
        kern,
        grid=(b, A_HEADS, s // tq),
        in_specs=[pl.BlockSpec(memory_space=pltpu.SMEM),
                  pl.BlockSpec((1, 1, LANES, tq), lambda bi, h, i: (bi, h, 0, i)),
                  pl.BlockSpec((1, lc, 256), lambda bi, h, i: (bi, 0, P_KZ // 256 + h)),
                  pl.BlockSpec((1, 1, VT_ROWS, lc), lambda bi, h, i: (bi, h, 0, 0)),
                  pl.BlockSpec((1, s, 256), lambda bi, h, i: (bi, 0, P_KZ // 256 + h)),
                  pl.BlockSpec((1, 1, VT_ROWS, s), lambda bi, h, i: (bi, h, 0, 0))],
        out_specs=pl.BlockSpec((1, tq, LANES), lambda bi, h, i: (bi, i, h)),
        out_shape=jax.ShapeDtypeStruct((b, s, A_HEADS * LANES), BF16),
        scratch_shapes=[pltpu.VMEM((2, VT_ROWS, tq), F32),
                        pltpu.VMEM((2, tkc, tq), F32), pltpu.VMEM((2, tkc, tq), F32),
                        pltpu.VMEM((2, tkc, tq), BF16), pltpu.VMEM((2, tkc, tq), BF16)],
        compiler_params=_cparams(("arbitrary", "arbitrary", "arbitrary")),
        name="diff_attention",
    )(lam, qt, pc, vtc, px, vtx)


_TN = (((0,), (0,)), ((), ()))


def _ret_kernel(qf_ref, kf_ref, vf_ref, qb_ref, kb_ref, vb_ref, kc_ref, vc_ref,
                dmat_ref, qdec_ref, kdec_ref, gch_ref, of_ref, ob_ref, sf, sb):
    c = RET_CHUNK
    n = pl.program_id(1)
    col_head = lax.broadcasted_iota(jnp.int32, (c, R_HEADS * R_DK), 1) // R_DK
    bd_mask = (lax.broadcasted_iota(jnp.int32, (R_HEADS * R_DK, R_HEADS * R_DV), 0) // R_DK
               == lax.broadcasted_iota(jnp.int32, (R_HEADS * R_DK, R_HEADS * R_DV), 1) // R_DV)

    def state_update(s_ref, k, v, kdec, gch):
        kd = (k.astype(F32) * kdec).astype(BF16)
        kv = lax.dot_general(kd, v, _TN, preferred_element_type=F32)
        s_ref[...] = gch * s_ref[...] + jnp.where(bd_mask, kv, 0.0)

    def chunk(s_ref, q, k, v, dmat, qdec, kdec, gch):
        zero = jnp.zeros_like(k)
        kblk = jnp.concatenate([jnp.where(col_head == h, k, zero) for h in range(R_HEADS)], axis=0)
        sc = lax.dot_general(q, kblk, _NT, preferred_element_type=F32) * dmat
        p = sc.astype(BF16)
        intra = jnp.concatenate(
            [jnp.dot(p[:, h * c:(h + 1) * c], v[:, h * R_DV:(h + 1) * R_DV], preferred_element_type=F32)
             for h in range(R_HEADS)], axis=1)
        qd = (q.astype(F32) * qdec).astype(BF16)
        cross = jnp.dot(qd, s_ref[...].astype(BF16), preferred_element_type=F32)
        out = intra + cross
        state_update(s_ref, k, v, kdec, gch)
        return out

    @pl.when(n == 0)
    def _():
        sf[...] = jnp.zeros_like(sf)
        sb[...] = jnp.zeros_like(sb)
        n_ctx = kc_ref.shape[1] // c
        for cc in range(n_ctx):
            state_update(sf, kc_ref[0, cc * c:(cc + 1) * c, :], vc_ref[0, cc * c:(cc + 1) * c, :],
                         kdec_ref[0], gch_ref[0])
        for cc in reversed(range(n_ctx)):
            state_update(sb, kc_ref[0, cc * c:(cc + 1) * c, :], vc_ref[0, cc * c:(cc + 1) * c, :],
                         kdec_ref[1], gch_ref[1])

    of_ref[0] = chunk(sf, qf_ref[0], kf_ref[0], vf_ref[0], dmat_ref[0], qdec_ref[0], kdec_ref[0],
                      gch_ref[0]).astype(BF16)
    ob_ref[0] = chunk(sb, qb_ref[0], kb_ref[0], vb_ref[0], dmat_ref[1], qdec_ref[1], kdec_ref[1],
                      gch_ref[1]).astype(BF16)


def _retention(px, pc, dmat, qdec, kdec, gch):
    b, s, _ = px.shape
    lc = pc.shape[1]
    c = RET_CHUNK
    nch = s // c
    qk_w = R_HEADS * R_DK
    v_w = R_HEADS * R_DV
    fwd = lambda bi, n: (bi, n, 0)
    spec_q = lambda im: pl.BlockSpec((1, c, qk_w), lambda bi, n: im(bi, n)[:2] + (P_RQ // qk_w,))
    spec_k = lambda im: pl.BlockSpec((1, c, qk_w), lambda bi, n: im(bi, n)[:2] + (P_RK // qk_w,))
    spec_v = lambda im: pl.BlockSpec((1, c, v_w), lambda bi, n: im(bi, n)[:2] + (P_RV // v_w,))
    bwd = lambda bi, n: (bi, nch - 1 - n, 0)
    const3 = lambda shape: pl.BlockSpec(shape, lambda bi, n: (0, 0, 0))
    return pl.pallas_call(
        _ret_kernel,
        grid=(b, nch),
        in_specs=[spec_q(fwd), spec_k(fwd), spec_v(fwd), spec_q(bwd), spec_k(bwd), spec_v(bwd),
                  pl.BlockSpec((1, lc, qk_w), lambda bi, n: (bi, 0, P_RK // qk_w)),
                  pl.BlockSpec((1, lc, v_w), lambda bi, n: (bi, 0, P_RV // v_w)),
                  const3((2, c, R_HEADS * c)), const3((2, c, qk_w)), const3((2, c, qk_w)),
                  const3((2, 1, v_w))],
        out_specs=[pl.BlockSpec((1, c, v_w), fwd), pl.BlockSpec((1, c, v_w), bwd)],
        out_shape=[jax.ShapeDtypeStruct((b, s, v_w), BF16), jax.ShapeDtypeStruct((b, s, v_w), BF16)],
        scratch_shapes=[pltpu.VMEM((qk_w, v_w), F32), pltpu.VMEM((qk_w, v_w), F32)],
        compiler_params=_cparams(("arbitrary", "arbitrary")),
        name="retention",
    )(px, px, px, px, px, px, pc, pc, dmat, qdec, kdec, gch)


def _retention_tables(decay_fwd, decay_bwd):
    c = RET_CHUNK
    pos = jnp.arange(c, dtype=F32)
    rel = pos[:, None] - pos[None, :]

    def tables(logits, backward):
        lg = jax.nn.log_sigmoid(logits.astype(F32))
        r = -rel if backward else rel
        dm = jnp.where(r[None] >= 0, jnp.exp(jnp.maximum(r, 0.0)[None] * lg[:, None, None]), 0.0)
        dm = jnp.transpose(dm, (1, 0, 2)).reshape(c, R_HEADS * c)
        qpow = (c - pos) if backward else (pos + 1)
        kpow = pos if backward else (c - 1 - pos)
        qd = jnp.repeat(jnp.exp(qpow[:, None] * lg[None, :]), R_DK, axis=1)
        kd = jnp.repeat(jnp.exp(kpow[:, None] * lg[None, :]), R_DK, axis=1)
        gc = jnp.repeat(jnp.exp(c * lg), R_DV)[None, :]
        return dm, qd, kd, gc

    tf = tables(decay_fwd, False)
    tb = tables(decay_bwd, True)
    return tuple(jnp.stack([a, bb]) for a, bb in zip(tf, tb))


def _pack_pairs(x):
    k = x.shape[1] // 2
    lo = lax.bitcast_convert_type(x[:, :k].astype(BF16).astype(F32), U32)
    hi = lax.bitcast_convert_type(x[:, k:].astype(BF16).astype(F32), U32)
    return (lo >> 16) | (hi & jnp.uint32(0xFFFF0000))


def _unpack_pairs(w):
    lo = lax.bitcast_convert_type(w << 16, F32)
    hi = lax.bitcast_convert_type(w & jnp.uint32(0xFFFF0000), F32)
    return lo, hi


ROW_CHUNKS = (D_MODEL // 2) // LANES


def _store_token_rows(ref, first_token, n_tokens, x):
    packed = _pack_pairs(x)
    for c in range(ROW_CHUNKS):
        ref[pl.ds(first_token * ROW_CHUNKS + c, n_tokens, stride=ROW_CHUNKS), :] = (
            packed[:, c * LANES:(c + 1) * LANES])


def _load_token_rows(ref, first_token, n_tokens):
    return [_unpack_pairs(ref[pl.ds(first_token * ROW_CHUNKS + c, n_tokens, stride=ROW_CHUNKS), :])
            for c in range(ROW_CHUNKS)]


def _rows_to_features(chunks):
    return jnp.concatenate([lo for lo, _ in chunks] + [hi for _, hi in chunks], axis=1)


def _route_tile(logits_t, bias_col, carry_ref):
    n_e, tm = logits_t.shape
    per = n_e // N_GROUPS
    neg = -jnp.inf
    scores = jax.nn.sigmoid(logits_t)
    sel = scores + bias_col
    sub = lax.broadcasted_iota(jnp.int32, (per, tm), 0).astype(F32)
    sel_g = [sel[g * per:(g + 1) * per, :] for g in range(N_GROUPS)]
    flat_g = [sub + float(g * per) for g in range(N_GROUPS)]

    gs = []
    for g in range(N_GROUPS):
        top1 = jnp.max(sel_g[g], axis=0, keepdims=True)
        idx1 = jnp.min(jnp.where(sel_g[g] == top1, sub, float(per)), axis=0, keepdims=True)
        top2 = jnp.max(jnp.where(sub == idx1, neg, sel_g[g]), axis=0, keepdims=True)
        gs.append(top1 + top2)
    gsel = [jnp.zeros((1, tm), F32) for _ in range(N_GROUPS)]
    for _ in range(TOPK_GROUPS):
        m = functools.reduce(jnp.maximum, gs)
        found = jnp.zeros((1, tm), F32)
        for g in range(N_GROUPS):
            hit = jnp.logical_and(gs[g] == m, found == 0.0)
            found = jnp.where(hit, 1.0, found)
            gsel[g] = jnp.where(hit, 1.0, gsel[g])
            gs[g] = jnp.where(hit, neg, gs[g])

    masked = [jnp.where(gsel[g] > 0.0, sel_g[g], neg) for g in range(N_GROUPS)]
    chosen = [jnp.zeros((per, tm), F32) for _ in range(N_GROUPS)]
    for _ in range(TOP_K):
        m = functools.reduce(jnp.maximum, [jnp.max(mg, axis=0, keepdims=True) for mg in masked])
        idx = functools.reduce(jnp.minimum, [
            jnp.min(jnp.where(masked[g] == m, flat_g[g], float(n_e)), axis=0, keepdims=True)
            for g in range(N_GROUPS)])
        for g in range(N_GROUPS):
            hit = flat_g[g] == idx
            chosen[g] = jnp.where(hit, 1.0, chosen[g])
            masked[g] = jnp.where(hit, neg, masked[g])
    chosen = jnp.concatenate(chosen, axis=0)
    flat = jnp.concatenate(flat_g, axis=0)

    w_dense = scores * chosen
    w_dense = w_dense / jnp.sum(w_dense, axis=0, keepdims=True) * ROUTED_SCALE

    chosen_bf = chosen.astype(BF16)
    lower = (lax.broadcasted_iota(jnp.int32, (n_e, n_e), 1)
             < lax.broadcasted_iota(jnp.int32, (n_e, n_e), 0)).astype(BF16)
    slot_idx = jnp.dot(lower, chosen_bf, preferred_element_type=F32)
    upper = (lax.broadcasted_iota(jnp.int32, (tm, tm), 0)
             < lax.broadcasted_iota(jnp.int32, (tm, tm), 1)).astype(BF16)
    rank = jnp.dot(chosen_bf, upper, preferred_element_type=F32) + carry_ref[...]
    carry_ref[...] += jnp.sum(chosen, axis=1, keepdims=True)

    slots = []
    for j in range(TOP_K):
        ind = jnp.where(slot_idx == float(j), chosen, 0.0)
        slots.append((jnp.sum(ind * flat, axis=0, keepdims=True),
                      jnp.sum(ind * w_dense, axis=0, keepdims=True),
                      jnp.sum(ind * rank, axis=0, keepdims=True)))
    return slots


def _merge_kernel(att_ref, rf_ref, rb_ref, gate_ref, x_ref, g1_ref, sh2_ref, sc2_ref, ga_ref, gr_ref,
                  wout_ref, n2g_ref, rwt_ref, rbias_ref, x1_ref, h2p_ref, se_ref, sw_ref, sr_ref, cnt_ref,
                  carry, *, att_scale):
    @pl.when(jnp.logical_and(pl.program_id(0) == 0, pl.program_id(1) == 0))
    def _():
        carry[...] = jnp.zeros_like(carry)

    att = att_ref[0].astype(F32)
    ret = rf_ref[0].astype(F32) + rb_ref[0].astype(F32)
    gate = gate_ref[0].astype(F32)
    parts = []
    for h in range(A_HEADS):
        a = att[:, h * 128:(h + 1) * 128]
        parts.append(_rmsnorm_rows(a, ga_ref[...]) * att_scale)
    for h in range(R_HEADS):
        r = _rmsnorm_rows(ret[:, h * R_DV:(h + 1) * R_DV], gr_ref[...])
        g = gate[:, h * R_DV:(h + 1) * R_DV]
        parts.append(g * jax.nn.sigmoid(g) * r)
    merged = jnp.concatenate(parts, axis=1).astype(BF16)
    y = jnp.dot(merged, wout_ref[...], preferred_element_type=F32)
    x1 = x_ref[0] + g1_ref[0] * y
    x1_ref[0] = x1
    h2 = _rmsnorm_rows(x1, n2g_ref[...]) * (1.0 + sc2_ref[0]) + sh2_ref[0]
    _store_token_rows(h2p_ref, 0, h2.shape[0], h2)
    logits_t = lax.dot_general(rwt_ref[...], h2.astype(BF16), _NT, preferred_element_type=F32)
    slots = _route_tile(logits_t, rbias_ref[...], carry)
    for j, (e_j, w_j, r_j) in enumerate(slots):
        se_ref[j:j + 1, :] = e_j.astype(jnp.int32)
        sw_ref[j:j + 1, :] = w_j
        sr_ref[j:j + 1, :] = r_j.astype(jnp.int32)
    cnt_ref[...] = carry[...]


def _merge(att, rf, rb, px, x, g1, sh2, sc2, ga, gr, wout_bf, n2g, rwt_bf, rbias, att_scale, tm):
    b, s, d = x.shape
    nt = s // tm
    chunked = pl.BlockSpec((ROW_CHUNKS * tm, LANES), lambda bi, i: (bi * nt + i, 0))
    chunked_shape = jax.ShapeDtypeStruct((b * s * ROW_CHUNKS, LANES), U32)
    row = lambda bi, i: (bi, i, 0)
    tok = lambda bi, i: (0, bi * nt + i)
    per_b = pl.BlockSpec((1, 1, d), lambda bi, i: (bi, 0, 0))
    const2 = lambda shape: pl.BlockSpec(shape, lambda bi, i: (0, 0))
    return pl.pallas_call(
        functools.partial(_merge_kernel, att_scale=att_scale),
        grid=(b, nt),
        in_specs=[pl.BlockSpec((1, tm, 512), row), pl.BlockSpec((1, tm, 512), row),
                  pl.BlockSpec((1, tm, 512), row),
                  pl.BlockSpec((1, tm, 512), lambda bi, i: (bi, i, P_RG // 512)),
                  pl.BlockSpec((1, tm, d), row), per_b, per_b, per_b,
                  const2((1, 128)), const2((1, R_DV)), const2((d, d)), const2((1, d)),
                  const2((N_EXPERTS, d)), const2((N_EXPERTS, 1))],
        out_specs=[pl.BlockSpec((1, tm, d), row), chunked,
                   pl.BlockSpec((TOP_K, tm), tok), pl.BlockSpec((TOP_K, tm), tok),
                   pl.BlockSpec((TOP_K, tm), tok), const2((N_EXPERTS, 1))],
        out_shape=[jax.ShapeDtypeStruct((b, s, d), F32), chunked_shape,
                   jax.ShapeDtypeStruct((TOP_K, b * s), jnp.int32),
                   jax.ShapeDtypeStruct((TOP_K, b * s), F32),
                   jax.ShapeDtypeStruct((TOP_K, b * s), jnp.int32),
                   jax.ShapeDtypeStruct((N_EXPERTS, 1), F32)],
        scratch_shapes=[pltpu.VMEM((N_EXPERTS, 1), F32)],
        compiler_params=_cparams(("arbitrary", "arbitrary")),
        name="mixer_merge",
    )(att, rf, rb, px, x, g1, sh2, sc2, ga, gr, wout_bf, n2g, rwt_bf, rbias)


def _dispatch_kernel(dest_ref, h_ref, xs_ref, buf, sem):
    rows = h_ref.shape[0]
    tm = rows // ROW_CHUNKS
    i = pl.program_id(0)
    n = pl.num_programs(0)
    slot = i % 2

    def wait_slot(s):
        for _ in range(TOP_K):
            pltpu.make_async_copy(buf.at[s], xs_ref.at[pl.ds(0, rows)], sem.at[s]).wait()

    @pl.when(i >= 2)
    def _():
        wait_slot(slot)

    buf[slot] = h_ref[...]

    def body(t, carry):
        src = buf.at[slot, pl.ds(pl.multiple_of(t * ROW_CHUNKS, ROW_CHUNKS), ROW_CHUNKS)]
        for j in range(TOP_K):
            d = pl.multiple_of(dest_ref[t * TOP_K + j], ROW_CHUNKS)
            pltpu.make_async_copy(src, xs_ref.at[pl.ds(d, ROW_CHUNKS)], sem.at[slot]).start(priority=j % 2)
        return carry

    lax.fori_loop(0, tm, body, 0)

    @pl.when(i == n - 1)
    def _():
        wait_slot(slot)

    @pl.when(jnp.logical_and(i == n - 1, n >= 2))
    def _():
        wait_slot(1 - slot)


def _dispatch(dest_rows, h2p, tm):
    n_rows = h2p.shape[0]
    n = n_rows // (ROW_CHUNKS * tm)
    return pl.pallas_call(
        _dispatch_kernel,
        grid=(n,),
        in_specs=[pl.BlockSpec((tm * TOP_K,), lambda i: (i,), memory_space=pltpu.SMEM),
                  pl.BlockSpec((ROW_CHUNKS * tm, LANES), lambda i: (i, 0))],
        out_specs=pl.BlockSpec(memory_space=pl.ANY),
        out_shape=jax.ShapeDtypeStruct((n_rows * TOP_K, LANES), U32),
        scratch_shapes=[pltpu.VMEM((2, ROW_CHUNKS * tm, LANES), U32), pltpu.SemaphoreType.DMA((2,))],
        compiler_params=_cparams(("arbitrary",)),
        name="moe_dispatch",
    )(dest_rows, h2p)


def _expert_kernel(iblk_ref, iexp_ref, ilo_ref, ihi_ref, xs_ref, w1_ref, w3_ref, w2_ref, ys_ref,
                   acc, hm, w1b, w3b, w2b):
    r = xs_ref.shape[0] // ROW_CHUNKS
    i = pl.program_id(0)
    n_items = pl.num_programs(0) - 1
    cur = jnp.minimum(i, n_items - 1)
    cur_prev = jnp.maximum(cur - 1, 0)
    dn = jnp.maximum(i - 1, 0)
    dn_prev = jnp.maximum(i - 2, 0)
    blk = iblk_ref[cur]
    lo = ilo_ref[cur]
    hi = ihi_ref[cur]
    dn_first = jnp.logical_or(i <= 1, iblk_ref[dn_prev] != iblk_ref[dn])
    slot = i % 2

    @pl.when(i == 0)
    def _():
        acc[...] = jnp.zeros_like(acc)
        hm[...] = jnp.zeros_like(hm)

    @pl.when(jnp.logical_or(i == 0, iexp_ref[cur_prev] != iexp_ref[cur]))
    def _():
        w1b[...] = w1_ref[0].astype(BF16)
        w3b[...] = w3_ref[0].astype(BF16)

    @pl.when(jnp.logical_or(i <= 1, iexp_ref[dn_prev] != iexp_ref[dn]))
    def _():
        w2b[...] = w2_ref[0].astype(BF16)

    n_half = 2
    rh = r // n_half

    def run(hm_rd, hm_wr):
        for hf in range(n_half):
            sl = slice(hf * rh, (hf + 1) * rh)
            y = jnp.dot(hm_rd[sl, :], w2b[...], preferred_element_type=F32)
            y = jnp.where(dn_first, y, acc[sl, :] + y)
            acc[sl, :] = y
            _store_token_rows(ys_ref, hf * rh, rh, y)

            rows = blk * r + hf * rh + lax.broadcasted_iota(jnp.int32, (rh, 1), 0)
            valid = jnp.logical_and(rows >= lo, rows < hi)
            x = _rows_to_features(_load_token_rows(xs_ref, hf * rh, rh))
            x = jnp.where(valid, x, 0.0).astype(BF16)
            a = jnp.dot(x, w1b[...], preferred_element_type=F32)
            bb = jnp.dot(x, w3b[...], preferred_element_type=F32)
            hm_wr[sl, :] = (a * jax.nn.sigmoid(a) * bb).astype(BF16)

    @pl.when(slot == 0)
    def _():
        run(hm.at[1], hm.at[0])

    @pl.when(slot == 1)
    def _():
        run(hm.at[0], hm.at[1])


def _experts(iblk, iexp, ilo, ihi, xs, w1, w3, w2, r):
    na, kw = xs.shape
    rows = r * ROW_CHUNKS
    n_items = iblk.shape[0]
    d = w1.shape[1]
    de = w1.shape[2]
    up = lambda i: jnp.minimum(i, n_items - 1)
    down = lambda i: jnp.maximum(i - 1, 0)
    grid_spec = pltpu.PrefetchScalarGridSpec(
        num_scalar_prefetch=4,
        grid=(n_items + 1,),
        in_specs=[pl.BlockSpec((rows, kw), lambda i, ib, ie, il, ih: (ib[up(i)], 0)),
                  pl.BlockSpec((1, d, de), lambda i, ib, ie, il, ih: (ie[up(i)], 0, 0)),
                  pl.BlockSpec((1, d, de), lambda i, ib, ie, il, ih: (ie[up(i)], 0, 0)),
                  pl.BlockSpec((1, de, d), lambda i, ib, ie, il, ih: (ie[down(i)], 0, 0))],
        out_specs=pl.BlockSpec((rows, kw), lambda i, ib, ie, il, ih: (ib[down(i)], 0)),
        scratch_shapes=[pltpu.VMEM((r, d), F32), pltpu.VMEM((2, r, de), BF16),
                        pltpu.VMEM((d, de), BF16), pltpu.VMEM((d, de), BF16), pltpu.VMEM((de, d), BF16)],
    )
    return pl.pallas_call(
        _expert_kernel,
        grid_spec=grid_spec,
        out_shape=jax.ShapeDtypeStruct((na, kw), U32),
        compiler_params=_cparams(("arbitrary",)),
        name="moe_experts",
    )(iblk, iexp, ilo, ihi, xs, w1, w3, w2)


def _combine_kernel(dest_ref, dnext_ref, x1_ref, h2p_ref, wts_ref, g2_ref, sw1_ref, sw3_ref, sw2_ref,
                    fg_ref, ys_ref, o_ref, gbuf, sem):
    tm = x1_ref.shape[0]
    i = pl.program_id(0)
    n = pl.num_programs(0)
    slot = i % 2

    rows = ROW_CHUNKS * tm

    def gather_row(dref, s, t, off, j):
        d = pl.multiple_of(dref[t * TOP_K + j], ROW_CHUNKS)
        pltpu.make_async_copy(ys_ref.at[pl.ds(d, ROW_CHUNKS)], gbuf.at[s, j, pl.ds(off, ROW_CHUNKS)],
                              sem.at[s]).start(priority=j % 2)

    def wait_slot(s):
        for _ in range(TOP_K):
            pltpu.make_async_copy(ys_ref.at[pl.ds(0, rows)], gbuf.at[s, 0], sem.at[s]).wait()

    @pl.when(i == 0)
    def _():
        def body(t, carry):
            for j in range(TOP_K):
                gather_row(dest_ref, 0, t, pl.multiple_of(t * ROW_CHUNKS, ROW_CHUNKS), j)
            return carry
        lax.fori_loop(0, tm, body, 0)

    wait_slot(slot)
    for t in range(tm):
        for j in range(TOP_K):
            gather_row(dnext_ref, 1 - slot, t, t * ROW_CHUNKS, j)

    wts = wts_ref[...]
    moe = None
    for j in range(TOP_K):
        wj = wts[:, j:j + 1]
        yj = [(lo * wj, hi * wj) for lo, hi in _load_token_rows(gbuf.at[slot, j], 0, tm)]
        moe = yj if moe is None else [(a + lo, b + hi) for (a, b), (lo, hi) in zip(moe, yj)]
    h = _rows_to_features(_load_token_rows(h2p_ref, 0, tm)).astype(BF16)
    a = jnp.dot(h, sw1_ref[...], preferred_element_type=F32)
    bb = jnp.dot(h, sw3_ref[...], preferred_element_type=F32)
    shared = jnp.dot((a * jax.nn.sigmoid(a) * bb).astype(BF16), sw2_ref[...], preferred_element_type=F32)
    ffn = _rows_to_features(moe) + shared
    xo = x1_ref[...] + g2_ref[0] * ffn
    o_ref[...] = _rmsnorm_rows(xo, fg_ref[...])

    @pl.when(i == n - 1)
    def _():
        wait_slot(1 - slot)


def _combine(dest_rows, x1, h2p, wts, g2, sw1, sw3, sw2, fg, ys, tiles_per_batch, tm):
    t, d = x1.shape
    n = t // tm
    de = sw1.shape[1]
    const2 = lambda shape: pl.BlockSpec(shape, lambda i: (0, 0))
    return pl.pallas_call(
        _combine_kernel,
        grid=(n,),
        in_specs=[pl.BlockSpec((tm * TOP_K,), lambda i: (i,), memory_space=pltpu.SMEM),
                  pl.BlockSpec((tm * TOP_K,), lambda i: (jnp.minimum(i + 1, n - 1),),
                               memory_space=pltpu.SMEM),
                  pl.BlockSpec((tm, d), lambda i: (i, 0)),
                  pl.BlockSpec((ROW_CHUNKS * tm, LANES), lambda i: (i, 0)),
                  pl.BlockSpec((tm, TOP_K), lambda i: (i, 0)),
                  pl.BlockSpec((1, 1, d), lambda i: (i // tiles_per_batch, 0, 0)),
                  const2((d, de)), const2((d, de)), const2((de, d)), const2((1, d)),
                  pl.BlockSpec(memory_space=pl.ANY)],
        out_specs=pl.BlockSpec((tm, d), lambda i: (i, 0)),
        out_shape=jax.ShapeDtypeStruct((t, d), F32),
        scratch_shapes=[pltpu.VMEM((2, TOP_K, ROW_CHUNKS * tm, LANES), U32),
                        pltpu.SemaphoreType.DMA((2,))],
        compiler_params=_cparams(("arbitrary",)),
        name="moe_combine",
    )(dest_rows, dest_rows, x1, h2p, wts, g2, sw1, sw3, sw2, fg, ys)


def _route_meta(counts, slot_e, slot_r, block_rows):
    t = slot_e.shape[1]
    ends = jnp.cumsum(counts)
    starts = ends - counts
    experts = jnp.arange(N_EXPERTS, dtype=jnp.int32)
    dest_t = slot_r + jnp.sum(jnp.where(slot_e[:, :, None] == experts, starts, 0), axis=-1)

    n_blk = t * TOP_K // block_rows
    n_items = n_blk + N_EXPERTS - 1
    first_blk = starts // block_rows
    last_blk = (ends - 1) // block_rows
    n_e = jnp.where(counts > 0, last_blk - first_blk + 1, 0)
    item_end = jnp.cumsum(n_e)
    item_off = item_end - n_e
    total = item_end[-1]
    idx = jnp.arange(n_items, dtype=jnp.int32)
    e_of = jnp.minimum(jnp.sum(item_end[None, :] <= idx[:, None], axis=1), N_EXPERTS - 1).astype(jnp.int32)
    onehot = e_of[:, None] == experts[None, :]
    lut = lambda table: jnp.sum(jnp.where(onehot, table[None, :], 0), axis=1)
    live = idx < total
    blk = lut(first_blk) + (idx - lut(item_off))
    lo = jnp.maximum(lut(starts), blk * block_rows)
    hi = jnp.minimum(lut(ends), (blk + 1) * block_rows)
    last_e = jnp.sum(jnp.where(idx == total - 1, e_of, 0))
    iblk = jnp.where(live, blk, n_blk - 1).astype(jnp.int32)
    iexp = jnp.where(live, e_of, last_e).astype(jnp.int32)
    ilo = jnp.where(live, lo, 0).astype(jnp.int32)
    ihi = jnp.where(live, hi, 0).astype(jnp.int32)
    return dest_t, iblk, iexp, ilo, ihi


def _rope_tables(n_tokens):
    n_rows = n_tokens // GRID_W
    rows = np.repeat(np.arange(n_rows, dtype=np.float32), GRID_W)
    cols = np.tile(np.arange(GRID_W, dtype=np.float32), n_rows)
    n_freq = HEAD_DIM // 4
    inv_freq = (np.float32(ROPE_BASE) ** (-np.arange(n_freq, dtype=np.float32) / n_freq)).astype(np.float32)
    ar = rows[:, None] * inv_freq
    ac = cols[:, None] * inv_freq
    cos64 = np.concatenate([np.cos(ar), np.cos(ar), np.cos(ac), np.cos(ac)], axis=1)
    sin64 = np.concatenate([-np.sin(ar), np.sin(ar), -np.sin(ac), np.sin(ac)], axis=1)
    return np.tile(cos64, (1, 2)).astype(np.float32), np.tile(sin64, (1, 2)).astype(np.float32)


def kernel(x, c, ctx, c_ctx, ada_w, ada_b, norm1_g, w_in, lambda_q1, lambda_k1, lambda_q2, lambda_k2,
           dattn_norm_g, ret_decay_fwd, ret_decay_bwd, ret_norm_g, w_out, norm2_g, router_w, router_bias,
           exp_w1, exp_w3, exp_w2, shared_w1, shared_w3, shared_w2, final_norm_g):
    b, s, d = x.shape
    lc = ctx.shape[1]
    depth = ada_w.shape[0]
    assert depth == 1, "single-layer problem: the context stream is never updated"
    l = 0
    lam_init = 0.8 - 0.6 * math.exp(-0.3 * l)

    c8 = jnp.zeros((8, d), F32).at[:b].set(c).at[b].set(c_ctx)
    mod = _modulation(c8, ada_w[l], ada_b[l])
    sh1, sc1, g1, sh2, sc2, g2 = [mod[:b, k * d:(k + 1) * d].reshape(b, 1, d) for k in range(6)]
    csh1 = jnp.broadcast_to(mod[b, 0:d].reshape(1, 1, d), (b, 1, d))
    csc1 = jnp.broadcast_to(mod[b, d:2 * d].reshape(1, 1, d), (b, 1, d))

    w_l = w_in[l]
    w_tok = jnp.concatenate([w_l[:, 512:1024], w_l[:, 1536:3072]], axis=1).astype(BF16)
    w_feat = jnp.concatenate([w_l[:, 0:512], w_l[:, 1024:1536]], axis=1).T.astype(BF16)
    n1g = norm1_g[l].reshape(1, d)
    cos, sin = _rope_tables(s)
    ones_c = jnp.ones((lc, LANES), F32)
    zeros_c = jnp.zeros((lc, LANES), F32)
    px, qt, vtx = _inproj(x, n1g, sc1, sh1, cos, sin, np.ascontiguousarray(cos.T),
                          np.ascontiguousarray(sin.T), w_tok, w_feat, tm=512)
    pc, _, vtc = _inproj(ctx, n1g, csc1, csh1, ones_c, zeros_c, ones_c.T, zeros_c.T, w_tok, w_feat, tm=lc)

    lam = (jnp.exp(jnp.sum(lambda_q1[l] * lambda_k1[l])) - jnp.exp(jnp.sum(lambda_q2[l] * lambda_k2[l]))
           ).astype(F32) + lam_init
    att = _attention(lam.reshape(1), qt, px, vtx, pc, vtc, tq=min(2048, s), tkc=min(512, s // 4))

    dmat, qdec, kdec, gch = _retention_tables(ret_decay_fwd[l], ret_decay_bwd[l])
    ret_f, ret_b = _retention(px, pc, dmat, qdec, kdec, gch)

    x1, h2p, slot_e, slot_w, slot_r, counts = _merge(
        att, ret_f, ret_b, px, x, g1, sh2, sc2,
        dattn_norm_g[l].reshape(1, 2 * HEAD_DIM), ret_norm_g[l].reshape(1, R_DV),
        w_out[l].astype(BF16), norm2_g[l].reshape(1, d), router_w[l].T.astype(BF16),
        router_bias[l].astype(F32).reshape(N_EXPERTS, 1), 1.0 - lam_init, tm=512)

    t = b * s
    block_rows = 512
    dest_t, iblk, iexp, ilo, ihi = _route_meta(counts[:, 0].astype(jnp.int32), slot_e, slot_r, block_rows)
    dest_rows = (dest_t.T * ROW_CHUNKS).reshape(t * TOP_K)
    xs = _dispatch(dest_rows, h2p, tm=128)
    ys = _experts(iblk, iexp, ilo, ihi, xs, exp_w1[l], exp_w3[l], exp_w2[l], block_rows)
    tm_c = 128
    out = _combine(dest_rows, x1.reshape(t, d), h2p, slot_w.T, g2, shared_w1[l].astype(BF16),
                   shared_w3[l].astype(BF16), shared_w2[l].astype(BF16), final_norm_g.reshape(1, d),
                   ys, s // tm_c, tm_c)
    return out.reshape(b, s, d)
```

```python
import functools
import math

import jax
import jax.numpy as jnp
import numpy as np
from jax import lax
from jax.experimental import pallas as pl
from jax.experimental.pallas import tpu as pltpu

F32 = jnp.float32
BF16 = jnp.bfloat16
U32 = jnp.uint32

D_MODEL = 1024
GRID_W = 64
HEAD_DIM = 64
A_HEADS = 4
R_HEADS = 4
R_DK = 64
R_DV = 128
RET_CHUNK = 128
ROPE_BASE = 10000.0
N_EXPERTS = 64
N_GROUPS = 8
TOPK_GROUPS = 4
TOP_K = 8
D_EXPERT = 256
ROUTED_SCALE = 2.5
NORM_EPS = 1e-6
LANES = 128

P_KZ = 0
P_RQ = 1024
P_RK = 1280
P_RV = 1536
P_RG = 2048
P_COLS = 2560
VT_ROWS = 144
LOG2E = 1.4426950408889634

VMEM_LIMIT = 56 * 1024 * 1024


def _cparams(sem):
    return pltpu.CompilerParams(dimension_semantics=sem, vmem_limit_bytes=VMEM_LIMIT)


def _mod_kernel(c_ref, w_ref, b_ref, o_ref):
    c = c_ref[...]
    s = c * jax.nn.sigmoid(c)
    o_ref[...] = jnp.dot(s, w_ref[...], preferred_element_type=F32,
                         precision=lax.Precision.HIGHEST) + b_ref[...]


def _modulation(c8, ada_w, ada_b):
    d = c8.shape[1]
    n = ada_w.shape[1]
    tn = 512
    return pl.pallas_call(
        _mod_kernel,
        grid=(n // tn,),
        in_specs=[pl.BlockSpec((8, d), lambda j: (0, 0)),
                  pl.BlockSpec((d, tn), lambda j: (0, j)),
                  pl.BlockSpec((1, tn), lambda j: (0, j))],
        out_specs=pl.BlockSpec((8, tn), lambda j: (0, j)),
        out_shape=jax.ShapeDtypeStruct((8, n), F32),
        compiler_params=_cparams(("arbitrary",)),
        name="modulation",
    )(c8, ada_w, ada_b.reshape(1, n))


def _rmsnorm_rows(x, g):
    return x * lax.rsqrt(jnp.mean(x * x, axis=-1, keepdims=True) + NORM_EPS) * g


def _inproj_kernel(x_ref, g_ref, sc_ref, sh_ref, cos_ref, sin_ref, cost_ref, sint_ref, w_ref, wt_ref,
                   p_ref, qt_ref, vt_ref):
    x = x_ref[0]
    tm = x.shape[0]
    h = _rmsnorm_rows(x, g_ref[...]) * (1.0 + sc_ref[0]) + sh_ref[0]
    hb = h.astype(BF16)
    p = jnp.dot(hb, w_ref[...], preferred_element_type=F32)
    pt = lax.dot_general(wt_ref[...], hb, _NT, preferred_element_type=F32)

    cos = cos_ref[...]
    sin = sin_ref[...]
    lane = lax.broadcasted_iota(jnp.int32, (tm, LANES), 1)
    first_half = (lane % 32) < 16
    low64 = lane < 64

    def rope(xs):
        partner = jnp.where(first_half, pltpu.roll(xs, LANES - 16, 1), pltpu.roll(xs, 16, 1))
        return xs * cos + partner * sin

    zero = jnp.zeros((tm, LANES), BF16)
    for hh in range(A_HEADS):
        k = rope(p[:, hh * 128:(hh + 1) * 128]).astype(BF16)
        p_ref[0, :, P_KZ + hh * 256:P_KZ + hh * 256 + 128] = jnp.where(low64, k, zero)
        p_ref[0, :, P_KZ + hh * 256 + 128:P_KZ + (hh + 1) * 256] = jnp.where(low64, zero, k)
    for cc in range(2):
        rq = rope(p[:, 512 + cc * 128:512 + (cc + 1) * 128])
        p_ref[0, :, P_RQ + cc * 128:P_RQ + (cc + 1) * 128] = rq.astype(BF16)
        rk = rope(p[:, 768 + cc * 128:768 + (cc + 1) * 128]) * (R_DK ** -0.5)
        p_ref[0, :, P_RK + cc * 128:P_RK + (cc + 1) * 128] = rk.astype(BF16)
    p_ref[0, :, P_RV:P_RV + 512] = p[:, 1024:1536].astype(BF16)
    p_ref[0, :, P_RG:P_RG + 512] = p[:, 1536:2048].astype(BF16)

    cost = cost_ref[...]
    sint = sint_ref[...]
    q_scale = (HEAD_DIM ** -0.5) * LOG2E
    ones_rows = jnp.where(lax.broadcasted_iota(jnp.int32, (VT_ROWS - 128, tm), 0) == 0, 1.0, 0.0)
    for hh in range(A_HEADS):
        xq = pt[hh * 128:(hh + 1) * 128, :]
        groups = [xq[16 * gi:16 * (gi + 1), :] for gi in range(8)]
        partner = jnp.concatenate([groups[gi ^ 1] for gi in range(8)], axis=0)
        qt_ref[0, hh] = ((xq * cost + partner * sint) * q_scale).astype(BF16)
        vt_ref[0, hh, 0:128, :] = pt[512 + hh * 128:512 + (hh + 1) * 128, :].astype(BF16)
        vt_ref[0, hh, 128:VT_ROWS, :] = ones_rows.astype(BF16)


def _inproj(x, g, sc, sh, cos, sin, cost, sint, w_bf, wt_bf, tm):
    b, s, d = x.shape
    n_tok = w_bf.shape[1]
    n_t = wt_bf.shape[0]
    const2 = lambda shape: pl.BlockSpec(shape, lambda bi, i: (0, 0))
    per_b = pl.BlockSpec((1, 1, d), lambda bi, i: (bi, 0, 0))
    return pl.pallas_call(
        _inproj_kernel,
        grid=(b, s // tm),
        in_specs=[pl.BlockSpec((1, tm, d), lambda bi, i: (bi, i, 0)),
                  const2((1, d)), per_b, per_b,
                  pl.BlockSpec((tm, LANES), lambda bi, i: (i, 0)),
                  pl.BlockSpec((tm, LANES), lambda bi, i: (i, 0)),
                  pl.BlockSpec((LANES, tm), lambda bi, i: (0, i)),
                  pl.BlockSpec((LANES, tm), lambda bi, i: (0, i)),
                  const2((d, n_tok)), const2((n_t, d))],
        out_specs=[pl.BlockSpec((1, tm, P_COLS), lambda bi, i: (bi, i, 0)),
                   pl.BlockSpec((1, A_HEADS, LANES, tm), lambda bi, i: (bi, 0, 0, i)),
                   pl.BlockSpec((1, A_HEADS, VT_ROWS, tm), lambda bi, i: (bi, 0, 0, i))],
        out_shape=[jax.ShapeDtypeStruct((b, s, P_COLS), BF16),
                   jax.ShapeDtypeStruct((b, A_HEADS, LANES, s), BF16),
                   jax.ShapeDtypeStruct((b, A_HEADS, VT_ROWS, s), BF16)],
        compiler_params=_cparams(("arbitrary", "arbitrary")),
        name="inproj",
    )(x, g, sc, sh, cos, sin, cost, sint, w_bf, wt_bf)


_NT = (((1,), (1,)), ((), ()))


def _attn_kernel(lam_ref, qt_ref, kzc_ref, vtc_ref, kzx_ref, vtx_ref, o_ref,
                 acc, s_a, s_b, e_a, e_b, *, tkc):
    qt = qt_ref[0, 0]
    tq = qt.shape[1]
    n_chunks = kzx_ref.shape[1] // tkc
    assert n_chunks % 2 == 0 and n_chunks >= 4

    def scores(kz, s_ref):
        for mp in range(2):
            s_ref[mp] = jnp.dot(kz[:, mp * LANES:(mp + 1) * LANES], qt, preferred_element_type=F32)

    def softmax(s_ref, e_ref, ms):
        new_ms, alphas = [], []
        for mp in range(2):
            st = s_ref[mp]
            m_new = jnp.maximum(ms[mp], jnp.max(st, axis=0, keepdims=True))
            alphas.append(jnp.exp2(ms[mp] - m_new))
            e_ref[mp] = jnp.exp2(st - m_new).astype(BF16)
            new_ms.append(m_new)
        return tuple(new_ms), tuple(alphas)

    def values(vt, e_ref, alphas):
        for mp in range(2):
            acc[mp] = alphas[mp] * acc[mp] + jnp.dot(vt, e_ref[mp], preferred_element_type=F32)

    def kchunk(c):
        return kzx_ref[0, pl.ds(pl.multiple_of(c * tkc, tkc), tkc), :]

    def vchunk(c):
        return vtx_ref[0, 0, :, pl.ds(pl.multiple_of(c * tkc, tkc), tkc)]

    ms = []
    for mp in range(2):
        st = jnp.dot(kzc_ref[0, :, mp * LANES:(mp + 1) * LANES], qt, preferred_element_type=F32)
        m_new = jnp.max(st, axis=0, keepdims=True)
        acc[mp] = jnp.dot(vtc_ref[0, 0], jnp.exp2(st - m_new).astype(BF16), preferred_element_type=F32)
        ms.append(m_new)
    ms = tuple(ms)

    scores(kchunk(0), s_a)
    scores(kchunk(1), s_b)
    ms, al_a = softmax(s_a, e_a, ms)

    def body(j, carry):
        ms, al_a = carry
        scores(kchunk(2 * j + 2), s_a)
        ms, al_b = softmax(s_b, e_b, ms)
        values(vchunk(2 * j), e_a, al_a)
        scores(kchunk(2 * j + 3), s_b)
        ms, al_a = softmax(s_a, e_a, ms)
        values(vchunk(2 * j + 1), e_b, al_b)
        return ms, al_a

    ms, al_a = lax.fori_loop(0, n_chunks // 2 - 1, body, (ms, al_a))
    ms, al_b = softmax(s_b, e_b, ms)
    values(vchunk(n_chunks - 2), e_a, al_a)
    values(vchunk(n_chunks - 1), e_b, al_b)

    a1 = acc[0]
    a2 = acc[1]
    o1 = a1[:LANES, :] / a1[LANES:LANES + 1, :]
    o2 = a2[:LANES, :] / a2[LANES:LANES + 1, :]
    o_ref[0] = (o1 - lam_ref[0] * o2).T.astype(BF16)


def _attention(lam, qt, px, vtx, pc, vtc, tq, tkc):
    b, s, _ = px.shape
    lc = pc.shape[1]
    kern = functools.partial(_attn_kernel, tkc=tkc)
    return pl.pallas_call(
        kern,
        grid=(b, A_HEADS, s // tq),
        in_specs=[pl.BlockSpec(memory_space=pltpu.SMEM),
                  pl.BlockSpec((1, 1, LANES, tq), lambda bi, h, i: (bi, h, 0, i)),
                  pl.BlockSpec((1, lc, 256), lambda bi, h, i: (bi, 0, P_KZ // 256 + h)),
                  pl.BlockSpec((1, 1, VT_ROWS, lc), lambda bi, h, i: (bi, h, 0, 0)),
                  pl.BlockSpec((1, s, 256), lambda bi, h, i: (bi, 0, P_KZ // 256 + h)),
                  pl.BlockSpec((1, 1, VT_ROWS, s), lambda bi, h, i: (bi, h, 0, 0))],
        out_specs=pl.BlockSpec((1, tq, LANES), lambda bi, h, i: (bi, i, h)),
        out_shape=jax.ShapeDtypeStruct((b, s, A_HEADS * LANES), BF16),
        scratch_shapes=[pltpu.VMEM((2, VT_ROWS, tq), F32),
                        pltpu.VMEM((2, tkc, tq), F32), pltpu.VMEM((2, tkc, tq), F32),
                        pltpu.VMEM((2, tkc, tq), BF16), pltpu.VMEM((2, tkc, tq), BF16)],
        compiler_params=_cparams(("arbitrary", "arbitrary", "arbitrary")),
        name="diff_attention",
    )(lam, qt, pc, vtc, px, vtx)


_TN = (((0,), (0,)), ((), ()))


def _ret_kernel(qf_ref, kf_ref, vf_ref, qb_ref, kb_ref, vb_ref, kc_ref, vc_ref,
                dmat_ref, qdec_ref, kdec_ref, gch_ref, of_ref, ob_ref, sf, sb):
    c = RET_CHUNK
    n = pl.program_id(1)
    col_head = lax.broadcasted_iota(jnp.int32, (c, R_HEADS * R_DK), 1) // R_DK
    bd_mask = (lax.broadcasted_iota(jnp.int32, (R_HEADS * R_DK, R_HEADS * R_DV), 0) // R_DK
               == lax.broadcasted_iota(jnp.int32, (R_HEADS * R_DK, R_HEADS * R_DV), 1) // R_DV)

    def state_update(s_ref, k, v, kdec, gch):
        kd = (k.astype(F32) * kdec).astype(BF16)
        kv = lax.dot_general(kd, v, _TN, preferred_element_type=F32)
        s_ref[...] = gch * s_ref[...] + jnp.where(bd_mask, kv, 0.0)

    def chunk(s_ref, q, k, v, dmat, qdec, kdec, gch):
        zero = jnp.zeros_like(k)
        kblk = jnp.concatenate([jnp.where(col_head == h, k, zero) for h in range(R_HEADS)], axis=0)
        sc = lax.dot_general(q, kblk, _NT, preferred_element_type=F32) * dmat
        p = sc.astype(BF16)
        intra = jnp.concatenate(
            [jnp.dot(p[:, h * c:(h + 1) * c], v[:, h * R_DV:(h + 1) * R_DV], preferred_element_type=F32)
             for h in range(R_HEADS)], axis=1)
        qd = (q.astype(F32) * qdec).astype(BF16)
        cross = jnp.dot(qd, s_ref[...].astype(BF16), preferred_element_type=F32)
        out = intra + cross
        state_update(s_ref, k, v, kdec, gch)
        return out

    @pl.when(n == 0)
    def _():
        sf[...] = jnp.zeros_like(sf)
        sb[...] = jnp.zeros_like(sb)
        n_ctx = kc_ref.shape[1] // c
        for cc in range(n_ctx):
            state_update(sf, kc_ref[0, cc * c:(cc + 1) * c, :], vc_ref[0, cc * c:(cc + 1) * c, :],
                         kdec_ref[0], gch_ref[0])
        for cc in reversed(range(n_ctx)):
            state_update(sb, kc_ref[0, cc * c:(cc + 1) * c, :], vc_ref[0, cc * c:(cc + 1) * c, :],
                         kdec_ref[1], gch_ref[1])

    chunks_per_step = qf_ref.shape[1] // c
    for cc in range(chunks_per_step):
        rows = slice(cc * c, (cc + 1) * c)
        of_ref[0, rows, :] = chunk(sf, qf_ref[0, rows, :], kf_ref[0, rows, :], vf_ref[0, rows, :], dmat_ref[0],
                                   qdec_ref[0], kdec_ref[0], gch_ref[0]).astype(BF16)
    for cc in reversed(range(chunks_per_step)):
        rows = slice(cc * c, (cc + 1) * c)
        ob_ref[0, rows, :] = chunk(sb, qb_ref[0, rows, :], kb_ref[0, rows, :], vb_ref[0, rows, :], dmat_ref[1],
                                   qdec_ref[1], kdec_ref[1], gch_ref[1]).astype(BF16)


def _retention(px, pc, dmat, qdec, kdec, gch, chunks_per_step):
    b, s, _ = px.shape
    lc = pc.shape[1]
    c = RET_CHUNK
    rows = c * chunks_per_step
    nch = s // rows
    qk_w = R_HEADS * R_DK
    v_w = R_HEADS * R_DV
    fwd = lambda bi, n: (bi, n, 0)
    spec_q = lambda im: pl.BlockSpec((1, rows, qk_w), lambda bi, n: im(bi, n)[:2] + (P_RQ // qk_w,))
    spec_k = lambda im: pl.BlockSpec((1, rows, qk_w), lambda bi, n: im(bi, n)[:2] + (P_RK // qk_w,))
    spec_v = lambda im: pl.BlockSpec((1, rows, v_w), lambda bi, n: im(bi, n)[:2] + (P_RV // v_w,))
    bwd = lambda bi, n: (bi, nch - 1 - n, 0)
    const3 = lambda shape: pl.BlockSpec(shape, lambda bi, n: (0, 0, 0))
    return pl.pallas_call(
        _ret_kernel,
        grid=(b, nch),
        in_specs=[spec_q(fwd), spec_k(fwd), spec_v(fwd), spec_q(bwd), spec_k(bwd), spec_v(bwd),
                  pl.BlockSpec((1, lc, qk_w), lambda bi, n: (bi, 0, P_RK // qk_w)),
                  pl.BlockSpec((1, lc, v_w), lambda bi, n: (bi, 0, P_RV // v_w)),
                  const3((2, c, R_HEADS * c)), const3((2, c, qk_w)), const3((2, c, qk_w)),
                  const3((2, 1, v_w))],
        out_specs=[pl.BlockSpec((1, rows, v_w), fwd), pl.BlockSpec((1, rows, v_w), bwd)],
        out_shape=[jax.ShapeDtypeStruct((b, s, v_w), BF16), jax.ShapeDtypeStruct((b, s, v_w), BF16)],
        scratch_shapes=[pltpu.VMEM((qk_w, v_w), F32), pltpu.VMEM((qk_w, v_w), F32)],
        compiler_params=_cparams(("arbitrary", "arbitrary")),
        name="retention",
    )(px, px, px, px, px, px, pc, pc, dmat, qdec, kdec, gch)


def _retention_tables(decay_fwd, decay_bwd):
    c = RET_CHUNK
    pos = jnp.arange(c, dtype=F32)
    rel = pos[:, None] - pos[None, :]

    def tables(logits, backward):
        lg = jax.nn.log_sigmoid(logits.astype(F32))
        r = -rel if backward else rel
        dm = jnp.where(r[None] >= 0, jnp.exp(jnp.maximum(r, 0.0)[None] * lg[:, None, None]), 0.0)
        dm = jnp.transpose(dm, (1, 0, 2)).reshape(c, R_HEADS * c)
        qpow = (c - pos) if backward else (pos + 1)
        kpow = pos if backward else (c - 1 - pos)
        qd = jnp.repeat(jnp.exp(qpow[:, None] * lg[None, :]), R_DK, axis=1)
        kd = jnp.repeat(jnp.exp(kpow[:, None] * lg[None, :]), R_DK, axis=1)
        gc = jnp.repeat(jnp.exp(c * lg), R_DV)[None, :]
        return dm, qd, kd, gc

    tf = tables(decay_fwd, False)
    tb = tables(decay_bwd, True)
    return tuple(jnp.stack([a, bb]) for a, bb in zip(tf, tb))


def _pack_pairs(x):
    k = x.shape[1] // 2
    lo = lax.bitcast_convert_type(x[:, :k].astype(BF16).astype(F32), U32)
    hi = lax.bitcast_convert_type(x[:, k:].astype(BF16).astype(F32), U32)
    return (lo >> 16) | (hi & jnp.uint32(0xFFFF0000))


def _unpack_pairs(w):
    lo = lax.bitcast_convert_type(w << 16, F32)
    hi = lax.bitcast_convert_type(w & jnp.uint32(0xFFFF0000), F32)
    return lo, hi


ROW_CHUNKS = (D_MODEL // 2) // LANES


def _store_token_rows(ref, first_token, n_tokens, x):
    packed = _pack_pairs(x)
    for c in range(ROW_CHUNKS):
        ref[pl.ds(first_token * ROW_CHUNKS + c, n_tokens, stride=ROW_CHUNKS), :] = (
            packed[:, c * LANES:(c + 1) * LANES])


def _load_token_rows(ref, first_token, n_tokens):
    return [_unpack_pairs(ref[pl.ds(first_token * ROW_CHUNKS + c, n_tokens, stride=ROW_CHUNKS), :])
            for c in range(ROW_CHUNKS)]


def _rows_to_features(chunks):
    return jnp.concatenate([lo for lo, _ in chunks] + [hi for _, hi in chunks], axis=1)


def _route_tile(logits_t, bias_col, carry_ref):
    n_e, tm = logits_t.shape
    per = n_e // N_GROUPS
    neg = -jnp.inf
    scores = jax.nn.sigmoid(logits_t)
    sel = scores + bias_col
    sub = lax.broadcasted_iota(jnp.int32, (per, tm), 0).astype(F32)
    sel_g = [sel[g * per:(g + 1) * per, :] for g in range(N_GROUPS)]
    flat_g = [sub + float(g * per) for g in range(N_GROUPS)]

    gs = []
    for g in range(N_GROUPS):
        top1 = jnp.max(sel_g[g], axis=0, keepdims=True)
        idx1 = jnp.min(jnp.where(sel_g[g] == top1, sub, float(per)), axis=0, keepdims=True)
        top2 = jnp.max(jnp.where(sub == idx1, neg, sel_g[g]), axis=0, keepdims=True)
        gs.append(top1 + top2)
    gsel = [jnp.zeros((1, tm), F32) for _ in range(N_GROUPS)]
    for _ in range(TOPK_GROUPS):
        m = functools.reduce(jnp.maximum, gs)
        found = jnp.zeros((1, tm), F32)
        for g in range(N_GROUPS):
            hit = jnp.logical_and(gs[g] == m, found == 0.0)
            found = jnp.where(hit, 1.0, found)
            gsel[g] = jnp.where(hit, 1.0, gsel[g])
            gs[g] = jnp.where(hit, neg, gs[g])

    masked = [jnp.where(gsel[g] > 0.0, sel_g[g], neg) for g in range(N_GROUPS)]
    chosen = [jnp.zeros((per, tm), F32) for _ in range(N_GROUPS)]
    for _ in range(TOP_K):
        m = functools.reduce(jnp.maximum, [jnp.max(mg, axis=0, keepdims=True) for mg in masked])
        idx = functools.reduce(jnp.minimum, [
            jnp.min(jnp.where(masked[g] == m, flat_g[g], float(n_e)), axis=0, keepdims=True)
            for g in range(N_GROUPS)])
        for g in range(N_GROUPS):
            hit = flat_g[g] == idx
            chosen[g] = jnp.where(hit, 1.0, chosen[g])
            masked[g] = jnp.where(hit, neg, masked[g])
    chosen = jnp.concatenate(chosen, axis=0)
    flat = jnp.concatenate(flat_g, axis=0)

    w_dense = scores * chosen
    w_dense = w_dense / jnp.sum(w_dense, axis=0, keepdims=True) * ROUTED_SCALE

    chosen_bf = chosen.astype(BF16)
    lower = (lax.broadcasted_iota(jnp.int32, (n_e, n_e), 1)
             < lax.broadcasted_iota(jnp.int32, (n_e, n_e), 0)).astype(BF16)
    slot_idx = jnp.dot(lower, chosen_bf, preferred_element_type=F32)
    upper = (lax.broadcasted_iota(jnp.int32, (tm, tm), 0)
             < lax.broadcasted_iota(jnp.int32, (tm, tm), 1)).astype(BF16)
    rank = jnp.dot(chosen_bf, upper, preferred_element_type=F32) + carry_ref[...]
    carry_ref[...] += jnp.sum(chosen, axis=1, keepdims=True)

    slots = []
    for j in range(TOP_K):
        ind = jnp.where(slot_idx == float(j), chosen, 0.0)
        slots.append((jnp.sum(ind * flat, axis=0, keepdims=True),
                      jnp.sum(ind * w_dense, axis=0, keepdims=True),
                      jnp.sum(ind * rank, axis=0, keepdims=True)))
    return slots


def _merge_kernel(att_ref, rf_ref, rb_ref, gate_ref, x_ref, g1_ref, sh2_ref, sc2_ref, ga_ref, gr_ref,
                  wout_ref, n2g_ref, rwt_ref, rbias_ref, x1_ref, h2p_ref, se_ref, sw_ref, sr_ref, cnt_ref,
                  carry, *, att_scale):
    @pl.when(jnp.logical_and(pl.program_id(0) == 0, pl.program_id(1) == 0))
    def _():
        carry[...] = jnp.zeros_like(carry)

    att = att_ref[0].astype(F32)
    ret = rf_ref[0].astype(F32) + rb_ref[0].astype(F32)
    gate = gate_ref[0].astype(F32)
    parts = []
    for h in range(A_HEADS):
        a = att[:, h * 128:(h + 1) * 128]
        parts.append(_rmsnorm_rows(a, ga_ref[...]) * att_scale)
    for h in range(R_HEADS):
        r = _rmsnorm_rows(ret[:, h * R_DV:(h + 1) * R_DV], gr_ref[...])
        g = gate[:, h * R_DV:(h + 1) * R_DV]
        parts.append(g * jax.nn.sigmoid(g) * r)
    merged = jnp.concatenate(parts, axis=1).astype(BF16)
    y = jnp.dot(merged, wout_ref[...], preferred_element_type=F32)
    x1 = x_ref[0] + g1_ref[0] * y
    x1_ref[0] = x1
    h2 = _rmsnorm_rows(x1, n2g_ref[...]) * (1.0 + sc2_ref[0]) + sh2_ref[0]
    _store_token_rows(h2p_ref, 0, h2.shape[0], h2)
    logits_t = lax.dot_general(rwt_ref[...], h2.astype(BF16), _NT, preferred_element_type=F32)
    slots = _route_tile(logits_t, rbias_ref[...], carry)
    for j, (e_j, w_j, r_j) in enumerate(slots):
        se_ref[j:j + 1, :] = e_j.astype(jnp.int32)
        sw_ref[j:j + 1, :] = w_j
        sr_ref[j:j + 1, :] = r_j.astype(jnp.int32)
    cnt_ref[...] = carry[...]


def _merge(att, rf, rb, px, x, g1, sh2, sc2, ga, gr, wout_bf, n2g, rwt_bf, rbias, att_scale, tm):
    b, s, d = x.shape
    nt = s // tm
    chunked = pl.BlockSpec((ROW_CHUNKS * tm, LANES), lambda bi, i: (bi * nt + i, 0))
    chunked_shape = jax.ShapeDtypeStruct((b * s * ROW_CHUNKS, LANES), U32)
    row = lambda bi, i: (bi, i, 0)
    tok = lambda bi, i: (0, bi * nt + i)
    per_b = pl.BlockSpec((1, 1, d), lambda bi, i: (bi, 0, 0))
    const2 = lambda shape: pl.BlockSpec(shape, lambda bi, i: (0, 0))
    return pl.pallas_call(
        functools.partial(_merge_kernel, att_scale=att_scale),
        grid=(b, nt),
        in_specs=[pl.BlockSpec((1, tm, 512), row), pl.BlockSpec((1, tm, 512), row),
                  pl.BlockSpec((1, tm, 512), row),
                  pl.BlockSpec((1, tm, 512), lambda bi, i: (bi, i, P_RG // 512)),
                  pl.BlockSpec((1, tm, d), row), per_b, per_b, per_b,
                  const2((1, 128)), const2((1, R_DV)), const2((d, d)), const2((1, d)),
                  const2((N_EXPERTS, d)), const2((N_EXPERTS, 1))],
        out_specs=[pl.BlockSpec((1, tm, d), row), chunked,
                   pl.BlockSpec((TOP_K, tm), tok), pl.BlockSpec((TOP_K, tm), tok),
                   pl.BlockSpec((TOP_K, tm), tok), const2((N_EXPERTS, 1))],
        out_shape=[jax.ShapeDtypeStruct((b, s, d), F32), chunked_shape,
                   jax.ShapeDtypeStruct((TOP_K, b * s), jnp.int32),
                   jax.ShapeDtypeStruct((TOP_K, b * s), F32),
                   jax.ShapeDtypeStruct((TOP_K, b * s), jnp.int32),
                   jax.ShapeDtypeStruct((N_EXPERTS, 1), F32)],
        scratch_shapes=[pltpu.VMEM((N_EXPERTS, 1), F32)],
        compiler_params=_cparams(("arbitrary", "arbitrary")),
        name="mixer_merge",
    )(att, rf, rb, px, x, g1, sh2, sc2, ga, gr, wout_bf, n2g, rwt_bf, rbias)


def _dispatch_kernel(dest_ref, h_ref, xs_ref, buf, sem):
    rows = h_ref.shape[0]
    tm = rows // ROW_CHUNKS
    i = pl.program_id(0)
    n = pl.num_programs(0)
    slot = i % 2

    def wait_slot(s):
        for _ in range(TOP_K):
            pltpu.make_async_copy(buf.at[s], xs_ref.at[pl.ds(0, rows)], sem.at[s]).wait()

    @pl.when(i >= 2)
    def _():
        wait_slot(slot)

    buf[slot] = h_ref[...]

    def body(t, carry):
        src = buf.at[slot, pl.ds(pl.multiple_of(t * ROW_CHUNKS, ROW_CHUNKS), ROW_CHUNKS)]
        for j in range(TOP_K):
            d = pl.multiple_of(dest_ref[t * TOP_K + j], ROW_CHUNKS)
            pltpu.make_async_copy(src, xs_ref.at[pl.ds(d, ROW_CHUNKS)], sem.at[slot]).start(priority=j % 2)
        return carry

    lax.fori_loop(0, tm, body, 0)

    @pl.when(i == n - 1)
    def _():
        wait_slot(slot)

    @pl.when(jnp.logical_and(i == n - 1, n >= 2))
    def _():
        wait_slot(1 - slot)


def _dispatch(dest_rows, h2p, tm):
    n_rows = h2p.shape[0]
    n = n_rows // (ROW_CHUNKS * tm)
    return pl.pallas_call(
        _dispatch_kernel,
        grid=(n,),
        in_specs=[pl.BlockSpec((tm * TOP_K,), lambda i: (i,), memory_space=pltpu.SMEM),
                  pl.BlockSpec((ROW_CHUNKS * tm, LANES), lambda i: (i, 0))],
        out_specs=pl.BlockSpec(memory_space=pl.ANY),
        out_shape=jax.ShapeDtypeStruct((n_rows * TOP_K, LANES), U32),
        scratch_shapes=[pltpu.VMEM((2, ROW_CHUNKS * tm, LANES), U32), pltpu.SemaphoreType.DMA((2,))],
        compiler_params=_cparams(("arbitrary",)),
        name="moe_dispatch",
    )(dest_rows, h2p)


def _expert_kernel(iblk_ref, iexp_ref, ilo_ref, ihi_ref, xs_ref, w1_ref, w3_ref, w2_ref, ys_ref,
                   acc, hm, w1b, w3b, w2b):
    r = xs_ref.shape[0] // ROW_CHUNKS
    i = pl.program_id(0)
    n_items = pl.num_programs(0) - 1
    cur = jnp.minimum(i, n_items - 1)
    cur_prev = jnp.maximum(cur - 1, 0)
    dn = jnp.maximum(i - 1, 0)
    dn_prev = jnp.maximum(i - 2, 0)
    blk = iblk_ref[cur]
    lo = ilo_ref[cur]
    hi = ihi_ref[cur]
    dn_first = jnp.logical_or(i <= 1, iblk_ref[dn_prev] != iblk_ref[dn])
    slot = i % 2

    @pl.when(i == 0)
    def _():
        acc[...] = jnp.zeros_like(acc)
        hm[...] = jnp.zeros_like(hm)

    @pl.when(jnp.logical_or(i == 0, iexp_ref[cur_prev] != iexp_ref[cur]))
    def _():
        w1b[...] = w1_ref[0].astype(BF16)
        w3b[...] = w3_ref[0].astype(BF16)

    @pl.when(jnp.logical_or(i <= 1, iexp_ref[dn_prev] != iexp_ref[dn]))
    def _():
        w2b[...] = w2_ref[0].astype(BF16)

    n_half = 4
    rh = r // n_half

    def run(hm_rd, hm_wr):
        for hf in range(n_half):
            sl = slice(hf * rh, (hf + 1) * rh)
            y = jnp.dot(hm_rd[sl, :], w2b[...], preferred_element_type=F32)
            y = jnp.where(dn_first, y, acc[sl, :] + y)
            acc[sl, :] = y
            _store_token_rows(ys_ref, hf * rh, rh, y)

            rows = blk * r + hf * rh + lax.broadcasted_iota(jnp.int32, (rh, 1), 0)
            valid = jnp.logical_and(rows >= lo, rows < hi)
            x = _rows_to_features(_load_token_rows(xs_ref, hf * rh, rh))
            x = jnp.where(valid, x, 0.0).astype(BF16)
            a = jnp.dot(x, w1b[...], preferred_element_type=F32)
            bb = jnp.dot(x, w3b[...], preferred_element_type=F32)
            hm_wr[sl, :] = (a * jax.nn.sigmoid(a) * bb).astype(BF16)

    @pl.when(slot == 0)
    def _():
        run(hm.at[1], hm.at[0])

    @pl.when(slot == 1)
    def _():
        run(hm.at[0], hm.at[1])


def _experts(iblk, iexp, ilo, ihi, xs, w1, w3, w2, r):
    na, kw = xs.shape
    rows = r * ROW_CHUNKS
    n_items = iblk.shape[0]
    d = w1.shape[1]
    de = w1.shape[2]
    up = lambda i: jnp.minimum(i, n_items - 1)
    down = lambda i: jnp.maximum(i - 1, 0)
    grid_spec = pltpu.PrefetchScalarGridSpec(
        num_scalar_prefetch=4,
        grid=(n_items + 1,),
        in_specs=[pl.BlockSpec((rows, kw), lambda i, ib, ie, il, ih: (ib[up(i)], 0)),
                  pl.BlockSpec((1, d, de), lambda i, ib, ie, il, ih: (ie[up(i)], 0, 0)),
                  pl.BlockSpec((1, d, de), lambda i, ib, ie, il, ih: (ie[up(i)], 0, 0)),
                  pl.BlockSpec((1, de, d), lambda i, ib, ie, il, ih: (ie[down(i)], 0, 0))],
        out_specs=pl.BlockSpec((rows, kw), lambda i, ib, ie, il, ih: (ib[down(i)], 0)),
        scratch_shapes=[pltpu.VMEM((r, d), F32), pltpu.VMEM((2, r, de), BF16),
                        pltpu.VMEM((d, de), BF16), pltpu.VMEM((d, de), BF16), pltpu.VMEM((de, d), BF16)],
    )
    return pl.pallas_call(
        _expert_kernel,
        grid_spec=grid_spec,
        out_shape=jax.ShapeDtypeStruct((na, kw), U32),
        compiler_params=_cparams(("arbitrary",)),
        name="moe_experts",
    )(iblk, iexp, ilo, ihi, xs, w1, w3, w2)


def _combine_kernel(dest_ref, dnext_ref, x1_ref, h2p_ref, wts_ref, g2_ref, sw1_ref, sw3_ref, sw2_ref,
                    fg_ref, ys_ref, o_ref, gbuf, sem):
    tm = x1_ref.shape[0]
    i = pl.program_id(0)
    n = pl.num_programs(0)
    slot = i % 2

    rows = ROW_CHUNKS * tm

    def gather_row(dref, s, t, off, j):
        d = pl.multiple_of(dref[t * TOP_K + j], ROW_CHUNKS)
        pltpu.make_async_copy(ys_ref.at[pl.ds(d, ROW_CHUNKS)], gbuf.at[s, j, pl.ds(off, ROW_CHUNKS)],
                              sem.at[s]).start(priority=j % 2)

    def wait_slot(s):
        for _ in range(TOP_K):
            pltpu.make_async_copy(ys_ref.at[pl.ds(0, rows)], gbuf.at[s, 0], sem.at[s]).wait()

    @pl.when(i == 0)
    def _():
        def body(t, carry):
            for j in range(TOP_K):
                gather_row(dest_ref, 0, t, pl.multiple_of(t * ROW_CHUNKS, ROW_CHUNKS), j)
            return carry
        lax.fori_loop(0, tm, body, 0)

    wait_slot(slot)
    for t in range(tm):
        for j in range(TOP_K):
            gather_row(dnext_ref, 1 - slot, t, t * ROW_CHUNKS, j)

    wts = wts_ref[...]
    moe = None
    for j in range(TOP_K):
        wj = wts[:, j:j + 1]
        yj = [(lo * wj, hi * wj) for lo, hi in _load_token_rows(gbuf.at[slot, j], 0, tm)]
        moe = yj if moe is None else [(a + lo, b + hi) for (a, b), (lo, hi) in zip(moe, yj)]
    h = _rows_to_features(_load_token_rows(h2p_ref, 0, tm)).astype(BF16)
    a = jnp.dot(h, sw1_ref[...], preferred_element_type=F32)
    bb = jnp.dot(h, sw3_ref[...], preferred_element_type=F32)
    shared = jnp.dot((a * jax.nn.sigmoid(a) * bb).astype(BF16), sw2_ref[...], preferred_element_type=F32)
    ffn = _rows_to_features(moe) + shared
    xo = x1_ref[...] + g2_ref[0] * ffn
    o_ref[...] = _rmsnorm_rows(xo, fg_ref[...])

    @pl.when(i == n - 1)
    def _():
        wait_slot(1 - slot)


def _combine(dest_rows, x1, h2p, wts, g2, sw1, sw3, sw2, fg, ys, tiles_per_batch, tm):
    t, d = x1.shape
    n = t // tm
    de = sw1.shape[1]
    const2 = lambda shape: pl.BlockSpec(shape, lambda i: (0, 0))
    return pl.pallas_call(
        _combine_kernel,
        grid=(n,),
        in_specs=[pl.BlockSpec((tm * TOP_K,), lambda i: (i,), memory_space=pltpu.SMEM),
                  pl.BlockSpec((tm * TOP_K,), lambda i: (jnp.minimum(i + 1, n - 1),),
                               memory_space=pltpu.SMEM),
                  pl.BlockSpec((tm, d), lambda i: (i, 0)),
                  pl.BlockSpec((ROW_CHUNKS * tm, LANES), lambda i: (i, 0)),
                  pl.BlockSpec((tm, TOP_K), lambda i: (i, 0)),
                  pl.BlockSpec((1, 1, d), lambda i: (i // tiles_per_batch, 0, 0)),
                  const2((d, de)), const2((d, de)), const2((de, d)), const2((1, d)),
                  pl.BlockSpec(memory_space=pl.ANY)],
        out_specs=pl.BlockSpec((tm, d), lambda i: (i, 0)),
        out_shape=jax.ShapeDtypeStruct((t, d), F32),
        scratch_shapes=[pltpu.VMEM((2, TOP_K, ROW_CHUNKS * tm, LANES), U32),
                        pltpu.SemaphoreType.DMA((2,))],
        compiler_params=_cparams(("arbitrary",)),
        name="moe_combine",
    )(dest_rows, dest_rows, x1, h2p, wts, g2, sw1, sw3, sw2, fg, ys)


def _route_meta(counts, slot_e, slot_r, block_rows):
    t = slot_e.shape[1]
    ends = jnp.cumsum(counts)
    starts = ends - counts
    experts = jnp.arange(N_EXPERTS, dtype=jnp.int32)
    dest_t = slot_r + jnp.sum(jnp.where(slot_e[:, :, None] == experts, starts, 0), axis=-1)

    n_blk = t * TOP_K // block_rows
    n_items = n_blk + N_EXPERTS - 1
    first_blk = starts // block_rows
    last_blk = (ends - 1) // block_rows
    n_e = jnp.where(counts > 0, last_blk - first_blk + 1, 0)
    item_end = jnp.cumsum(n_e)
    item_off = item_end - n_e
    total = item_end[-1]
    idx = jnp.arange(n_items, dtype=jnp.int32)
    e_of = jnp.minimum(jnp.sum(item_end[None, :] <= idx[:, None], axis=1), N_EXPERTS - 1).astype(jnp.int32)
    onehot = e_of[:, None] == experts[None, :]
    lut = lambda table: jnp.sum(jnp.where(onehot, table[None, :], 0), axis=1)
    live = idx < total
    blk = lut(first_blk) + (idx - lut(item_off))
    lo = jnp.maximum(lut(starts), blk * block_rows)
    hi = jnp.minimum(lut(ends), (blk + 1) * block_rows)
    last_e = jnp.sum(jnp.where(idx == total - 1, e_of, 0))
    iblk = jnp.where(live, blk, n_blk - 1).astype(jnp.int32)
    iexp = jnp.where(live, e_of, last_e).astype(jnp.int32)
    ilo = jnp.where(live, lo, 0).astype(jnp.int32)
    ihi = jnp.where(live, hi, 0).astype(jnp.int32)
    return dest_t, iblk, iexp, ilo, ihi


def _rope_tables(n_tokens):
    n_rows = n_tokens // GRID_W
    rows = np.repeat(np.arange(n_rows, dtype=np.float32), GRID_W)
    cols = np.tile(np.arange(GRID_W, dtype=np.float32), n_rows)
    n_freq = HEAD_DIM // 4
    inv_freq = (np.float32(ROPE_BASE) ** (-np.arange(n_freq, dtype=np.float32) / n_freq)).astype(np.float32)
    ar = rows[:, None] * inv_freq
    ac = cols[:, None] * inv_freq
    cos64 = np.concatenate([np.cos(ar), np.cos(ar), np.cos(ac), np.cos(ac)], axis=1)
    sin64 = np.concatenate([-np.sin(ar), np.sin(ar), -np.sin(ac), np.sin(ac)], axis=1)
    return np.tile(cos64, (1, 2)).astype(np.float32), np.tile(sin64, (1, 2)).astype(np.float32)


def kernel(x, c, ctx, c_ctx, ada_w, ada_b, norm1_g, w_in, lambda_q1, lambda_k1, lambda_q2, lambda_k2,
           dattn_norm_g, ret_decay_fwd, ret_decay_bwd, ret_norm_g, w_out, norm2_g, router_w, router_bias,
           exp_w1, exp_w3, exp_w2, shared_w1, shared_w3, shared_w2, final_norm_g):
    b, s, d = x.shape
    lc = ctx.shape[1]
    depth = ada_w.shape[0]
    assert depth == 1, "single-layer problem: the context stream is never updated"
    l = 0
    lam_init = 0.8 - 0.6 * math.exp(-0.3 * l)

    c8 = jnp.zeros((8, d), F32).at[:b].set(c).at[b].set(c_ctx)
    mod = _modulation(c8, ada_w[l], ada_b[l])
    sh1, sc1, g1, sh2, sc2, g2 = [mod[:b, k * d:(k + 1) * d].reshape(b, 1, d) for k in range(6)]
    csh1 = jnp.broadcast_to(mod[b, 0:d].reshape(1, 1, d), (b, 1, d))
    csc1 = jnp.broadcast_to(mod[b, d:2 * d].reshape(1, 1, d), (b, 1, d))

    w_l = w_in[l]
    w_tok = jnp.concatenate([w_l[:, 512:1024], w_l[:, 1536:3072]], axis=1).astype(BF16)
    w_feat = jnp.concatenate([w_l[:, 0:512], w_l[:, 1024:1536]], axis=1).T.astype(BF16)
    n1g = norm1_g[l].reshape(1, d)
    cos, sin = _rope_tables(s)
    ones_c = jnp.ones((lc, LANES), F32)
    zeros_c = jnp.zeros((lc, LANES), F32)
    px, qt, vtx = _inproj(x, n1g, sc1, sh1, cos, sin, np.ascontiguousarray(cos.T),
                          np.ascontiguousarray(sin.T), w_tok, w_feat, tm=512)
    pc, _, vtc = _inproj(ctx, n1g, csc1, csh1, ones_c, zeros_c, ones_c.T, zeros_c.T, w_tok, w_feat, tm=lc)

    lam = (jnp.exp(jnp.sum(lambda_q1[l] * lambda_k1[l])) - jnp.exp(jnp.sum(lambda_q2[l] * lambda_k2[l]))
           ).astype(F32) + lam_init
    att = _attention(lam.reshape(1), qt, px, vtx, pc, vtc, tq=min(2048, s), tkc=min(512, s // 4))

    dmat, qdec, kdec, gch = _retention_tables(ret_decay_fwd[l], ret_decay_bwd[l])
    ret_f, ret_b = _retention(px, pc, dmat, qdec, kdec, gch, chunks_per_step=min(8, s // RET_CHUNK))

    x1, h2p, slot_e, slot_w, slot_r, counts = _merge(
        att, ret_f, ret_b, px, x, g1, sh2, sc2,
        dattn_norm_g[l].reshape(1, 2 * HEAD_DIM), ret_norm_g[l].reshape(1, R_DV),
        w_out[l].astype(BF16), norm2_g[l].reshape(1, d), router_w[l].T.astype(BF16),
        router_bias[l].astype(F32).reshape(N_EXPERTS, 1), 1.0 - lam_init, tm=512)

    t = b * s
    block_rows = 512
    dest_t, iblk, iexp, ilo, ihi = _route_meta(counts[:, 0].astype(jnp.int32), slot_e, slot_r, block_rows)
    dest_rows = (dest_t.T * ROW_CHUNKS).reshape(t * TOP_K)
    xs = _dispatch(dest_rows, h2p, tm=128)
    ys = _experts(iblk, iexp, ilo, ihi, xs, exp_w1[l], exp_w3[l], exp_w2[l], block_rows)
    tm_c = 128
    out = _combine(dest_rows, x1.reshape(t, d), h2p, slot_w.T, g2, shared_w1[l].astype(BF16),
                   shared_w3[l].astype(BF16), shared_w2[l].astype(BF16), final_norm_g.reshape(1, d),
                   ys, s // tm_c, tm_c)
    return out.reshape(b, s, d)
```

```python
import functools
import math
from typing import NamedTuple

import jax
import jax.numpy as jnp
import numpy as np
from jax import lax
from jax.experimental import pallas as pl
from jax.experimental.pallas import tpu as pltpu

F32 = jnp.float32
BF16 = jnp.bfloat16
U32 = jnp.uint32

D_MODEL = 1024
GRID_W = 64
HEAD_DIM = 64
A_HEADS = 4
R_HEADS = 4
R_DK = 64
R_DV = 128
RET_CHUNK = 128
ROPE_BASE = 10000.0
N_EXPERTS = 64
N_GROUPS = 8
TOPK_GROUPS = 4
TOP_K = 8
D_EXPERT = 256
ROUTED_SCALE = 2.5
NORM_EPS = 1e-6
LANES = 128

P_KZ = 0
P_RQ = 1024
P_RK = 1280
P_RV = 1536
P_RG = 2048
P_COLS = 2560
VT_ROWS = 144
LOG2E = 1.4426950408889634

VMEM_LIMIT = 56 * 1024 * 1024


class _Tiles(NamedTuple):
    proj_tokens: int
    attn_queries: int
    attn_keys: int
    ret_chunks: int
    moe_tokens: int
    expert_rows: int


def _tile_plan(seq_len):
    return _Tiles(proj_tokens=min(512, seq_len), attn_queries=min(2048, seq_len),
                  attn_keys=min(512, seq_len // 4), ret_chunks=min(8, seq_len // RET_CHUNK),
                  moe_tokens=128, expert_rows=512)


def _cparams(sem):
    return pltpu.CompilerParams(dimension_semantics=sem, vmem_limit_bytes=VMEM_LIMIT)


def _mod_kernel(c_ref, w_ref, b_ref, o_ref):
    c = c_ref[...]
    s = c * jax.nn.sigmoid(c)
    o_ref[...] = jnp.dot(s, w_ref[...], preferred_element_type=F32,
                         precision=lax.Precision.HIGHEST) + b_ref[...]


def _modulation(c8, ada_w, ada_b):
    d = c8.shape[1]
    n = ada_w.shape[1]
    tn = 512
    return pl.pallas_call(
        _mod_kernel,
        grid=(n // tn,),
        in_specs=[pl.BlockSpec((8, d), lambda j: (0, 0)),
                  pl.BlockSpec((d, tn), lambda j: (0, j)),
                  pl.BlockSpec((1, tn), lambda j: (0, j))],
        out_specs=pl.BlockSpec((8, tn), lambda j: (0, j)),
        out_shape=jax.ShapeDtypeStruct((8, n), F32),
        compiler_params=_cparams(("arbitrary",)),
        name="modulation",
    )(c8, ada_w, ada_b.reshape(1, n))


def _rmsnorm_rows(x, g):
    return x * lax.rsqrt(jnp.mean(x * x, axis=-1, keepdims=True) + NORM_EPS) * g


def _inproj_kernel(x_ref, g_ref, sc_ref, sh_ref, cos_ref, sin_ref, cost_ref, sint_ref, w_ref, wt_ref,
                   p_ref, qt_ref, vt_ref):
    x = x_ref[0]
    tm = x.shape[0]
    h = _rmsnorm_rows(x, g_ref[...]) * (1.0 + sc_ref[0]) + sh_ref[0]
    hb = h.astype(BF16)
    p = jnp.dot(hb, w_ref[...], preferred_element_type=F32)
    pt = lax.dot_general(wt_ref[...], hb, _NT, preferred_element_type=F32)

    cos = cos_ref[...]
    sin = sin_ref[...]
    lane = lax.broadcasted_iota(jnp.int32, (tm, LANES), 1)
    first_half = (lane % 32) < 16
    low64 = lane < 64

    def rope(xs):
        partner = jnp.where(first_half, pltpu.roll(xs, LANES - 16, 1), pltpu.roll(xs, 16, 1))
        return xs * cos + partner * sin

    zero = jnp.zeros((tm, LANES), BF16)
    for hh in range(A_HEADS):
        k = rope(p[:, hh * 128:(hh + 1) * 128]).astype(BF16)
        p_ref[0, :, P_KZ + hh * 256:P_KZ + hh * 256 + 128] = jnp.where(low64, k, zero)
        p_ref[0, :, P_KZ + hh * 256 + 128:P_KZ + (hh + 1) * 256] = jnp.where(low64, zero, k)
    for cc in range(2):
        rq = rope(p[:, 512 + cc * 128:512 + (cc + 1) * 128])
        p_ref[0, :, P_RQ + cc * 128:P_RQ + (cc + 1) * 128] = rq.astype(BF16)
        rk = rope(p[:, 768 + cc * 128:768 + (cc + 1) * 128]) * (R_DK ** -0.5)
        p_ref[0, :, P_RK + cc * 128:P_RK + (cc + 1) * 128] = rk.astype(BF16)
    p_ref[0, :, P_RV:P_RV + 512] = p[:, 1024:1536].astype(BF16)
    p_ref[0, :, P_RG:P_RG + 512] = p[:, 1536:2048].astype(BF16)

    cost = cost_ref[...]
    sint = sint_ref[...]
    q_scale = (HEAD_DIM ** -0.5) * LOG2E
    ones_rows = jnp.where(lax.broadcasted_iota(jnp.int32, (VT_ROWS - 128, tm), 0) == 0, 1.0, 0.0)
    for hh in range(A_HEADS):
        xq = pt[hh * 128:(hh + 1) * 128, :]
        groups = [xq[16 * gi:16 * (gi + 1), :] for gi in range(8)]
        partner = jnp.concatenate([groups[gi ^ 1] for gi in range(8)], axis=0)
        qt_ref[0, hh] = ((xq * cost + partner * sint) * q_scale).astype(BF16)
        vt_ref[0, hh, 0:128, :] = pt[512 + hh * 128:512 + (hh + 1) * 128, :].astype(BF16)
        vt_ref[0, hh, 128:VT_ROWS, :] = ones_rows.astype(BF16)


def _inproj(x, g, sc, sh, cos, sin, cost, sint, w_bf, wt_bf, tm):
    b, s, d = x.shape
    n_tok = w_bf.shape[1]
    n_t = wt_bf.shape[0]
    const2 = lambda shape: pl.BlockSpec(shape, lambda bi, i: (0, 0))
    per_b = pl.BlockSpec((1, 1, d), lambda bi, i: (bi, 0, 0))
    return pl.pallas_call(
        _inproj_kernel,
        grid=(b, s // tm),
        in_specs=[pl.BlockSpec((1, tm, d), lambda bi, i: (bi, i, 0)),
                  const2((1, d)), per_b, per_b,
                  pl.BlockSpec((tm, LANES), lambda bi, i: (i, 0)),
                  pl.BlockSpec((tm, LANES), lambda bi, i: (i, 0)),
                  pl.BlockSpec((LANES, tm), lambda bi, i: (0, i)),
                  pl.BlockSpec((LANES, tm), lambda bi, i: (0, i)),
                  const2((d, n_tok)), const2((n_t, d))],
        out_specs=[pl.BlockSpec((1, tm, P_COLS), lambda bi, i: (bi, i, 0)),
                   pl.BlockSpec((1, A_HEADS, LANES, tm), lambda bi, i: (bi, 0, 0, i)),
                   pl.BlockSpec((1, A_HEADS, VT_ROWS, tm), lambda bi, i: (bi, 0, 0, i))],
        out_shape=[jax.ShapeDtypeStruct((b, s, P_COLS), BF16),
                   jax.ShapeDtypeStruct((b, A_HEADS, LANES, s), BF16),
                   jax.ShapeDtypeStruct((b, A_HEADS, VT_ROWS, s), BF16)],
        compiler_params=_cparams(("arbitrary", "arbitrary")),
        name="inproj",
    )(x, g, sc, sh, cos, sin, cost, sint, w_bf, wt_bf)


_NT = (((1,), (1,)), ((), ()))


def _attn_kernel(lam_ref, qt_ref, kzc_ref, vtc_ref, kzx_ref, vtx_ref, o_ref,
                 acc, s_a, s_b, e_a, e_b, *, tkc):
    qt = qt_ref[0, 0]
    tq = qt.shape[1]
    n_chunks = kzx_ref.shape[1] // tkc
    assert n_chunks % 2 == 0 and n_chunks >= 4

    def scores(kz, s_ref):
        for mp in range(2):
            s_ref[mp] = jnp.dot(kz[:, mp * LANES:(mp + 1) * LANES], qt, preferred_element_type=F32)

    def softmax(s_ref, e_ref, ms):
        new_ms, alphas = [], []
        for mp in range(2):
            st = s_ref[mp]
            m_new = jnp.maximum(ms[mp], jnp.max(st, axis=0, keepdims=True))
            alphas.append(jnp.exp2(ms[mp] - m_new))
            e_ref[mp] = jnp.exp2(st - m_new).astype(BF16)
            new_ms.append(m_new)
        return tuple(new_ms), tuple(alphas)

    def values(vt, e_ref, alphas):
        for mp in range(2):
            acc[mp] = alphas[mp] * acc[mp] + jnp.dot(vt, e_ref[mp], preferred_element_type=F32)

    def kchunk(c):
        return kzx_ref[0, pl.ds(pl.multiple_of(c * tkc, tkc), tkc), :]

    def vchunk(c):
        return vtx_ref[0, 0, :, pl.ds(pl.multiple_of(c * tkc, tkc), tkc)]

    ms = []
    for mp in range(2):
        st = jnp.dot(kzc_ref[0, :, mp * LANES:(mp + 1) * LANES], qt, preferred_element_type=F32)
        m_new = jnp.max(st, axis=0, keepdims=True)
        acc[mp] = jnp.dot(vtc_ref[0, 0], jnp.exp2(st - m_new).astype(BF16), preferred_element_type=F32)
        ms.append(m_new)
    ms = tuple(ms)

    scores(kchunk(0), s_a)
    scores(kchunk(1), s_b)
    ms, al_a = softmax(s_a, e_a, ms)

    def body(j, carry):
        ms, al_a = carry
        scores(kchunk(2 * j + 2), s_a)
        ms, al_b = softmax(s_b, e_b, ms)
        values(vchunk(2 * j), e_a, al_a)
        scores(kchunk(2 * j + 3), s_b)
        ms, al_a = softmax(s_a, e_a, ms)
        values(vchunk(2 * j + 1), e_b, al_b)
        return ms, al_a

    ms, al_a = lax.fori_loop(0, n_chunks // 2 - 1, body, (ms, al_a))
    ms, al_b = softmax(s_b, e_b, ms)
    values(vchunk(n_chunks - 2), e_a, al_a)
    values(vchunk(n_chunks - 1), e_b, al_b)

    a1 = acc[0]
    a2 = acc[1]
    o1 = a1[:LANES, :] / a1[LANES:LANES + 1, :]
    o2 = a2[:LANES, :] / a2[LANES:LANES + 1, :]
    o_ref[0] = (o1 - lam_ref[0] * o2).T.astype(BF16)


def _attention(lam, qt, px, vtx, pc, vtc, tq, tkc):
    b, s, _ = px.shape
    lc = pc.shape[1]
    kern = functools.partial(_attn_kernel, tkc=tkc)
    return pl.pallas_call(
        kern,
        grid=(b, A_HEADS, s // tq),
        in_specs=[pl.BlockSpec(memory_space=pltpu.SMEM),
                  pl.BlockSpec((1, 1, LANES, tq), lambda bi, h, i: (bi, h, 0, i)),
                  pl.BlockSpec((1, lc, 256), lambda bi, h, i: (bi, 0, P_KZ // 256 + h)),
                  pl.BlockSpec((1, 1, VT_ROWS, lc), lambda bi, h, i: (bi, h, 0, 0)),
                  pl.BlockSpec((1, s, 256), lambda bi, h, i: (bi, 0, P_KZ // 256 + h)),
                  pl.BlockSpec((1, 1, VT_ROWS, s), lambda bi, h, i: (bi, h, 0, 0))],
        out_specs=pl.BlockSpec((1, tq, LANES), lambda bi, h, i: (bi, i, h)),
        out_shape=jax.ShapeDtypeStruct((b, s, A_HEADS * LANES), BF16),
        scratch_shapes=[pltpu.VMEM((2, VT_ROWS, tq), F32),
                        pltpu.VMEM((2, tkc, tq), F32), pltpu.VMEM((2, tkc, tq), F32),
                        pltpu.VMEM((2, tkc, tq), BF16), pltpu.VMEM((2, tkc, tq), BF16)],
        compiler_params=_cparams(("arbitrary", "arbitrary", "arbitrary")),
        name="diff_attention",
    )(lam, qt, pc, vtc, px, vtx)


_TN = (((0,), (0,)), ((), ()))


def _ret_kernel(qf_ref, kf_ref, vf_ref, qb_ref, kb_ref, vb_ref, kc_ref, vc_ref,
                dmat_ref, qdec_ref, kdec_ref, gch_ref, of_ref, ob_ref, sf, sb):
    c = RET_CHUNK
    n = pl.program_id(1)
    col_head = lax.broadcasted_iota(jnp.int32, (c, R_HEADS * R_DK), 1) // R_DK
    bd_mask = (lax.broadcasted_iota(jnp.int32, (R_HEADS * R_DK, R_HEADS * R_DV), 0) // R_DK
               == lax.broadcasted_iota(jnp.int32, (R_HEADS * R_DK, R_HEADS * R_DV), 1) // R_DV)

    def state_update(s_ref, k, v, kdec, gch):
        kd = (k.astype(F32) * kdec).astype(BF16)
        kv = lax.dot_general(kd, v, _TN, preferred_element_type=F32)
        s_ref[...] = gch * s_ref[...] + jnp.where(bd_mask, kv, 0.0)

    def chunk(s_ref, q, k, v, dmat, qdec, kdec, gch):
        zero = jnp.zeros_like(k)
        kblk = jnp.concatenate([jnp.where(col_head == h, k, zero) for h in range(R_HEADS)], axis=0)
        sc = lax.dot_general(q, kblk, _NT, preferred_element_type=F32) * dmat
        p = sc.astype(BF16)
        intra = jnp.concatenate(
            [jnp.dot(p[:, h * c:(h + 1) * c], v[:, h * R_DV:(h + 1) * R_DV], preferred_element_type=F32)
             for h in range(R_HEADS)], axis=1)
        qd = (q.astype(F32) * qdec).astype(BF16)
        cross = jnp.dot(qd, s_ref[...].astype(BF16), preferred_element_type=F32)
        out = intra + cross
        state_update(s_ref, k, v, kdec, gch)
        return out

    @pl.when(n == 0)
    def _():
        sf[...] = jnp.zeros_like(sf)
        sb[...] = jnp.zeros_like(sb)
        n_ctx = kc_ref.shape[1] // c
        for cc in range(n_ctx):
            state_update(sf, kc_ref[0, cc * c:(cc + 1) * c, :], vc_ref[0, cc * c:(cc + 1) * c, :],
                         kdec_ref[0], gch_ref[0])
        for cc in reversed(range(n_ctx)):
            state_update(sb, kc_ref[0, cc * c:(cc + 1) * c, :], vc_ref[0, cc * c:(cc + 1) * c, :],
                         kdec_ref[1], gch_ref[1])

    chunks_per_step = qf_ref.shape[1] // c
    for cc in range(chunks_per_step):
        rows = slice(cc * c, (cc + 1) * c)
        of_ref[0, rows, :] = chunk(sf, qf_ref[0, rows, :], kf_ref[0, rows, :], vf_ref[0, rows, :], dmat_ref[0],
                                   qdec_ref[0], kdec_ref[0], gch_ref[0]).astype(BF16)
    for cc in reversed(range(chunks_per_step)):
        rows = slice(cc * c, (cc + 1) * c)
        ob_ref[0, rows, :] = chunk(sb, qb_ref[0, rows, :], kb_ref[0, rows, :], vb_ref[0, rows, :], dmat_ref[1],
                                   qdec_ref[1], kdec_ref[1], gch_ref[1]).astype(BF16)


def _retention(px, pc, dmat, qdec, kdec, gch, chunks_per_step):
    b, s, _ = px.shape
    lc = pc.shape[1]
    c = RET_CHUNK
    rows = c * chunks_per_step
    nch = s // rows
    qk_w = R_HEADS * R_DK
    v_w = R_HEADS * R_DV
    fwd = lambda bi, n: (bi, n, 0)
    spec_q = lambda im: pl.BlockSpec((1, rows, qk_w), lambda bi, n: im(bi, n)[:2] + (P_RQ // qk_w,))
    spec_k = lambda im: pl.BlockSpec((1, rows, qk_w), lambda bi, n: im(bi, n)[:2] + (P_RK // qk_w,))
    spec_v = lambda im: pl.BlockSpec((1, rows, v_w), lambda bi, n: im(bi, n)[:2] + (P_RV // v_w,))
    bwd = lambda bi, n: (bi, nch - 1 - n, 0)
    const3 = lambda shape: pl.BlockSpec(shape, lambda bi, n: (0, 0, 0))
    return pl.pallas_call(
        _ret_kernel,
        grid=(b, nch),
        in_specs=[spec_q(fwd), spec_k(fwd), spec_v(fwd), spec_q(bwd), spec_k(bwd), spec_v(bwd),
                  pl.BlockSpec((1, lc, qk_w), lambda bi, n: (bi, 0, P_RK // qk_w)),
                  pl.BlockSpec((1, lc, v_w), lambda bi, n: (bi, 0, P_RV // v_w)),
                  const3((2, c, R_HEADS * c)), const3((2, c, qk_w)), const3((2, c, qk_w)),
                  const3((2, 1, v_w))],
        out_specs=[pl.BlockSpec((1, rows, v_w), fwd), pl.BlockSpec((1, rows, v_w), bwd)],
        out_shape=[jax.ShapeDtypeStruct((b, s, v_w), BF16), jax.ShapeDtypeStruct((b, s, v_w), BF16)],
        scratch_shapes=[pltpu.VMEM((qk_w, v_w), F32), pltpu.VMEM((qk_w, v_w), F32)],
        compiler_params=_cparams(("arbitrary", "arbitrary")),
        name="retention",
    )(px, px, px, px, px, px, pc, pc, dmat, qdec, kdec, gch)


def _retention_tables(decay_fwd, decay_bwd):
    c = RET_CHUNK
    pos = jnp.arange(c, dtype=F32)
    rel = pos[:, None] - pos[None, :]

    def tables(logits, backward):
        lg = jax.nn.log_sigmoid(logits.astype(F32))
        r = -rel if backward else rel
        dm = jnp.where(r[None] >= 0, jnp.exp(jnp.maximum(r, 0.0)[None] * lg[:, None, None]), 0.0)
        dm = jnp.transpose(dm, (1, 0, 2)).reshape(c, R_HEADS * c)
        qpow = (c - pos) if backward else (pos + 1)
        kpow = pos if backward else (c - 1 - pos)
        qd = jnp.repeat(jnp.exp(qpow[:, None] * lg[None, :]), R_DK, axis=1)
        kd = jnp.repeat(jnp.exp(kpow[:, None] * lg[None, :]), R_DK, axis=1)
        gc = jnp.repeat(jnp.exp(c * lg), R_DV)[None, :]
        return dm, qd, kd, gc

    tf = tables(decay_fwd, False)
    tb = tables(decay_bwd, True)
    return tuple(jnp.stack([a, bb]) for a, bb in zip(tf, tb))


def _pack_pairs(x):
    k = x.shape[1] // 2
    lo = lax.bitcast_convert_type(x[:, :k].astype(BF16).astype(F32), U32)
    hi = lax.bitcast_convert_type(x[:, k:].astype(BF16).astype(F32), U32)
    return (lo >> 16) | (hi & jnp.uint32(0xFFFF0000))


def _unpack_pairs(w):
    lo = lax.bitcast_convert_type(w << 16, F32)
    hi = lax.bitcast_convert_type(w & jnp.uint32(0xFFFF0000), F32)
    return lo, hi


ROW_CHUNKS = (D_MODEL // 2) // LANES


def _store_token_rows(ref, first_token, n_tokens, x):
    packed = _pack_pairs(x)
    for c in range(ROW_CHUNKS):
        ref[pl.ds(first_token * ROW_CHUNKS + c, n_tokens, stride=ROW_CHUNKS), :] = (
            packed[:, c * LANES:(c + 1) * LANES])


def _load_token_rows(ref, first_token, n_tokens):
    return [_unpack_pairs(ref[pl.ds(first_token * ROW_CHUNKS + c, n_tokens, stride=ROW_CHUNKS), :])
            for c in range(ROW_CHUNKS)]


def _rows_to_features(chunks):
    return jnp.concatenate([lo for lo, _ in chunks] + [hi for _, hi in chunks], axis=1)


def _route_tile(logits_t, bias_col, carry_ref):
    n_e, tm = logits_t.shape
    per = n_e // N_GROUPS
    neg = -jnp.inf
    scores = jax.nn.sigmoid(logits_t)
    sel = scores + bias_col
    sub = lax.broadcasted_iota(jnp.int32, (per, tm), 0).astype(F32)
    sel_g = [sel[g * per:(g + 1) * per, :] for g in range(N_GROUPS)]
    flat_g = [sub + float(g * per) for g in range(N_GROUPS)]

    gs = []
    for g in range(N_GROUPS):
        top1 = jnp.max(sel_g[g], axis=0, keepdims=True)
        idx1 = jnp.min(jnp.where(sel_g[g] == top1, sub, float(per)), axis=0, keepdims=True)
        top2 = jnp.max(jnp.where(sub == idx1, neg, sel_g[g]), axis=0, keepdims=True)
        gs.append(top1 + top2)
    gsel = [jnp.zeros((1, tm), F32) for _ in range(N_GROUPS)]
    for _ in range(TOPK_GROUPS):
        m = functools.reduce(jnp.maximum, gs)
        found = jnp.zeros((1, tm), F32)
        for g in range(N_GROUPS):
            hit = jnp.logical_and(gs[g] == m, found == 0.0)
            found = jnp.where(hit, 1.0, found)
            gsel[g] = jnp.where(hit, 1.0, gsel[g])
            gs[g] = jnp.where(hit, neg, gs[g])

    masked = [jnp.where(gsel[g] > 0.0, sel_g[g], neg) for g in range(N_GROUPS)]
    chosen = [jnp.zeros((per, tm), F32) for _ in range(N_GROUPS)]
    for _ in range(TOP_K):
        m = functools.reduce(jnp.maximum, [jnp.max(mg, axis=0, keepdims=True) for mg in masked])
        idx = functools.reduce(jnp.minimum, [
            jnp.min(jnp.where(masked[g] == m, flat_g[g], float(n_e)), axis=0, keepdims=True)
            for g in range(N_GROUPS)])
        for g in range(N_GROUPS):
            hit = flat_g[g] == idx
            chosen[g] = jnp.where(hit, 1.0, chosen[g])
            masked[g] = jnp.where(hit, neg, masked[g])
    chosen = jnp.concatenate(chosen, axis=0)
    flat = jnp.concatenate(flat_g, axis=0)

    w_dense = scores * chosen
    w_dense = w_dense / jnp.sum(w_dense, axis=0, keepdims=True) * ROUTED_SCALE

    chosen_bf = chosen.astype(BF16)
    lower = (lax.broadcasted_iota(jnp.int32, (n_e, n_e), 1)
             < lax.broadcasted_iota(jnp.int32, (n_e, n_e), 0)).astype(BF16)
    slot_idx = jnp.dot(lower, chosen_bf, preferred_element_type=F32)
    upper = (lax.broadcasted_iota(jnp.int32, (tm, tm), 0)
             < lax.broadcasted_iota(jnp.int32, (tm, tm), 1)).astype(BF16)
    rank = jnp.dot(chosen_bf, upper, preferred_element_type=F32) + carry_ref[...]
    carry_ref[...] += jnp.sum(chosen, axis=1, keepdims=True)

    slots = []
    for j in range(TOP_K):
        ind = jnp.where(slot_idx == float(j), chosen, 0.0)
        slots.append((jnp.sum(ind * flat, axis=0, keepdims=True),
                      jnp.sum(ind * w_dense, axis=0, keepdims=True),
                      jnp.sum(ind * rank, axis=0, keepdims=True)))
    return slots


def _merge_kernel(att_ref, rf_ref, rb_ref, gate_ref, x_ref, g1_ref, sh2_ref, sc2_ref, ga_ref, gr_ref,
                  wout_ref, n2g_ref, rwt_ref, rbias_ref, x1_ref, h2p_ref, se_ref, sw_ref, sr_ref, cnt_ref,
                  carry, *, att_scale):
    @pl.when(jnp.logical_and(pl.program_id(0) == 0, pl.program_id(1) == 0))
    def _():
        carry[...] = jnp.zeros_like(carry)

    att = att_ref[0].astype(F32)
    ret = rf_ref[0].astype(F32) + rb_ref[0].astype(F32)
    gate = gate_ref[0].astype(F32)
    parts = []
    for h in range(A_HEADS):
        a = att[:, h * 128:(h + 1) * 128]
        parts.append(_rmsnorm_rows(a, ga_ref[...]) * att_scale)
    for h in range(R_HEADS):
        r = _rmsnorm_rows(ret[:, h * R_DV:(h + 1) * R_DV], gr_ref[...])
        g = gate[:, h * R_DV:(h + 1) * R_DV]
        parts.append(g * jax.nn.sigmoid(g) * r)
    merged = jnp.concatenate(parts, axis=1).astype(BF16)
    y = jnp.dot(merged, wout_ref[...], preferred_element_type=F32)
    x1 = x_ref[0] + g1_ref[0] * y
    x1_ref[0] = x1
    h2 = _rmsnorm_rows(x1, n2g_ref[...]) * (1.0 + sc2_ref[0]) + sh2_ref[0]
    _store_token_rows(h2p_ref, 0, h2.shape[0], h2)
    logits_t = lax.dot_general(rwt_ref[...], h2.astype(BF16), _NT, preferred_element_type=F32)
    slots = _route_tile(logits_t, rbias_ref[...], carry)
    for j, (e_j, w_j, r_j) in enumerate(slots):
        se_ref[j:j + 1, :] = e_j.astype(jnp.int32)
        sw_ref[j:j + 1, :] = w_j
        sr_ref[j:j + 1, :] = r_j.astype(jnp.int32)
    cnt_ref[...] = carry[...]


def _merge(att, rf, rb, px, x, g1, sh2, sc2, ga, gr, wout_bf, n2g, rwt_bf, rbias, att_scale, tm):
    b, s, d = x.shape
    nt = s // tm
    chunked = pl.BlockSpec((ROW_CHUNKS * tm, LANES), lambda bi, i: (bi * nt + i, 0))
    chunked_shape = jax.ShapeDtypeStruct((b * s * ROW_CHUNKS, LANES), U32)
    row = lambda bi, i: (bi, i, 0)
    tok = lambda bi, i: (0, bi * nt + i)
    per_b = pl.BlockSpec((1, 1, d), lambda bi, i: (bi, 0, 0))
    const2 = lambda shape: pl.BlockSpec(shape, lambda bi, i: (0, 0))
    return pl.pallas_call(
        functools.partial(_merge_kernel, att_scale=att_scale),
        grid=(b, nt),
        in_specs=[pl.BlockSpec((1, tm, 512), row), pl.BlockSpec((1, tm, 512), row),
                  pl.BlockSpec((1, tm, 512), row),
                  pl.BlockSpec((1, tm, 512), lambda bi, i: (bi, i, P_RG // 512)),
                  pl.BlockSpec((1, tm, d), row), per_b, per_b, per_b,
                  const2((1, 128)), const2((1, R_DV)), const2((d, d)), const2((1, d)),
                  const2((N_EXPERTS, d)), const2((N_EXPERTS, 1))],
        out_specs=[pl.BlockSpec((1, tm, d), row), chunked,
                   pl.BlockSpec((TOP_K, tm), tok), pl.BlockSpec((TOP_K, tm), tok),
                   pl.BlockSpec((TOP_K, tm), tok), const2((N_EXPERTS, 1))],
        out_shape=[jax.ShapeDtypeStruct((b, s, d), F32), chunked_shape,
                   jax.ShapeDtypeStruct((TOP_K, b * s), jnp.int32),
                   jax.ShapeDtypeStruct((TOP_K, b * s), F32),
                   jax.ShapeDtypeStruct((TOP_K, b * s), jnp.int32),
                   jax.ShapeDtypeStruct((N_EXPERTS, 1), F32)],
        scratch_shapes=[pltpu.VMEM((N_EXPERTS, 1), F32)],
        compiler_params=_cparams(("arbitrary", "arbitrary")),
        name="mixer_merge",
    )(att, rf, rb, px, x, g1, sh2, sc2, ga, gr, wout_bf, n2g, rwt_bf, rbias)


def _dispatch_kernel(dest_ref, h_ref, xs_ref, buf, sem):
    rows = h_ref.shape[0]
    tm = rows // ROW_CHUNKS
    i = pl.program_id(0)
    n = pl.num_programs(0)
    slot = i % 2

    def wait_slot(s):
        for _ in range(TOP_K):
            pltpu.make_async_copy(buf.at[s], xs_ref.at[pl.ds(0, rows)], sem.at[s]).wait()

    @pl.when(i >= 2)
    def _():
        wait_slot(slot)

    buf[slot] = h_ref[...]

    def body(t, carry):
        src = buf.at[slot, pl.ds(pl.multiple_of(t * ROW_CHUNKS, ROW_CHUNKS), ROW_CHUNKS)]
        for j in range(TOP_K):
            d = pl.multiple_of(dest_ref[t * TOP_K + j], ROW_CHUNKS)
            pltpu.make_async_copy(src, xs_ref.at[pl.ds(d, ROW_CHUNKS)], sem.at[slot]).start(priority=j % 2)
        return carry

    lax.fori_loop(0, tm, body, 0)

    @pl.when(i == n - 1)
    def _():
        wait_slot(slot)

    @pl.when(jnp.logical_and(i == n - 1, n >= 2))
    def _():
        wait_slot(1 - slot)


def _dispatch(dest_rows, h2p, tm):
    n_rows = h2p.shape[0]
    n = n_rows // (ROW_CHUNKS * tm)
    return pl.pallas_call(
        _dispatch_kernel,
        grid=(n,),
        in_specs=[pl.BlockSpec((tm * TOP_K,), lambda i: (i,), memory_space=pltpu.SMEM),
                  pl.BlockSpec((ROW_CHUNKS * tm, LANES), lambda i: (i, 0))],
        out_specs=pl.BlockSpec(memory_space=pl.ANY),
        out_shape=jax.ShapeDtypeStruct((n_rows * TOP_K, LANES), U32),
        scratch_shapes=[pltpu.VMEM((2, ROW_CHUNKS * tm, LANES), U32), pltpu.SemaphoreType.DMA((2,))],
        compiler_params=_cparams(("arbitrary",)),
        name="moe_dispatch",
    )(dest_rows, h2p)


def _expert_kernel(iblk_ref, iexp_ref, ilo_ref, ihi_ref, xs_ref, w1_ref, w3_ref, w2_ref, ys_ref,
                   acc, hm, w1b, w3b, w2b):
    r = xs_ref.shape[0] // ROW_CHUNKS
    i = pl.program_id(0)
    n_items = pl.num_programs(0) - 1
    cur = jnp.minimum(i, n_items - 1)
    cur_prev = jnp.maximum(cur - 1, 0)
    dn = jnp.maximum(i - 1, 0)
    dn_prev = jnp.maximum(i - 2, 0)
    blk = iblk_ref[cur]
    lo = ilo_ref[cur]
    hi = ihi_ref[cur]
    dn_first = jnp.logical_or(i <= 1, iblk_ref[dn_prev] != iblk_ref[dn])
    slot = i % 2

    @pl.when(i == 0)
    def _():
        acc[...] = jnp.zeros_like(acc)
        hm[...] = jnp.zeros_like(hm)

    @pl.when(jnp.logical_or(i == 0, iexp_ref[cur_prev] != iexp_ref[cur]))
    def _():
        w1b[...] = w1_ref[0].astype(BF16)
        w3b[...] = w3_ref[0].astype(BF16)

    @pl.when(jnp.logical_or(i <= 1, iexp_ref[dn_prev] != iexp_ref[dn]))
    def _():
        w2b[...] = w2_ref[0].astype(BF16)

    n_half = 4
    rh = r // n_half

    def run(hm_rd, hm_wr):
        for hf in range(n_half):
            sl = slice(hf * rh, (hf + 1) * rh)
            y = jnp.dot(hm_rd[sl, :], w2b[...], preferred_element_type=F32)
            y = jnp.where(dn_first, y, acc[sl, :] + y)
            acc[sl, :] = y
            _store_token_rows(ys_ref, hf * rh, rh, y)

            rows = blk * r + hf * rh + lax.broadcasted_iota(jnp.int32, (rh, 1), 0)
            valid = jnp.logical_and(rows >= lo, rows < hi)
            x = _rows_to_features(_load_token_rows(xs_ref, hf * rh, rh))
            x = jnp.where(valid, x, 0.0).astype(BF16)
            a = jnp.dot(x, w1b[...], preferred_element_type=F32)
            bb = jnp.dot(x, w3b[...], preferred_element_type=F32)
            hm_wr[sl, :] = (a * jax.nn.sigmoid(a) * bb).astype(BF16)

    @pl.when(slot == 0)
    def _():
        run(hm.at[1], hm.at[0])

    @pl.when(slot == 1)
    def _():
        run(hm.at[0], hm.at[1])


def _experts(iblk, iexp, ilo, ihi, xs, w1, w3, w2, r):
    na, kw = xs.shape
    rows = r * ROW_CHUNKS
    n_items = iblk.shape[0]
    d = w1.shape[1]
    de = w1.shape[2]
    up = lambda i: jnp.minimum(i, n_items - 1)
    down = lambda i: jnp.maximum(i - 1, 0)
    grid_spec = pltpu.PrefetchScalarGridSpec(
        num_scalar_prefetch=4,
        grid=(n_items + 1,),
        in_specs=[pl.BlockSpec((rows, kw), lambda i, ib, ie, il, ih: (ib[up(i)], 0)),
                  pl.BlockSpec((1, d, de), lambda i, ib, ie, il, ih: (ie[up(i)], 0, 0)),
                  pl.BlockSpec((1, d, de), lambda i, ib, ie, il, ih: (ie[up(i)], 0, 0)),
                  pl.BlockSpec((1, de, d), lambda i, ib, ie, il, ih: (ie[down(i)], 0, 0))],
        out_specs=pl.BlockSpec((rows, kw), lambda i, ib, ie, il, ih: (ib[down(i)], 0)),
        scratch_shapes=[pltpu.VMEM((r, d), F32), pltpu.VMEM((2, r, de), BF16),
                        pltpu.VMEM((d, de), BF16), pltpu.VMEM((d, de), BF16), pltpu.VMEM((de, d), BF16)],
    )
    return pl.pallas_call(
        _expert_kernel,
        grid_spec=grid_spec,
        out_shape=jax.ShapeDtypeStruct((na, kw), U32),
        compiler_params=_cparams(("arbitrary",)),
        name="moe_experts",
    )(iblk, iexp, ilo, ihi, xs, w1, w3, w2)


def _combine_kernel(dest_ref, dnext_ref, x1_ref, h2p_ref, wts_ref, g2_ref, sw1_ref, sw3_ref, sw2_ref,
                    fg_ref, ys_ref, o_ref, gbuf, sem):
    tm = x1_ref.shape[0]
    i = pl.program_id(0)
    n = pl.num_programs(0)
    slot = i % 2

    rows = ROW_CHUNKS * tm

    def gather_row(dref, s, t, off, j):
        d = pl.multiple_of(dref[t * TOP_K + j], ROW_CHUNKS)
        pltpu.make_async_copy(ys_ref.at[pl.ds(d, ROW_CHUNKS)], gbuf.at[s, j, pl.ds(off, ROW_CHUNKS)],
                              sem.at[s]).start(priority=j % 2)

    def wait_slot(s):
        for _ in range(TOP_K):
            pltpu.make_async_copy(ys_ref.at[pl.ds(0, rows)], gbuf.at[s, 0], sem.at[s]).wait()

    @pl.when(i == 0)
    def _():
        def body(t, carry):
            for j in range(TOP_K):
                gather_row(dest_ref, 0, t, pl.multiple_of(t * ROW_CHUNKS, ROW_CHUNKS), j)
            return carry
        lax.fori_loop(0, tm, body, 0)

    def step(cur):
        wait_slot(cur)
        wts = wts_ref[...]
        n_pieces = TOP_K + 2
        bounds = [(p * tm) // n_pieces for p in range(n_pieces + 1)]

        def issue_piece(p):
            for t in range(bounds[p], bounds[p + 1]):
                for jj in range(TOP_K):
                    gather_row(dnext_ref, 1 - cur, t, t * ROW_CHUNKS, jj)

        issue_piece(0)
        h = _rows_to_features(_load_token_rows(h2p_ref, 0, tm)).astype(BF16)
        a = jnp.dot(h, sw1_ref[...], preferred_element_type=F32)
        bb = jnp.dot(h, sw3_ref[...], preferred_element_type=F32)
        shared = jnp.dot((a * jax.nn.sigmoid(a) * bb).astype(BF16), sw2_ref[...],
                         preferred_element_type=F32)
        moe = None
        for j in range(TOP_K):
            issue_piece(1 + j)
            wj = wts[:, j:j + 1]
            yj = [(lo * wj, hi * wj) for lo, hi in _load_token_rows(gbuf.at[cur, j], 0, tm)]
            moe = yj if moe is None else [(a + lo, b + hi) for (a, b), (lo, hi) in zip(moe, yj)]
        issue_piece(n_pieces - 1)
        ffn = _rows_to_features(moe) + shared
        xo = x1_ref[...] + g2_ref[0] * ffn
        o_ref[...] = _rmsnorm_rows(xo, fg_ref[...])

        @pl.when(i == n - 1)
        def _():
            wait_slot(1 - cur)

    @pl.when(slot == 0)
    def _():
        step(0)

    @pl.when(slot == 1)
    def _():
        step(1)


def _combine(dest_rows, x1, h2p, wts, g2, sw1, sw3, sw2, fg, ys, tiles_per_batch, tm):
    t, d = x1.shape
    n = t // tm
    de = sw1.shape[1]
    const2 = lambda shape: pl.BlockSpec(shape, lambda i: (0, 0))
    return pl.pallas_call(
        _combine_kernel,
        grid=(n,),
        in_specs=[pl.BlockSpec((tm * TOP_K,), lambda i: (i,), memory_space=pltpu.SMEM),
                  pl.BlockSpec((tm * TOP_K,), lambda i: (jnp.minimum(i + 1, n - 1),),
                               memory_space=pltpu.SMEM),
                  pl.BlockSpec((tm, d), lambda i: (i, 0)),
                  pl.BlockSpec((ROW_CHUNKS * tm, LANES), lambda i: (i, 0)),
                  pl.BlockSpec((tm, TOP_K), lambda i: (i, 0)),
                  pl.BlockSpec((1, 1, d), lambda i: (i // tiles_per_batch, 0, 0)),
                  const2((d, de)), const2((d, de)), const2((de, d)), const2((1, d)),
                  pl.BlockSpec(memory_space=pl.ANY)],
        out_specs=pl.BlockSpec((tm, d), lambda i: (i, 0)),
        out_shape=jax.ShapeDtypeStruct((t, d), F32),
        scratch_shapes=[pltpu.VMEM((2, TOP_K, ROW_CHUNKS * tm, LANES), U32),
                        pltpu.SemaphoreType.DMA((2,))],
        compiler_params=_cparams(("arbitrary",)),
        name="moe_combine",
    )(dest_rows, dest_rows, x1, h2p, wts, g2, sw1, sw3, sw2, fg, ys)


def _route_meta(counts, slot_e, slot_r, block_rows):
    t = slot_e.shape[1]
    ends = jnp.cumsum(counts)
    starts = ends - counts
    experts = jnp.arange(N_EXPERTS, dtype=jnp.int32)
    dest_t = slot_r + jnp.sum(jnp.where(slot_e[:, :, None] == experts, starts, 0), axis=-1)

    n_blk = t * TOP_K // block_rows
    n_items = n_blk + N_EXPERTS - 1
    first_blk = starts // block_rows
    last_blk = (ends - 1) // block_rows
    n_e = jnp.where(counts > 0, last_blk - first_blk + 1, 0)
    item_end = jnp.cumsum(n_e)
    item_off = item_end - n_e
    total = item_end[-1]
    idx = jnp.arange(n_items, dtype=jnp.int32)
    e_of = jnp.minimum(jnp.sum(item_end[None, :] <= idx[:, None], axis=1), N_EXPERTS - 1).astype(jnp.int32)
    onehot = e_of[:, None] == experts[None, :]
    lut = lambda table: jnp.sum(jnp.where(onehot, table[None, :], 0), axis=1)
    live = idx < total
    blk = lut(first_blk) + (idx - lut(item_off))
    lo = jnp.maximum(lut(starts), blk * block_rows)
    hi = jnp.minimum(lut(ends), (blk + 1) * block_rows)
    last_e = jnp.sum(jnp.where(idx == total - 1, e_of, 0))
    iblk = jnp.where(live, blk, n_blk - 1).astype(jnp.int32)
    iexp = jnp.where(live, e_of, last_e).astype(jnp.int32)
    ilo = jnp.where(live, lo, 0).astype(jnp.int32)
    ihi = jnp.where(live, hi, 0).astype(jnp.int32)
    return dest_t, iblk, iexp, ilo, ihi


def _rope_tables(n_tokens):
    n_rows = n_tokens // GRID_W
    rows = np.repeat(np.arange(n_rows, dtype=np.float32), GRID_W)
    cols = np.tile(np.arange(GRID_W, dtype=np.float32), n_rows)
    n_freq = HEAD_DIM // 4
    inv_freq = (np.float32(ROPE_BASE) ** (-np.arange(n_freq, dtype=np.float32) / n_freq)).astype(np.float32)
    ar = rows[:, None] * inv_freq
    ac = cols[:, None] * inv_freq
    cos64 = np.concatenate([np.cos(ar), np.cos(ar), np.cos(ac), np.cos(ac)], axis=1)
    sin64 = np.concatenate([-np.sin(ar), np.sin(ar), -np.sin(ac), np.sin(ac)], axis=1)
    return np.tile(cos64, (1, 2)).astype(np.float32), np.tile(sin64, (1, 2)).astype(np.float32)


def kernel(x, c, ctx, c_ctx, ada_w, ada_b, norm1_g, w_in, lambda_q1, lambda_k1, lambda_q2, lambda_k2,
           dattn_norm_g, ret_decay_fwd, ret_decay_bwd, ret_norm_g, w_out, norm2_g, router_w, router_bias,
           exp_w1, exp_w3, exp_w2, shared_w1, shared_w3, shared_w2, final_norm_g):
    b, s, d = x.shape
    lc = ctx.shape[1]
    depth = ada_w.shape[0]
    assert depth == 1, "single-layer problem: the context stream is never updated"
    l = 0
    lam_init = 0.8 - 0.6 * math.exp(-0.3 * l)

    c8 = jnp.zeros((8, d), F32).at[:b].set(c).at[b].set(c_ctx)
    mod = _modulation(c8, ada_w[l], ada_b[l])
    sh1, sc1, g1, sh2, sc2, g2 = [mod[:b, k * d:(k + 1) * d].reshape(b, 1, d) for k in range(6)]
    csh1 = jnp.broadcast_to(mod[b, 0:d].reshape(1, 1, d), (b, 1, d))
    csc1 = jnp.broadcast_to(mod[b, d:2 * d].reshape(1, 1, d), (b, 1, d))

    w_l = w_in[l]
    w_tok = jnp.concatenate([w_l[:, 512:1024], w_l[:, 1536:3072]], axis=1).astype(BF16)
    w_feat = jnp.concatenate([w_l[:, 0:512], w_l[:, 1024:1536]], axis=1).T.astype(BF16)
    n1g = norm1_g[l].reshape(1, d)
    cos, sin = _rope_tables(s)
    ones_c = jnp.ones((lc, LANES), F32)
    zeros_c = jnp.zeros((lc, LANES), F32)
    tiles = _tile_plan(s)
    px, qt, vtx = _inproj(x, n1g, sc1, sh1, cos, sin, np.ascontiguousarray(cos.T),
                          np.ascontiguousarray(sin.T), w_tok, w_feat, tm=tiles.proj_tokens)
    pc, _, vtc = _inproj(ctx, n1g, csc1, csh1, ones_c, zeros_c, ones_c.T, zeros_c.T, w_tok, w_feat, tm=lc)

    lam = (jnp.exp(jnp.sum(lambda_q1[l] * lambda_k1[l])) - jnp.exp(jnp.sum(lambda_q2[l] * lambda_k2[l]))
           ).astype(F32) + lam_init
    att = _attention(lam.reshape(1), qt, px, vtx, pc, vtc, tq=tiles.attn_queries, tkc=tiles.attn_keys)

    dmat, qdec, kdec, gch = _retention_tables(ret_decay_fwd[l], ret_decay_bwd[l])
    ret_f, ret_b = _retention(px, pc, dmat, qdec, kdec, gch, chunks_per_step=tiles.ret_chunks)

    x1, h2p, slot_e, slot_w, slot_r, counts = _merge(
        att, ret_f, ret_b, px, x, g1, sh2, sc2,
        dattn_norm_g[l].reshape(1, 2 * HEAD_DIM), ret_norm_g[l].reshape(1, R_DV),
        w_out[l].astype(BF16), norm2_g[l].reshape(1, d), router_w[l].T.astype(BF16),
        router_bias[l].astype(F32).reshape(N_EXPERTS, 1), 1.0 - lam_init, tm=tiles.proj_tokens)

    t = b * s
    dest_t, iblk, iexp, ilo, ihi = _route_meta(counts[:, 0].astype(jnp.int32), slot_e, slot_r,
                                               tiles.expert_rows)
    dest_rows = (dest_t.T * ROW_CHUNKS).reshape(t * TOP_K)
    xs = _dispatch(dest_rows, h2p, tm=tiles.moe_tokens)
    ys = _experts(iblk, iexp, ilo, ihi, xs, exp_w1[l], exp_w3[l], exp_w2[l], tiles.expert_rows)
    out = _combine(dest_rows, x1.reshape(t, d), h2p, slot_w.T, g2, shared_w1[l].astype(BF16),
                   shared_w3[l].astype(BF16), shared_w2[l].astype(BF16), final_norm_g.reshape(1, d),
                   ys, s // tiles.moe_tokens, tiles.moe_tokens)
    return out.reshape(b, s, d)
```

```python
import functools
import math
from typing import NamedTuple

import jax
import jax.numpy as jnp
import numpy as np
from jax import lax
from jax.experimental import pallas as pl
from jax.experimental.pallas import tpu as pltpu

F32 = jnp.float32
BF16 = jnp.bfloat16
U32 = jnp.uint32

D_MODEL = 1024
GRID_W = 64
HEAD_DIM = 64
A_HEADS = 4
R_HEADS = 4
R_DK = 64
R_DV = 128
RET_CHUNK = 128
ROPE_BASE = 10000.0
N_EXPERTS = 64
N_GROUPS = 8
TOPK_GROUPS = 4
TOP_K = 8
D_EXPERT = 256
ROUTED_SCALE = 2.5
NORM_EPS = 1e-6
LANES = 128

P_KZ = 0
P_RQ = 1024
P_RK = 1280
P_RV = 1536
P_RG = 2048
P_COLS = 2560
VT_ROWS = 144
LOG2E = 1.4426950408889634

VMEM_LIMIT = 56 * 1024 * 1024


class _Tiles(NamedTuple):
    proj_tokens: int
    attn_queries: int
    attn_keys: int
    ret_chunks: int
    moe_tokens: int
    expert_rows: int


def _tile_plan(seq_len):
    return _Tiles(proj_tokens=min(512, seq_len), attn_queries=min(2048, seq_len),
                  attn_keys=min(512, seq_len // 4), ret_chunks=min(8, seq_len // RET_CHUNK),
                  moe_tokens=128, expert_rows=512)


def _cparams(sem):
    return pltpu.CompilerParams(dimension_semantics=sem, vmem_limit_bytes=VMEM_LIMIT)


def _mod_kernel(c_ref, w_ref, b_ref, o_ref):
    c = c_ref[...]
    s = c * jax.nn.sigmoid(c)
    o_ref[...] = jnp.dot(s, w_ref[...], preferred_element_type=F32,
                         precision=lax.Precision.HIGHEST) + b_ref[...]


def _modulation(c8, ada_w, ada_b):
    d = c8.shape[1]
    n = ada_w.shape[1]
    tn = 512
    return pl.pallas_call(
        _mod_kernel,
        grid=(n // tn,),
        in_specs=[pl.BlockSpec((8, d), lambda j: (0, 0)),
                  pl.BlockSpec((d, tn), lambda j: (0, j)),
                  pl.BlockSpec((1, tn), lambda j: (0, j))],
        out_specs=pl.BlockSpec((8, tn), lambda j: (0, j)),
        out_shape=jax.ShapeDtypeStruct((8, n), F32),
        compiler_params=_cparams(("arbitrary",)),
        name="modulation",
    )(c8, ada_w, ada_b.reshape(1, n))


def _rmsnorm_rows(x, g):
    return x * lax.rsqrt(jnp.mean(x * x, axis=-1, keepdims=True) + NORM_EPS) * g


def _inproj_kernel(x_ref, g_ref, sc_ref, sh_ref, cos_ref, sin_ref, cost_ref, sint_ref, w_ref, wt_ref,
                   p_ref, qt_ref, vt_ref):
    x = x_ref[0]
    tm = x.shape[0]
    h = _rmsnorm_rows(x, g_ref[...]) * (1.0 + sc_ref[0]) + sh_ref[0]
    hb = h.astype(BF16)
    p = jnp.dot(hb, w_ref[...], preferred_element_type=F32)
    pt = lax.dot_general(wt_ref[...], hb, _NT, preferred_element_type=F32)

    cos = cos_ref[...]
    sin = sin_ref[...]
    lane = lax.broadcasted_iota(jnp.int32, (tm, LANES), 1)
    first_half = (lane % 32) < 16
    low64 = lane < 64

    def rope(xs):
        partner = jnp.where(first_half, pltpu.roll(xs, LANES - 16, 1), pltpu.roll(xs, 16, 1))
        return xs * cos + partner * sin

    zero = jnp.zeros((tm, LANES), BF16)
    for hh in range(A_HEADS):
        k = rope(p[:, hh * 128:(hh + 1) * 128]).astype(BF16)
        p_ref[0, :, P_KZ + hh * 256:P_KZ + hh * 256 + 128] = jnp.where(low64, k, zero)
        p_ref[0, :, P_KZ + hh * 256 + 128:P_KZ + (hh + 1) * 256] = jnp.where(low64, zero, k)
    for cc in range(2):
        rq = rope(p[:, 512 + cc * 128:512 + (cc + 1) * 128])
        p_ref[0, :, P_RQ + cc * 128:P_RQ + (cc + 1) * 128] = rq.astype(BF16)
        rk = rope(p[:, 768 + cc * 128:768 + (cc + 1) * 128]) * (R_DK ** -0.5)
        p_ref[0, :, P_RK + cc * 128:P_RK + (cc + 1) * 128] = rk.astype(BF16)
    p_ref[0, :, P_RV:P_RV + 512] = p[:, 1024:1536].astype(BF16)
    p_ref[0, :, P_RG:P_RG + 512] = p[:, 1536:2048].astype(BF16)

    cost = cost_ref[...]
    sint = sint_ref[...]
    q_scale = (HEAD_DIM ** -0.5) * LOG2E
    ones_rows = jnp.where(lax.broadcasted_iota(jnp.int32, (VT_ROWS - 128, tm), 0) == 0, 1.0, 0.0)
    for hh in range(A_HEADS):
        xq = pt[hh * 128:(hh + 1) * 128, :]
        groups = [xq[16 * gi:16 * (gi + 1), :] for gi in range(8)]
        partner = jnp.concatenate([groups[gi ^ 1] for gi in range(8)], axis=0)
        qt_ref[0, hh] = ((xq * cost + partner * sint) * q_scale).astype(BF16)
        vt_ref[0, hh, 0:128, :] = pt[512 + hh * 128:512 + (hh + 1) * 128, :].astype(BF16)
        vt_ref[0, hh, 128:VT_ROWS, :] = ones_rows.astype(BF16)


def _inproj(x, g, sc, sh, cos, sin, cost, sint, w_bf, wt_bf, tm):
    b, s, d = x.shape
    n_tok = w_bf.shape[1]
    n_t = wt_bf.shape[0]
    const2 = lambda shape: pl.BlockSpec(shape, lambda bi, i: (0, 0))
    per_b = pl.BlockSpec((1, 1, d), lambda bi, i: (bi, 0, 0))
    return pl.pallas_call(
        _inproj_kernel,
        grid=(b, s // tm),
        in_specs=[pl.BlockSpec((1, tm, d), lambda bi, i: (bi, i, 0)),
                  const2((1, d)), per_b, per_b,
                  pl.BlockSpec((tm, LANES), lambda bi, i: (i, 0)),
                  pl.BlockSpec((tm, LANES), lambda bi, i: (i, 0)),
                  pl.BlockSpec((LANES, tm), lambda bi, i: (0, i)),
                  pl.BlockSpec((LANES, tm), lambda bi, i: (0, i)),
                  const2((d, n_tok)), const2((n_t, d))],
        out_specs=[pl.BlockSpec((1, tm, P_COLS), lambda bi, i: (bi, i, 0)),
                   pl.BlockSpec((1, A_HEADS, LANES, tm), lambda bi, i: (bi, 0, 0, i)),
                   pl.BlockSpec((1, A_HEADS, VT_ROWS, tm), lambda bi, i: (bi, 0, 0, i))],
        out_shape=[jax.ShapeDtypeStruct((b, s, P_COLS), BF16),
                   jax.ShapeDtypeStruct((b, A_HEADS, LANES, s), BF16),
                   jax.ShapeDtypeStruct((b, A_HEADS, VT_ROWS, s), BF16)],
        compiler_params=_cparams(("arbitrary", "arbitrary")),
        name="inproj",
    )(x, g, sc, sh, cos, sin, cost, sint, w_bf, wt_bf)


_NT = (((1,), (1,)), ((), ()))


def _attn_kernel(lam_ref, qt_ref, kzc_ref, vtc_ref, kzx_ref, vtx_ref, o_ref,
                 acc, s_a, s_b, e_a, e_b, *, tkc):
    qt = qt_ref[0, 0]
    tq = qt.shape[1]
    n_chunks = kzx_ref.shape[1] // tkc
    assert n_chunks % 2 == 0 and n_chunks >= 4

    def scores(kz, s_ref):
        for mp in range(2):
            s_ref[mp] = jnp.dot(kz[:, mp * LANES:(mp + 1) * LANES], qt, preferred_element_type=F32)

    def softmax(s_ref, e_ref, ms):
        new_ms, alphas = [], []
        for mp in range(2):
            st = s_ref[mp]
            m_new = jnp.maximum(ms[mp], jnp.max(st, axis=0, keepdims=True))
            alphas.append(jnp.exp2(ms[mp] - m_new))
            e_ref[mp] = jnp.exp2(st - m_new).astype(BF16)
            new_ms.append(m_new)
        return tuple(new_ms), tuple(alphas)

    def values(vt, e_ref, alphas):
        for mp in range(2):
            acc[mp] = alphas[mp] * acc[mp] + jnp.dot(vt, e_ref[mp], preferred_element_type=F32)

    def kchunk(c):
        return kzx_ref[0, pl.ds(pl.multiple_of(c * tkc, tkc), tkc), :]

    def vchunk(c):
        return vtx_ref[0, 0, :, pl.ds(pl.multiple_of(c * tkc, tkc), tkc)]

    ms = []
    for mp in range(2):
        st = jnp.dot(kzc_ref[0, :, mp * LANES:(mp + 1) * LANES], qt, preferred_element_type=F32)
        m_new = jnp.max(st, axis=0, keepdims=True)
        acc[mp] = jnp.dot(vtc_ref[0, 0], jnp.exp2(st - m_new).astype(BF16), preferred_element_type=F32)
        ms.append(m_new)
    ms = tuple(ms)

    scores(kchunk(0), s_a)
    scores(kchunk(1), s_b)
    ms, al_a = softmax(s_a, e_a, ms)

    def body(j, carry):
        ms, al_a = carry
        scores(kchunk(2 * j + 2), s_a)
        ms, al_b = softmax(s_b, e_b, ms)
        values(vchunk(2 * j), e_a, al_a)
        scores(kchunk(2 * j + 3), s_b)
        ms, al_a = softmax(s_a, e_a, ms)
        values(vchunk(2 * j + 1), e_b, al_b)
        return ms, al_a

    ms, al_a = lax.fori_loop(0, n_chunks // 2 - 1, body, (ms, al_a))
    ms, al_b = softmax(s_b, e_b, ms)
    values(vchunk(n_chunks - 2), e_a, al_a)
    values(vchunk(n_chunks - 1), e_b, al_b)

    a1 = acc[0]
    a2 = acc[1]
    o1 = a1[:LANES, :] / a1[LANES:LANES + 1, :]
    o2 = a2[:LANES, :] / a2[LANES:LANES + 1, :]
    o_ref[0] = (o1 - lam_ref[0] * o2).T.astype(BF16)


def _attention(lam, qt, px, vtx, pc, vtc, tq, tkc):
    b, s, _ = px.shape
    lc = pc.shape[1]
    kern = functools.partial(_attn_kernel, tkc=tkc)
    return pl.pallas_call(
        kern,
        grid=(b, A_HEADS, s // tq),
        in_specs=[pl.BlockSpec(memory_space=pltpu.SMEM),
                  pl.BlockSpec((1, 1, LANES, tq), lambda bi, h, i: (bi, h, 0, i)),
                  pl.BlockSpec((1, lc, 256), lambda bi, h, i: (bi, 0, P_KZ // 256 + h)),
                  pl.BlockSpec((1, 1, VT_ROWS, lc), lambda bi, h, i: (bi, h, 0, 0)),
                  pl.BlockSpec((1, s, 256), lambda bi, h, i: (bi, 0, P_KZ // 256 + h)),
                  pl.BlockSpec((1, 1, VT_ROWS, s), lambda bi, h, i: (bi, h, 0, 0))],
        out_specs=pl.BlockSpec((1, tq, LANES), lambda bi, h, i: (bi, i, h)),
        out_shape=jax.ShapeDtypeStruct((b, s, A_HEADS * LANES), BF16),
        scratch_shapes=[pltpu.VMEM((2, VT_ROWS, tq), F32),
                        pltpu.VMEM((2, tkc, tq), F32), pltpu.VMEM((2, tkc, tq), F32),
                        pltpu.VMEM((2, tkc, tq), BF16), pltpu.VMEM((2, tkc, tq), BF16)],
        compiler_params=_cparams(("arbitrary", "arbitrary", "arbitrary")),
        name="diff_attention",
    )(lam, qt, pc, vtc, px, vtx)


_TN = (((0,), (0,)), ((), ()))


def _ret_kernel(qf_ref, kf_ref, vf_ref, qb_ref, kb_ref, vb_ref, kc_ref, vc_ref,
                dmat_ref, qdec_ref, kdec_ref, gch_ref, of_ref, ob_ref, sf, sb):
    c = RET_CHUNK
    n = pl.program_id(1)
    col_head = lax.broadcasted_iota(jnp.int32, (c, R_HEADS * R_DK), 1) // R_DK
    bd_mask = (lax.broadcasted_iota(jnp.int32, (R_HEADS * R_DK, R_HEADS * R_DV), 0) // R_DK
               == lax.broadcasted_iota(jnp.int32, (R_HEADS * R_DK, R_HEADS * R_DV), 1) // R_DV)

    def state_update(s_ref, k, v, kdec, gch):
        kd = (k.astype(F32) * kdec).astype(BF16)
        kv = lax.dot_general(kd, v, _TN, preferred_element_type=F32)
        s_ref[...] = gch * s_ref[...] + jnp.where(bd_mask, kv, 0.0)

    def chunk(s_ref, q, k, v, dmat, qdec, kdec, gch):
        zero = jnp.zeros_like(k)
        kblk = jnp.concatenate([jnp.where(col_head == h, k, zero) for h in range(R_HEADS)], axis=0)
        sc = lax.dot_general(q, kblk, _NT, preferred_element_type=F32) * dmat
        p = sc.astype(BF16)
        intra = jnp.concatenate(
            [jnp.dot(p[:, h * c:(h + 1) * c], v[:, h * R_DV:(h + 1) * R_DV], preferred_element_type=F32)
             for h in range(R_HEADS)], axis=1)
        qd = (q.astype(F32) * qdec).astype(BF16)
        cross = jnp.dot(qd, s_ref[...].astype(BF16), preferred_element_type=F32)
        out = intra + cross
        state_update(s_ref, k, v, kdec, gch)
        return out

    @pl.when(n == 0)
    def _():
        sf[...] = jnp.zeros_like(sf)
        sb[...] = jnp.zeros_like(sb)
        n_ctx = kc_ref.shape[1] // c
        for cc in range(n_ctx):
            state_update(sf, kc_ref[0, cc * c:(cc + 1) * c, :], vc_ref[0, cc * c:(cc + 1) * c, :],
                         kdec_ref[0], gch_ref[0])
        for cc in reversed(range(n_ctx)):
            state_update(sb, kc_ref[0, cc * c:(cc + 1) * c, :], vc_ref[0, cc * c:(cc + 1) * c, :],
                         kdec_ref[1], gch_ref[1])

    chunks_per_step = qf_ref.shape[1] // c
    for cc in range(chunks_per_step):
        rows = slice(cc * c, (cc + 1) * c)
        of_ref[0, rows, :] = chunk(sf, qf_ref[0, rows, :], kf_ref[0, rows, :], vf_ref[0, rows, :], dmat_ref[0],
                                   qdec_ref[0], kdec_ref[0], gch_ref[0]).astype(BF16)
    for cc in reversed(range(chunks_per_step)):
        rows = slice(cc * c, (cc + 1) * c)
        ob_ref[0, rows, :] = chunk(sb, qb_ref[0, rows, :], kb_ref[0, rows, :], vb_ref[0, rows, :], dmat_ref[1],
                                   qdec_ref[1], kdec_ref[1], gch_ref[1]).astype(BF16)


def _retention(px, pc, dmat, qdec, kdec, gch, chunks_per_step):
    b, s, _ = px.shape
    lc = pc.shape[1]
    c = RET_CHUNK
    rows = c * chunks_per_step
    nch = s // rows
    qk_w = R_HEADS * R_DK
    v_w = R_HEADS * R_DV
    fwd = lambda bi, n: (bi, n, 0)
    spec_q = lambda im: pl.BlockSpec((1, rows, qk_w), lambda bi, n: im(bi, n)[:2] + (P_RQ // qk_w,))
    spec_k = lambda im: pl.BlockSpec((1, rows, qk_w), lambda bi, n: im(bi, n)[:2] + (P_RK // qk_w,))
    spec_v = lambda im: pl.BlockSpec((1, rows, v_w), lambda bi, n: im(bi, n)[:2] + (P_RV // v_w,))
    bwd = lambda bi, n: (bi, nch - 1 - n, 0)
    const3 = lambda shape: pl.BlockSpec(shape, lambda bi, n: (0, 0, 0))
    return pl.pallas_call(
        _ret_kernel,
        grid=(b, nch),
        in_specs=[spec_q(fwd), spec_k(fwd), spec_v(fwd), spec_q(bwd), spec_k(bwd), spec_v(bwd),
                  pl.BlockSpec((1, lc, qk_w), lambda bi, n: (bi, 0, P_RK // qk_w)),
                  pl.BlockSpec((1, lc, v_w), lambda bi, n: (bi, 0, P_RV // v_w)),
                  const3((2, c, R_HEADS * c)), const3((2, c, qk_w)), const3((2, c, qk_w)),
                  const3((2, 1, v_w))],
        out_specs=[pl.BlockSpec((1, rows, v_w), fwd), pl.BlockSpec((1, rows, v_w), bwd)],
        out_shape=[jax.ShapeDtypeStruct((b, s, v_w), BF16), jax.ShapeDtypeStruct((b, s, v_w), BF16)],
        scratch_shapes=[pltpu.VMEM((qk_w, v_w), F32), pltpu.VMEM((qk_w, v_w), F32)],
        compiler_params=_cparams(("arbitrary", "arbitrary")),
        name="retention",
    )(px, px, px, px, px, px, pc, pc, dmat, qdec, kdec, gch)


def _retention_tables(decay_fwd, decay_bwd):
    c = RET_CHUNK
    pos = jnp.arange(c, dtype=F32)
    rel = pos[:, None] - pos[None, :]

    def tables(logits, backward):
        lg = jax.nn.log_sigmoid(logits.astype(F32))
        r = -rel if backward else rel
        dm = jnp.where(r[None] >= 0, jnp.exp(jnp.maximum(r, 0.0)[None] * lg[:, None, None]), 0.0)
        dm = jnp.transpose(dm, (1, 0, 2)).reshape(c, R_HEADS * c)
        qpow = (c - pos) if backward else (pos + 1)
        kpow = pos if backward else (c - 1 - pos)
        qd = jnp.repeat(jnp.exp(qpow[:, None] * lg[None, :]), R_DK, axis=1)
        kd = jnp.repeat(jnp.exp(kpow[:, None] * lg[None, :]), R_DK, axis=1)
        gc = jnp.repeat(jnp.exp(c * lg), R_DV)[None, :]
        return dm, qd, kd, gc

    tf = tables(decay_fwd, False)
    tb = tables(decay_bwd, True)
    return tuple(jnp.stack([a, bb]) for a, bb in zip(tf, tb))


def _pack_pairs(x):
    k = x.shape[1] // 2
    lo = lax.bitcast_convert_type(x[:, :k].astype(BF16).astype(F32), U32)
    hi = lax.bitcast_convert_type(x[:, k:].astype(BF16).astype(F32), U32)
    return (lo >> 16) | (hi & jnp.uint32(0xFFFF0000))


def _unpack_pairs(w):
    lo = lax.bitcast_convert_type(w << 16, F32)
    hi = lax.bitcast_convert_type(w & jnp.uint32(0xFFFF0000), F32)
    return lo, hi


ROW_CHUNKS = (D_MODEL // 2) // LANES


def _store_token_rows(ref, first_token, n_tokens, x):
    packed = _pack_pairs(x)
    for c in range(ROW_CHUNKS):
        ref[pl.ds(first_token * ROW_CHUNKS + c, n_tokens, stride=ROW_CHUNKS), :] = (
            packed[:, c * LANES:(c + 1) * LANES])


def _load_token_rows(ref, first_token, n_tokens):
    return [_unpack_pairs(ref[pl.ds(first_token * ROW_CHUNKS + c, n_tokens, stride=ROW_CHUNKS), :])
            for c in range(ROW_CHUNKS)]


def _rows_to_features(chunks):
    return jnp.concatenate([lo for lo, _ in chunks] + [hi for _, hi in chunks], axis=1)


def _route_tile(logits_t, bias_col, carry_ref):
    n_e, tm = logits_t.shape
    per = n_e // N_GROUPS
    neg = -jnp.inf
    scores = jax.nn.sigmoid(logits_t)
    sel = scores + bias_col
    sub = lax.broadcasted_iota(jnp.int32, (per, tm), 0).astype(F32)
    sel_g = [sel[g * per:(g + 1) * per, :] for g in range(N_GROUPS)]
    flat_g = [sub + float(g * per) for g in range(N_GROUPS)]

    gs = []
    for g in range(N_GROUPS):
        top1 = jnp.max(sel_g[g], axis=0, keepdims=True)
        idx1 = jnp.min(jnp.where(sel_g[g] == top1, sub, float(per)), axis=0, keepdims=True)
        top2 = jnp.max(jnp.where(sub == idx1, neg, sel_g[g]), axis=0, keepdims=True)
        gs.append(top1 + top2)
    gsel = [jnp.zeros((1, tm), F32) for _ in range(N_GROUPS)]
    for _ in range(TOPK_GROUPS):
        m = functools.reduce(jnp.maximum, gs)
        found = jnp.zeros((1, tm), F32)
        for g in range(N_GROUPS):
            hit = jnp.logical_and(gs[g] == m, found == 0.0)
            found = jnp.where(hit, 1.0, found)
            gsel[g] = jnp.where(hit, 1.0, gsel[g])
            gs[g] = jnp.where(hit, neg, gs[g])

    masked = [jnp.where(gsel[g] > 0.0, sel_g[g], neg) for g in range(N_GROUPS)]
    chosen = [jnp.zeros((per, tm), F32) for _ in range(N_GROUPS)]
    for _ in range(TOP_K):
        m = functools.reduce(jnp.maximum, [jnp.max(mg, axis=0, keepdims=True) for mg in masked])
        idx = functools.reduce(jnp.minimum, [
            jnp.min(jnp.where(masked[g] == m, flat_g[g], float(n_e)), axis=0, keepdims=True)
            for g in range(N_GROUPS)])
        for g in range(N_GROUPS):
            hit = flat_g[g] == idx
            chosen[g] = jnp.where(hit, 1.0, chosen[g])
            masked[g] = jnp.where(hit, neg, masked[g])
    chosen = jnp.concatenate(chosen, axis=0)
    flat = jnp.concatenate(flat_g, axis=0)

    w_dense = scores * chosen
    w_dense = w_dense / jnp.sum(w_dense, axis=0, keepdims=True) * ROUTED_SCALE

    chosen_bf = chosen.astype(BF16)
    lower = (lax.broadcasted_iota(jnp.int32, (n_e, n_e), 1)
             < lax.broadcasted_iota(jnp.int32, (n_e, n_e), 0)).astype(BF16)
    slot_idx = jnp.dot(lower, chosen_bf, preferred_element_type=F32)
    upper = (lax.broadcasted_iota(jnp.int32, (tm, tm), 0)
             < lax.broadcasted_iota(jnp.int32, (tm, tm), 1)).astype(BF16)
    rank = jnp.dot(chosen_bf, upper, preferred_element_type=F32) + carry_ref[...]
    carry_ref[...] += jnp.sum(chosen, axis=1, keepdims=True)

    slots = []
    for j in range(TOP_K):
        ind = jnp.where(slot_idx == float(j), chosen, 0.0)
        slots.append((jnp.sum(ind * flat, axis=0, keepdims=True),
                      jnp.sum(ind * w_dense, axis=0, keepdims=True),
                      jnp.sum(ind * rank, axis=0, keepdims=True)))
    return slots


def _merge_kernel(att_ref, rf_ref, rb_ref, gate_ref, x_ref, g1_ref, sh2_ref, sc2_ref, ga_ref, gr_ref,
                  wout_ref, n2g_ref, rwt_ref, rbias_ref, x1_ref, h2p_ref, se_ref, sw_ref, sr_ref, cnt_ref,
                  carry, *, att_scale):
    @pl.when(jnp.logical_and(pl.program_id(0) == 0, pl.program_id(1) == 0))
    def _():
        carry[...] = jnp.zeros_like(carry)

    att = att_ref[0].astype(F32)
    ret = rf_ref[0].astype(F32) + rb_ref[0].astype(F32)
    gate = gate_ref[0].astype(F32)
    parts = []
    for h in range(A_HEADS):
        a = att[:, h * 128:(h + 1) * 128]
        parts.append(_rmsnorm_rows(a, ga_ref[...]) * att_scale)
    for h in range(R_HEADS):
        r = _rmsnorm_rows(ret[:, h * R_DV:(h + 1) * R_DV], gr_ref[...])
        g = gate[:, h * R_DV:(h + 1) * R_DV]
        parts.append(g * jax.nn.sigmoid(g) * r)
    merged = jnp.concatenate(parts, axis=1).astype(BF16)
    y = jnp.dot(merged, wout_ref[...], preferred_element_type=F32)
    x1 = x_ref[0] + g1_ref[0] * y
    x1_ref[0] = x1
    h2 = _rmsnorm_rows(x1, n2g_ref[...]) * (1.0 + sc2_ref[0]) + sh2_ref[0]
    _store_token_rows(h2p_ref, 0, h2.shape[0], h2)
    logits_t = lax.dot_general(rwt_ref[...], h2.astype(BF16), _NT, preferred_element_type=F32)
    slots = _route_tile(logits_t, rbias_ref[...], carry)
    for j, (e_j, w_j, r_j) in enumerate(slots):
        se_ref[j:j + 1, :] = e_j.astype(jnp.int32)
        sw_ref[j:j + 1, :] = w_j
        sr_ref[j:j + 1, :] = r_j.astype(jnp.int32)
    cnt_ref[...] = carry[...]


def _merge(att, rf, rb, px, x, g1, sh2, sc2, ga, gr, wout_bf, n2g, rwt_bf, rbias, att_scale, tm):
    b, s, d = x.shape
    nt = s // tm
    chunked = pl.BlockSpec((ROW_CHUNKS * tm, LANES), lambda bi, i: (bi * nt + i, 0))
    chunked_shape = jax.ShapeDtypeStruct((b * s * ROW_CHUNKS, LANES), U32)
    row = lambda bi, i: (bi, i, 0)
    tok = lambda bi, i: (0, bi * nt + i)
    per_b = pl.BlockSpec((1, 1, d), lambda bi, i: (bi, 0, 0))
    const2 = lambda shape: pl.BlockSpec(shape, lambda bi, i: (0, 0))
    return pl.pallas_call(
        functools.partial(_merge_kernel, att_scale=att_scale),
        grid=(b, nt),
        in_specs=[pl.BlockSpec((1, tm, 512), row), pl.BlockSpec((1, tm, 512), row),
                  pl.BlockSpec((1, tm, 512), row),
                  pl.BlockSpec((1, tm, 512), lambda bi, i: (bi, i, P_RG // 512)),
                  pl.BlockSpec((1, tm, d), row), per_b, per_b, per_b,
                  const2((1, 128)), const2((1, R_DV)), const2((d, d)), const2((1, d)),
                  const2((N_EXPERTS, d)), const2((N_EXPERTS, 1))],
        out_specs=[pl.BlockSpec((1, tm, d), row), chunked,
                   pl.BlockSpec((TOP_K, tm), tok), pl.BlockSpec((TOP_K, tm), tok),
                   pl.BlockSpec((TOP_K, tm), tok), const2((N_EXPERTS, 1))],
        out_shape=[jax.ShapeDtypeStruct((b, s, d), F32), chunked_shape,
                   jax.ShapeDtypeStruct((TOP_K, b * s), jnp.int32),
                   jax.ShapeDtypeStruct((TOP_K, b * s), F32),
                   jax.ShapeDtypeStruct((TOP_K, b * s), jnp.int32),
                   jax.ShapeDtypeStruct((N_EXPERTS, 1), F32)],
        scratch_shapes=[pltpu.VMEM((N_EXPERTS, 1), F32)],
        compiler_params=_cparams(("arbitrary", "arbitrary")),
        name="mixer_merge",
    )(att, rf, rb, px, x, g1, sh2, sc2, ga, gr, wout_bf, n2g, rwt_bf, rbias)


def _dispatch_kernel(dest_ref, h_ref, xs_ref, buf, sem):
    rows = h_ref.shape[0]
    tm = rows // ROW_CHUNKS
    i = pl.program_id(0)
    n = pl.num_programs(0)
    slot = i % 2

    def wait_slot(s):
        for _ in range(TOP_K):
            pltpu.make_async_copy(buf.at[s], xs_ref.at[pl.ds(0, rows)], sem.at[s]).wait()

    @pl.when(i >= 2)
    def _():
        wait_slot(slot)

    buf[slot] = h_ref[...]

    def body(t, carry):
        src = buf.at[slot, pl.ds(pl.multiple_of(t * ROW_CHUNKS, ROW_CHUNKS), ROW_CHUNKS)]
        for j in range(TOP_K):
            d = pl.multiple_of(dest_ref[t * TOP_K + j], ROW_CHUNKS)
            pltpu.make_async_copy(src, xs_ref.at[pl.ds(d, ROW_CHUNKS)], sem.at[slot]).start(priority=j % 2)
        return carry

    lax.fori_loop(0, tm, body, 0)

    @pl.when(i == n - 1)
    def _():
        wait_slot(slot)

    @pl.when(jnp.logical_and(i == n - 1, n >= 2))
    def _():
        wait_slot(1 - slot)


def _dispatch(dest_rows, h2p, tm):
    n_rows = h2p.shape[0]
    n = n_rows // (ROW_CHUNKS * tm)
    return pl.pallas_call(
        _dispatch_kernel,
        grid=(n,),
        in_specs=[pl.BlockSpec((tm * TOP_K,), lambda i: (i,), memory_space=pltpu.SMEM),
                  pl.BlockSpec((ROW_CHUNKS * tm, LANES), lambda i: (i, 0))],
        out_specs=pl.BlockSpec(memory_space=pl.ANY),
        out_shape=jax.ShapeDtypeStruct((n_rows * TOP_K, LANES), U32),
        scratch_shapes=[pltpu.VMEM((2, ROW_CHUNKS * tm, LANES), U32), pltpu.SemaphoreType.DMA((2,))],
        compiler_params=_cparams(("arbitrary",)),
        name="moe_dispatch",
    )(dest_rows, h2p)


def _expert_kernel(iblk_ref, iexp_ref, ilo_ref, ihi_ref, xs_ref, w1_ref, w3_ref, w2_ref, ys_ref,
                   acc, hm, w1b, w3b, w2b):
    r = xs_ref.shape[0] // ROW_CHUNKS
    i = pl.program_id(0)
    n_items = pl.num_programs(0) - 1
    cur = jnp.minimum(i, n_items - 1)
    cur_prev = jnp.maximum(cur - 1, 0)
    dn = jnp.maximum(i - 1, 0)
    dn_prev = jnp.maximum(i - 2, 0)
    blk = iblk_ref[cur]
    lo = ilo_ref[cur]
    hi = ihi_ref[cur]
    dn_first = jnp.logical_or(i <= 1, iblk_ref[dn_prev] != iblk_ref[dn])
    slot = i % 2

    @pl.when(i == 0)
    def _():
        acc[...] = jnp.zeros_like(acc)
        hm[...] = jnp.zeros_like(hm)

    @pl.when(jnp.logical_or(i == 0, iexp_ref[cur_prev] != iexp_ref[cur]))
    def _():
        w1b[...] = w1_ref[0].astype(BF16)
        w3b[...] = w3_ref[0].astype(BF16)

    @pl.when(jnp.logical_or(i <= 1, iexp_ref[dn_prev] != iexp_ref[dn]))
    def _():
        w2b[...] = w2_ref[0].astype(BF16)

    n_half = 4
    rh = r // n_half

    def run(hm_rd, hm_wr):
        for hf in range(n_half):
            sl = slice(hf * rh, (hf + 1) * rh)
            y = jnp.dot(hm_rd[sl, :], w2b[...], preferred_element_type=F32)
            y = jnp.where(dn_first, y, acc[sl, :] + y)
            acc[sl, :] = y
            _store_token_rows(ys_ref, hf * rh, rh, y)

            rows = blk * r + hf * rh + lax.broadcasted_iota(jnp.int32, (rh, 1), 0)
            valid = jnp.logical_and(rows >= lo, rows < hi)
            x = _rows_to_features(_load_token_rows(xs_ref, hf * rh, rh))
            x = jnp.where(valid, x, 0.0).astype(BF16)
            a = jnp.dot(x, w1b[...], preferred_element_type=F32)
            bb = jnp.dot(x, w3b[...], preferred_element_type=F32)
            hm_wr[sl, :] = (a * jax.nn.sigmoid(a) * bb).astype(BF16)

    @pl.when(slot == 0)
    def _():
        run(hm.at[1], hm.at[0])

    @pl.when(slot == 1)
    def _():
        run(hm.at[0], hm.at[1])


def _experts(iblk, iexp, ilo, ihi, xs, w1, w3, w2, r):
    na, kw = xs.shape
    rows = r * ROW_CHUNKS
    n_items = iblk.shape[0]
    d = w1.shape[1]
    de = w1.shape[2]
    up = lambda i: jnp.minimum(i, n_items - 1)
    down = lambda i: jnp.maximum(i - 1, 0)
    grid_spec = pltpu.PrefetchScalarGridSpec(
        num_scalar_prefetch=4,
        grid=(n_items + 1,),
        in_specs=[pl.BlockSpec((rows, kw), lambda i, ib, ie, il, ih: (ib[up(i)], 0)),
                  pl.BlockSpec((1, d, de), lambda i, ib, ie, il, ih: (ie[up(i)], 0, 0)),
                  pl.BlockSpec((1, d, de), lambda i, ib, ie, il, ih: (ie[up(i)], 0, 0)),
                  pl.BlockSpec((1, de, d), lambda i, ib, ie, il, ih: (ie[down(i)], 0, 0))],
        out_specs=pl.BlockSpec((rows, kw), lambda i, ib, ie, il, ih: (ib[down(i)], 0)),
        scratch_shapes=[pltpu.VMEM((r, d), F32), pltpu.VMEM((2, r, de), BF16),
                        pltpu.VMEM((d, de), BF16), pltpu.VMEM((d, de), BF16), pltpu.VMEM((de, d), BF16)],
    )
    return pl.pallas_call(
        _expert_kernel,
        grid_spec=grid_spec,
        out_shape=jax.ShapeDtypeStruct((na, kw), U32),
        compiler_params=_cparams(("arbitrary",)),
        name="moe_experts",
    )(iblk, iexp, ilo, ihi, xs, w1, w3, w2)


def _combine_kernel(dest_ref, dnext_ref, x1_ref, h2p_ref, wts_ref, g2_ref, sw1_ref, sw3_ref, sw2_ref,
                    fg_ref, ys_ref, o_ref, gbuf, sem):
    tm = x1_ref.shape[0]
    i = pl.program_id(0)
    n = pl.num_programs(0)
    slot = i % 2

    rows = ROW_CHUNKS * tm

    def gather_row(dref, s, t, off, j):
        d = pl.multiple_of(dref[t * TOP_K + j], ROW_CHUNKS)
        pltpu.make_async_copy(ys_ref.at[pl.ds(d, ROW_CHUNKS)], gbuf.at[s, j, pl.ds(off, ROW_CHUNKS)],
                              sem.at[s]).start(priority=j % 2)

    def wait_slot(s):
        for _ in range(TOP_K):
            pltpu.make_async_copy(ys_ref.at[pl.ds(0, rows)], gbuf.at[s, 0], sem.at[s]).wait()

    @pl.when(i == 0)
    def _():
        def body(t, carry):
            for j in range(TOP_K):
                gather_row(dest_ref, 0, t, pl.multiple_of(t * ROW_CHUNKS, ROW_CHUNKS), j)
            return carry
        lax.fori_loop(0, tm, body, 0)

    wait_slot(slot)
    for t in range(tm):
        for j in range(TOP_K):
            gather_row(dnext_ref, 1 - slot, t, t * ROW_CHUNKS, j)

    wts = wts_ref[...]
    moe = None
    for j in range(TOP_K):
        wj = wts[:, j:j + 1]
        yj = [(lo * wj, hi * wj) for lo, hi in _load_token_rows(gbuf.at[slot, j], 0, tm)]
        moe = yj if moe is None else [(a + lo, b + hi) for (a, b), (lo, hi) in zip(moe, yj)]
    h = _rows_to_features(_load_token_rows(h2p_ref, 0, tm)).astype(BF16)
    a = jnp.dot(h, sw1_ref[...], preferred_element_type=F32)
    bb = jnp.dot(h, sw3_ref[...], preferred_element_type=F32)
    shared = jnp.dot((a * jax.nn.sigmoid(a) * bb).astype(BF16), sw2_ref[...], preferred_element_type=F32)
    ffn = _rows_to_features(moe) + shared
    xo = x1_ref[...] + g2_ref[0] * ffn
    o_ref[...] = _rmsnorm_rows(xo, fg_ref[...])

    @pl.when(i == n - 1)
    def _():
        wait_slot(1 - slot)


def _combine(dest_rows, x1, h2p, wts, g2, sw1, sw3, sw2, fg, ys, tiles_per_batch, tm):
    t, d = x1.shape
    n = t // tm
    de = sw1.shape[1]
    const2 = lambda shape: pl.BlockSpec(shape, lambda i: (0, 0))
    return pl.pallas_call(
        _combine_kernel,
        grid=(n,),
        in_specs=[pl.BlockSpec((tm * TOP_K,), lambda i: (i,), memory_space=pltpu.SMEM),
                  pl.BlockSpec((tm * TOP_K,), lambda i: (jnp.minimum(i + 1, n - 1),),
                               memory_space=pltpu.SMEM),
                  pl.BlockSpec((tm, d), lambda i: (i, 0)),
                  pl.BlockSpec((ROW_CHUNKS * tm, LANES), lambda i: (i, 0)),
                  pl.BlockSpec((tm, TOP_K), lambda i: (i, 0)),
                  pl.BlockSpec((1, 1, d), lambda i: (i // tiles_per_batch, 0, 0)),
                  const2((d, de)), const2((d, de)), const2((de, d)), const2((1, d)),
                  pl.BlockSpec(memory_space=pl.ANY)],
        out_specs=pl.BlockSpec((tm, d), lambda i: (i, 0)),
        out_shape=jax.ShapeDtypeStruct((t, d), F32),
        scratch_shapes=[pltpu.VMEM((2, TOP_K, ROW_CHUNKS * tm, LANES), U32),
                        pltpu.SemaphoreType.DMA((2,))],
        compiler_params=_cparams(("arbitrary",)),
        name="moe_combine",
    )(dest_rows, dest_rows, x1, h2p, wts, g2, sw1, sw3, sw2, fg, ys)


def _route_meta(counts, slot_e, slot_r, block_rows):
    t = slot_e.shape[1]
    ends = jnp.cumsum(counts)
    starts = ends - counts
    experts = jnp.arange(N_EXPERTS, dtype=jnp.int32)
    dest_t = slot_r + jnp.sum(jnp.where(slot_e[:, :, None] == experts, starts, 0), axis=-1)

    n_blk = t * TOP_K // block_rows
    n_items = n_blk + N_EXPERTS - 1
    first_blk = starts // block_rows
    last_blk = (ends - 1) // block_rows
    n_e = jnp.where(counts > 0, last_blk - first_blk + 1, 0)
    item_end = jnp.cumsum(n_e)
    item_off = item_end - n_e
    total = item_end[-1]
    idx = jnp.arange(n_items, dtype=jnp.int32)
    e_of = jnp.minimum(jnp.sum(item_end[None, :] <= idx[:, None], axis=1), N_EXPERTS - 1).astype(jnp.int32)
    onehot = e_of[:, None] == experts[None, :]
    lut = lambda table: jnp.sum(jnp.where(onehot, table[None, :], 0), axis=1)
    live = idx < total
    blk = lut(first_blk) + (idx - lut(item_off))
    lo = jnp.maximum(lut(starts), blk * block_rows)
    hi = jnp.minimum(lut(ends), (blk + 1) * block_rows)
    last_e = jnp.sum(jnp.where(idx == total - 1, e_of, 0))
    iblk = jnp.where(live, blk, n_blk - 1).astype(jnp.int32)
    iexp = jnp.where(live, e_of, last_e).astype(jnp.int32)
    ilo = jnp.where(live, lo, 0).astype(jnp.int32)
    ihi = jnp.where(live, hi, 0).astype(jnp.int32)
    return dest_t, iblk, iexp, ilo, ihi


def _rope_tables(n_tokens):
    n_rows = n_tokens // GRID_W
    rows = np.repeat(np.arange(n_rows, dtype=np.float32), GRID_W)
    cols = np.tile(np.arange(GRID_W, dtype=np.float32), n_rows)
    n_freq = HEAD_DIM // 4
    inv_freq = (np.float32(ROPE_BASE) ** (-np.arange(n_freq, dtype=np.float32) / n_freq)).astype(np.float32)
    ar = rows[:, None] * inv_freq
    ac = cols[:, None] * inv_freq
    cos64 = np.concatenate([np.cos(ar), np.cos(ar), np.cos(ac), np.cos(ac)], axis=1)
    sin64 = np.concatenate([-np.sin(ar), np.sin(ar), -np.sin(ac), np.sin(ac)], axis=1)
    return np.tile(cos64, (1, 2)).astype(np.float32), np.tile(sin64, (1, 2)).astype(np.float32)


def kernel(x, c, ctx, c_ctx, ada_w, ada_b, norm1_g, w_in, lambda_q1, lambda_k1, lambda_q2, lambda_k2,
           dattn_norm_g, ret_decay_fwd, ret_decay_bwd, ret_norm_g, w_out, norm2_g, router_w, router_bias,
           exp_w1, exp_w3, exp_w2, shared_w1, shared_w3, shared_w2, final_norm_g):
    b, s, d = x.shape
    lc = ctx.shape[1]
    depth = ada_w.shape[0]
    assert depth == 1, "single-layer problem: the context stream is never updated"
    l = 0
    lam_init = 0.8 - 0.6 * math.exp(-0.3 * l)

    c8 = jnp.zeros((8, d), F32).at[:b].set(c).at[b].set(c_ctx)
    mod = _modulation(c8, ada_w[l], ada_b[l])
    sh1, sc1, g1, sh2, sc2, g2 = [mod[:b, k * d:(k + 1) * d].reshape(b, 1, d) for k in range(6)]
    csh1 = jnp.broadcast_to(mod[b, 0:d].reshape(1, 1, d), (b, 1, d))
    csc1 = jnp.broadcast_to(mod[b, d:2 * d].reshape(1, 1, d), (b, 1, d))

    w_l = w_in[l]
    w_tok = jnp.concatenate([w_l[:, 512:1024], w_l[:, 1536:3072]], axis=1).astype(BF16)
    w_feat = jnp.concatenate([w_l[:, 0:512], w_l[:, 1024:1536]], axis=1).T.astype(BF16)
    n1g = norm1_g[l].reshape(1, d)
    cos, sin = _rope_tables(s)
    ones_c = jnp.ones((lc, LANES), F32)
    zeros_c = jnp.zeros((lc, LANES), F32)
    tiles = _tile_plan(s)
    px, qt, vtx = _inproj(x, n1g, sc1, sh1, cos, sin, np.ascontiguousarray(cos.T),
                          np.ascontiguousarray(sin.T), w_tok, w_feat, tm=tiles.proj_tokens)
    pc, _, vtc = _inproj(ctx, n1g, csc1, csh1, ones_c, zeros_c, ones_c.T, zeros_c.T, w_tok, w_feat, tm=lc)

    lam = (jnp.exp(jnp.sum(lambda_q1[l] * lambda_k1[l])) - jnp.exp(jnp.sum(lambda_q2[l] * lambda_k2[l]))
           ).astype(F32) + lam_init
    att = _attention(lam.reshape(1), qt, px, vtx, pc, vtc, tq=tiles.attn_queries, tkc=tiles.attn_keys)

    dmat, qdec, kdec, gch = _retention_tables(ret_decay_fwd[l], ret_decay_bwd[l])
    ret_f, ret_b = _retention(px, pc, dmat, qdec, kdec, gch, chunks_per_step=tiles.ret_chunks)

    x1, h2p, slot_e, slot_w, slot_r, counts = _merge(
        att, ret_f, ret_b, px, x, g1, sh2, sc2,
        dattn_norm_g[l].reshape(1, 2 * HEAD_DIM), ret_norm_g[l].reshape(1, R_DV),
        w_out[l].astype(BF16), norm2_g[l].reshape(1, d), router_w[l].T.astype(BF16),
        router_bias[l].astype(F32).reshape(N_EXPERTS, 1), 1.0 - lam_init, tm=tiles.proj_tokens)

    t = b * s
    dest_t, iblk, iexp, ilo, ihi = _route_meta(counts[:, 0].astype(jnp.int32), slot_e, slot_r,
                                               tiles.expert_rows)
    dest_rows = (dest_t.T * ROW_CHUNKS).reshape(t * TOP_K)
    xs = _dispatch(dest_rows, h2p, tm=tiles.moe_tokens)
    ys = _experts(iblk, iexp, ilo, ihi, xs, exp_w1[l], exp_w3[l], exp_w2[l], tiles.expert_rows)
    out = _combine(dest_rows, x1.reshape(t, d), h2p, slot_w.T, g2, shared_w1[l].astype(BF16),
                   shared_w3[l].astype(BF16), shared_w2[l].astype(BF16), final_norm_g.reshape(1, d),
                   ys, s // tiles.moe_tokens, tiles.moe_tokens)
    return out.reshape(b, s, d)
```

```python
import functools
import math
from typing import NamedTuple

import jax
import jax.numpy as jnp
import numpy as np
from jax import lax
from jax.experimental import pallas as pl
from jax.experimental.pallas import tpu as pltpu

F32 = jnp.float32
BF16 = jnp.bfloat16
U32 = jnp.uint32

D_MODEL = 1024
GRID_W = 64
HEAD_DIM = 64
A_HEADS = 4
R_HEADS = 4
R_DK = 64
R_DV = 128
RET_CHUNK = 128
ROPE_BASE = 10000.0
N_EXPERTS = 64
N_GROUPS = 8
TOPK_GROUPS = 4
TOP_K = 8
D_EXPERT = 256
ROUTED_SCALE = 2.5
NORM_EPS = 1e-6
LANES = 128

P_KZ = 0
P_RQ = 1024
P_RK = 1280
P_RV = 1536
P_RG = 2048
P_COLS = 2560
VT_ROWS = 144
LOG2E = 1.4426950408889634

VMEM_LIMIT = 56 * 1024 * 1024


class _Tiles(NamedTuple):
    proj_tokens: int
    attn_queries: int
    attn_keys: int
    ret_chunks: int
    moe_tokens: int
    expert_rows: int


def _tile_plan(seq_len):
    return _Tiles(proj_tokens=min(512, seq_len), attn_queries=min(2048, seq_len),
                  attn_keys=min(512, seq_len // 4), ret_chunks=min(8, seq_len // RET_CHUNK),
                  moe_tokens=256, expert_rows=512)


def _cparams(sem):
    return pltpu.CompilerParams(dimension_semantics=sem, vmem_limit_bytes=VMEM_LIMIT)


def _mod_kernel(c_ref, w_ref, b_ref, o_ref):
    c = c_ref[...]
    s = c * jax.nn.sigmoid(c)
    o_ref[...] = jnp.dot(s, w_ref[...], preferred_element_type=F32,
                         precision=lax.Precision.HIGHEST) + b_ref[...]


def _modulation(c8, ada_w, ada_b):
    d = c8.shape[1]
    n = ada_w.shape[1]
    tn = 512
    return pl.pallas_call(
        _mod_kernel,
        grid=(n // tn,),
        in_specs=[pl.BlockSpec((8, d), lambda j: (0, 0)),
                  pl.BlockSpec((d, tn), lambda j: (0, j)),
                  pl.BlockSpec((1, tn), lambda j: (0, j))],
        out_specs=pl.BlockSpec((8, tn), lambda j: (0, j)),
        out_shape=jax.ShapeDtypeStruct((8, n), F32),
        compiler_params=_cparams(("arbitrary",)),
        name="modulation",
    )(c8, ada_w, ada_b.reshape(1, n))


def _rmsnorm_rows(x, g):
    return x * lax.rsqrt(jnp.mean(x * x, axis=-1, keepdims=True) + NORM_EPS) * g


def _inproj_kernel(x_ref, g_ref, sc_ref, sh_ref, cos_ref, sin_ref, cost_ref, sint_ref, w_ref, wt_ref,
                   p_ref, qt_ref, vt_ref):
    x = x_ref[0]
    tm = x.shape[0]
    h = _rmsnorm_rows(x, g_ref[...]) * (1.0 + sc_ref[0]) + sh_ref[0]
    hb = h.astype(BF16)
    p = jnp.dot(hb, w_ref[...], preferred_element_type=F32)
    pt = lax.dot_general(wt_ref[...], hb, _NT, preferred_element_type=F32)

    cos = cos_ref[...]
    sin = sin_ref[...]
    lane = lax.broadcasted_iota(jnp.int32, (tm, LANES), 1)
    first_half = (lane % 32) < 16
    low64 = lane < 64

    def rope(xs):
        partner = jnp.where(first_half, pltpu.roll(xs, LANES - 16, 1), pltpu.roll(xs, 16, 1))
        return xs * cos + partner * sin

    zero = jnp.zeros((tm, LANES), BF16)
    for hh in range(A_HEADS):
        k = rope(p[:, hh * 128:(hh + 1) * 128]).astype(BF16)
        p_ref[0, :, P_KZ + hh * 256:P_KZ + hh * 256 + 128] = jnp.where(low64, k, zero)
        p_ref[0, :, P_KZ + hh * 256 + 128:P_KZ + (hh + 1) * 256] = jnp.where(low64, zero, k)
    for cc in range(2):
        rq = rope(p[:, 512 + cc * 128:512 + (cc + 1) * 128])
        p_ref[0, :, P_RQ + cc * 128:P_RQ + (cc + 1) * 128] = rq.astype(BF16)
        rk = rope(p[:, 768 + cc * 128:768 + (cc + 1) * 128]) * (R_DK ** -0.5)
        p_ref[0, :, P_RK + cc * 128:P_RK + (cc + 1) * 128] = rk.astype(BF16)
    p_ref[0, :, P_RV:P_RV + 512] = p[:, 1024:1536].astype(BF16)
    p_ref[0, :, P_RG:P_RG + 512] = p[:, 1536:2048].astype(BF16)

    cost = cost_ref[...]
    sint = sint_ref[...]
    q_scale = (HEAD_DIM ** -0.5) * LOG2E
    ones_rows = jnp.where(lax.broadcasted_iota(jnp.int32, (VT_ROWS - 128, tm), 0) == 0, 1.0, 0.0)
    for hh in range(A_HEADS):
        xq = pt[hh * 128:(hh + 1) * 128, :]
        groups = [xq[16 * gi:16 * (gi + 1), :] for gi in range(8)]
        partner = jnp.concatenate([groups[gi ^ 1] for gi in range(8)], axis=0)
        qt_ref[0, hh] = ((xq * cost + partner * sint) * q_scale).astype(BF16)
        vt_ref[0, hh, 0:128, :] = pt[512 + hh * 128:512 + (hh + 1) * 128, :].astype(BF16)
        vt_ref[0, hh, 128:VT_ROWS, :] = ones_rows.astype(BF16)


def _inproj(x, g, sc, sh, cos, sin, cost, sint, w_bf, wt_bf, tm):
    b, s, d = x.shape
    n_tok = w_bf.shape[1]
    n_t = wt_bf.shape[0]
    const2 = lambda shape: pl.BlockSpec(shape, lambda bi, i: (0, 0))
    per_b = pl.BlockSpec((1, 1, d), lambda bi, i: (bi, 0, 0))
    return pl.pallas_call(
        _inproj_kernel,
        grid=(b, s // tm),
        in_specs=[pl.BlockSpec((1, tm, d), lambda bi, i: (bi, i, 0)),
                  const2((1, d)), per_b, per_b,
                  pl.BlockSpec((tm, LANES), lambda bi, i: (i, 0)),
                  pl.BlockSpec((tm, LANES), lambda bi, i: (i, 0)),
                  pl.BlockSpec((LANES, tm), lambda bi, i: (0, i)),
                  pl.BlockSpec((LANES, tm), lambda bi, i: (0, i)),
                  const2((d, n_tok)), const2((n_t, d))],
        out_specs=[pl.BlockSpec((1, tm, P_COLS), lambda bi, i: (bi, i, 0)),
                   pl.BlockSpec((1, A_HEADS, LANES, tm), lambda bi, i: (bi, 0, 0, i)),
                   pl.BlockSpec((1, A_HEADS, VT_ROWS, tm), lambda bi, i: (bi, 0, 0, i))],
        out_shape=[jax.ShapeDtypeStruct((b, s, P_COLS), BF16),
                   jax.ShapeDtypeStruct((b, A_HEADS, LANES, s), BF16),
                   jax.ShapeDtypeStruct((b, A_HEADS, VT_ROWS, s), BF16)],
        compiler_params=_cparams(("arbitrary", "arbitrary")),
        name="inproj",
    )(x, g, sc, sh, cos, sin, cost, sint, w_bf, wt_bf)


_NT = (((1,), (1,)), ((), ()))


def _attn_kernel(lam_ref, qt_ref, kzc_ref, vtc_ref, kzx_ref, vtx_ref, o_ref,
                 acc, s_a, s_b, e_a, e_b, *, tkc):
    qt = qt_ref[0, 0]
    tq = qt.shape[1]
    n_chunks = kzx_ref.shape[1] // tkc
    assert n_chunks % 2 == 0 and n_chunks >= 4

    def scores(kz, s_ref):
        for mp in range(2):
            s_ref[mp] = jnp.dot(kz[:, mp * LANES:(mp + 1) * LANES], qt, preferred_element_type=F32)

    def softmax(s_ref, e_ref, ms):
        new_ms, alphas = [], []
        for mp in range(2):
            st = s_ref[mp]
            m_new = jnp.maximum(ms[mp], jnp.max(st, axis=0, keepdims=True))
            alphas.append(jnp.exp2(ms[mp] - m_new))
            e_ref[mp] = jnp.exp2(st - m_new).astype(BF16)
            new_ms.append(m_new)
        return tuple(new_ms), tuple(alphas)

    def values(vt, e_ref, alphas):
        for mp in range(2):
            acc[mp] = alphas[mp] * acc[mp] + jnp.dot(vt, e_ref[mp], preferred_element_type=F32)

    def kchunk(c):
        return kzx_ref[0, pl.ds(pl.multiple_of(c * tkc, tkc), tkc), :]

    def vchunk(c):
        return vtx_ref[0, 0, :, pl.ds(pl.multiple_of(c * tkc, tkc), tkc)]

    ms = []
    for mp in range(2):
        st = jnp.dot(kzc_ref[0, :, mp * LANES:(mp + 1) * LANES], qt, preferred_element_type=F32)
        m_new = jnp.max(st, axis=0, keepdims=True)
        acc[mp] = jnp.dot(vtc_ref[0, 0], jnp.exp2(st - m_new).astype(BF16), preferred_element_type=F32)
        ms.append(m_new)
    ms = tuple(ms)

    scores(kchunk(0), s_a)
    scores(kchunk(1), s_b)
    ms, al_a = softmax(s_a, e_a, ms)

    def body(j, carry):
        ms, al_a = carry
        scores(kchunk(2 * j + 2), s_a)
        ms, al_b = softmax(s_b, e_b, ms)
        values(vchunk(2 * j), e_a, al_a)
        scores(kchunk(2 * j + 3), s_b)
        ms, al_a = softmax(s_a, e_a, ms)
        values(vchunk(2 * j + 1), e_b, al_b)
        return ms, al_a

    ms, al_a = lax.fori_loop(0, n_chunks // 2 - 1, body, (ms, al_a))
    ms, al_b = softmax(s_b, e_b, ms)
    values(vchunk(n_chunks - 2), e_a, al_a)
    values(vchunk(n_chunks - 1), e_b, al_b)

    a1 = acc[0]
    a2 = acc[1]
    o1 = a1[:LANES, :] / a1[LANES:LANES + 1, :]
    o2 = a2[:LANES, :] / a2[LANES:LANES + 1, :]
    o_ref[0] = (o1 - lam_ref[0] * o2).T.astype(BF16)


def _attention(lam, qt, px, vtx, pc, vtc, tq, tkc):
    b, s, _ = px.shape
    lc = pc.shape[1]
    kern = functools.partial(_attn_kernel, tkc=tkc)
    return pl.pallas_call(
        kern,
        grid=(b, A_HEADS, s // tq),
        in_specs=[pl.BlockSpec(memory_space=pltpu.SMEM),
                  pl.BlockSpec((1, 1, LANES, tq), lambda bi, h, i: (bi, h, 0, i)),
                  pl.BlockSpec((1, lc, 256), lambda bi, h, i: (bi, 0, P_KZ // 256 + h)),
                  pl.BlockSpec((1, 1, VT_ROWS, lc), lambda bi, h, i: (bi, h, 0, 0)),
                  pl.BlockSpec((1, s, 256), lambda bi, h, i: (bi, 0, P_KZ // 256 + h)),
                  pl.BlockSpec((1, 1, VT_ROWS, s), lambda bi, h, i: (bi, h, 0, 0))],
        out_specs=pl.BlockSpec((1, tq, LANES), lambda bi, h, i: (bi, i, h)),
        out_shape=jax.ShapeDtypeStruct((b, s, A_HEADS * LANES), BF16),
        scratch_shapes=[pltpu.VMEM((2, VT_ROWS, tq), F32),
                        pltpu.VMEM((2, tkc, tq), F32), pltpu.VMEM((2, tkc, tq), F32),
                        pltpu.VMEM((2, tkc, tq), BF16), pltpu.VMEM((2, tkc, tq), BF16)],
        compiler_params=_cparams(("arbitrary", "arbitrary", "arbitrary")),
        name="diff_attention",
    )(lam, qt, pc, vtc, px, vtx)


_TN = (((0,), (0,)), ((), ()))


def _ret_kernel(qf_ref, kf_ref, vf_ref, qb_ref, kb_ref, vb_ref, kc_ref, vc_ref,
                dmat_ref, qdec_ref, kdec_ref, gch_ref, of_ref, ob_ref, sf, sb):
    c = RET_CHUNK
    n = pl.program_id(1)
    col_head = lax.broadcasted_iota(jnp.int32, (c, R_HEADS * R_DK), 1) // R_DK
    bd_mask = (lax.broadcasted_iota(jnp.int32, (R_HEADS * R_DK, R_HEADS * R_DV), 0) // R_DK
               == lax.broadcasted_iota(jnp.int32, (R_HEADS * R_DK, R_HEADS * R_DV), 1) // R_DV)

    def state_update(s_ref, k, v, kdec, gch):
        kd = (k.astype(F32) * kdec).astype(BF16)
        kv = lax.dot_general(kd, v, _TN, preferred_element_type=F32)
        s_ref[...] = gch * s_ref[...] + jnp.where(bd_mask, kv, 0.0)

    def chunk(s_ref, q, k, v, dmat, qdec, kdec, gch):
        zero = jnp.zeros_like(k)
        kblk = jnp.concatenate([jnp.where(col_head == h, k, zero) for h in range(R_HEADS)], axis=0)
        sc = lax.dot_general(q, kblk, _NT, preferred_element_type=F32) * dmat
        p = sc.astype(BF16)
        intra = jnp.concatenate(
            [jnp.dot(p[:, h * c:(h + 1) * c], v[:, h * R_DV:(h + 1) * R_DV], preferred_element_type=F32)
             for h in range(R_HEADS)], axis=1)
        qd = (q.astype(F32) * qdec).astype(BF16)
        cross = jnp.dot(qd, s_ref[...].astype(BF16), preferred_element_type=F32)
        out = intra + cross
        state_update(s_ref, k, v, kdec, gch)
        return out

    @pl.when(n == 0)
    def _():
        sf[...] = jnp.zeros_like(sf)
        sb[...] = jnp.zeros_like(sb)
        n_ctx = kc_ref.shape[1] // c
        for cc in range(n_ctx):
            state_update(sf, kc_ref[0, cc * c:(cc + 1) * c, :], vc_ref[0, cc * c:(cc + 1) * c, :],
                         kdec_ref[0], gch_ref[0])
        for cc in reversed(range(n_ctx)):
            state_update(sb, kc_ref[0, cc * c:(cc + 1) * c, :], vc_ref[0, cc * c:(cc + 1) * c, :],
                         kdec_ref[1], gch_ref[1])

    chunks_per_step = qf_ref.shape[1] // c
    for cc in range(chunks_per_step):
        rows = slice(cc * c, (cc + 1) * c)
        of_ref[0, rows, :] = chunk(sf, qf_ref[0, rows, :], kf_ref[0, rows, :], vf_ref[0, rows, :], dmat_ref[0],
                                   qdec_ref[0], kdec_ref[0], gch_ref[0]).astype(BF16)
    for cc in reversed(range(chunks_per_step)):
        rows = slice(cc * c, (cc + 1) * c)
        ob_ref[0, rows, :] = chunk(sb, qb_ref[0, rows, :], kb_ref[0, rows, :], vb_ref[0, rows, :], dmat_ref[1],
                                   qdec_ref[1], kdec_ref[1], gch_ref[1]).astype(BF16)


def _retention(px, pc, dmat, qdec, kdec, gch, chunks_per_step):
    b, s, _ = px.shape
    lc = pc.shape[1]
    c = RET_CHUNK
    rows = c * chunks_per_step
    nch = s // rows
    qk_w = R_HEADS * R_DK
    v_w = R_HEADS * R_DV
    fwd = lambda bi, n: (bi, n, 0)
    spec_q = lambda im: pl.BlockSpec((1, rows, qk_w), lambda bi, n: im(bi, n)[:2] + (P_RQ // qk_w,))
    spec_k = lambda im: pl.BlockSpec((1, rows, qk_w), lambda bi, n: im(bi, n)[:2] + (P_RK // qk_w,))
    spec_v = lambda im: pl.BlockSpec((1, rows, v_w), lambda bi, n: im(bi, n)[:2] + (P_RV // v_w,))
    bwd = lambda bi, n: (bi, nch - 1 - n, 0)
    const3 = lambda shape: pl.BlockSpec(shape, lambda bi, n: (0, 0, 0))
    return pl.pallas_call(
        _ret_kernel,
        grid=(b, nch),
        in_specs=[spec_q(fwd), spec_k(fwd), spec_v(fwd), spec_q(bwd), spec_k(bwd), spec_v(bwd),
                  pl.BlockSpec((1, lc, qk_w), lambda bi, n: (bi, 0, P_RK // qk_w)),
                  pl.BlockSpec((1, lc, v_w), lambda bi, n: (bi, 0, P_RV // v_w)),
                  const3((2, c, R_HEADS * c)), const3((2, c, qk_w)), const3((2, c, qk_w)),
                  const3((2, 1, v_w))],
        out_specs=[pl.BlockSpec((1, rows, v_w), fwd), pl.BlockSpec((1, rows, v_w), bwd)],
        out_shape=[jax.ShapeDtypeStruct((b, s, v_w), BF16), jax.ShapeDtypeStruct((b, s, v_w), BF16)],
        scratch_shapes=[pltpu.VMEM((qk_w, v_w), F32), pltpu.VMEM((qk_w, v_w), F32)],
        compiler_params=_cparams(("arbitrary", "arbitrary")),
        name="retention",
    )(px, px, px, px, px, px, pc, pc, dmat, qdec, kdec, gch)


def _retention_tables(decay_fwd, decay_bwd):
    c = RET_CHUNK
    pos = jnp.arange(c, dtype=F32)
    rel = pos[:, None] - pos[None, :]

    def tables(logits, backward):
        lg = jax.nn.log_sigmoid(logits.astype(F32))
        r = -rel if backward else rel
        dm = jnp.where(r[None] >= 0, jnp.exp(jnp.maximum(r, 0.0)[None] * lg[:, None, None]), 0.0)
        dm = jnp.transpose(dm, (1, 0, 2)).reshape(c, R_HEADS * c)
        qpow = (c - pos) if backward else (pos + 1)
        kpow = pos if backward else (c - 1 - pos)
        qd = jnp.repeat(jnp.exp(qpow[:, None] * lg[None, :]), R_DK, axis=1)
        kd = jnp.repeat(jnp.exp(kpow[:, None] * lg[None, :]), R_DK, axis=1)
        gc = jnp.repeat(jnp.exp(c * lg), R_DV)[None, :]
        return dm, qd, kd, gc

    tf = tables(decay_fwd, False)
    tb = tables(decay_bwd, True)
    return tuple(jnp.stack([a, bb]) for a, bb in zip(tf, tb))


def _pack_pairs(x):
    k = x.shape[1] // 2
    lo = lax.bitcast_convert_type(x[:, :k].astype(BF16).astype(F32), U32)
    hi = lax.bitcast_convert_type(x[:, k:].astype(BF16).astype(F32), U32)
    return (lo >> 16) | (hi & jnp.uint32(0xFFFF0000))


def _unpack_pairs(w):
    lo = lax.bitcast_convert_type(w << 16, F32)
    hi = lax.bitcast_convert_type(w & jnp.uint32(0xFFFF0000), F32)
    return lo, hi


ROW_CHUNKS = (D_MODEL // 2) // LANES


def _store_token_rows(ref, first_token, n_tokens, x):
    packed = _pack_pairs(x)
    for c in range(ROW_CHUNKS):
        ref[pl.ds(first_token * ROW_CHUNKS + c, n_tokens, stride=ROW_CHUNKS), :] = (
            packed[:, c * LANES:(c + 1) * LANES])


def _load_token_rows(ref, first_token, n_tokens):
    return [_unpack_pairs(ref[pl.ds(first_token * ROW_CHUNKS + c, n_tokens, stride=ROW_CHUNKS), :])
            for c in range(ROW_CHUNKS)]


def _rows_to_features(chunks):
    return jnp.concatenate([lo for lo, _ in chunks] + [hi for _, hi in chunks], axis=1)


def _route_tile(logits_t, bias_col, carry_ref):
    n_e, tm = logits_t.shape
    per = n_e // N_GROUPS
    neg = -jnp.inf
    scores = jax.nn.sigmoid(logits_t)
    sel = scores + bias_col
    sub = lax.broadcasted_iota(jnp.int32, (per, tm), 0).astype(F32)
    sel_g = [sel[g * per:(g + 1) * per, :] for g in range(N_GROUPS)]
    flat_g = [sub + float(g * per) for g in range(N_GROUPS)]

    gs = []
    for g in range(N_GROUPS):
        top1 = jnp.max(sel_g[g], axis=0, keepdims=True)
        idx1 = jnp.min(jnp.where(sel_g[g] == top1, sub, float(per)), axis=0, keepdims=True)
        top2 = jnp.max(jnp.where(sub == idx1, neg, sel_g[g]), axis=0, keepdims=True)
        gs.append(top1 + top2)
    gsel = [jnp.zeros((1, tm), F32) for _ in range(N_GROUPS)]
    for _ in range(TOPK_GROUPS):
        m = functools.reduce(jnp.maximum, gs)
        found = jnp.zeros((1, tm), F32)
        for g in range(N_GROUPS):
            hit = jnp.logical_and(gs[g] == m, found == 0.0)
            found = jnp.where(hit, 1.0, found)
            gsel[g] = jnp.where(hit, 1.0, gsel[g])
            gs[g] = jnp.where(hit, neg, gs[g])

    masked = [jnp.where(gsel[g] > 0.0, sel_g[g], neg) for g in range(N_GROUPS)]
    chosen = [jnp.zeros((per, tm), F32) for _ in range(N_GROUPS)]
    for _ in range(TOP_K):
        m = functools.reduce(jnp.maximum, [jnp.max(mg, axis=0, keepdims=True) for mg in masked])
        idx = functools.reduce(jnp.minimum, [
            jnp.min(jnp.where(masked[g] == m, flat_g[g], float(n_e)), axis=0, keepdims=True)
            for g in range(N_GROUPS)])
        for g in range(N_GROUPS):
            hit = flat_g[g] == idx
            chosen[g] = jnp.where(hit, 1.0, chosen[g])
            masked[g] = jnp.where(hit, neg, masked[g])
    chosen = jnp.concatenate(chosen, axis=0)
    flat = jnp.concatenate(flat_g, axis=0)

    w_dense = scores * chosen
    w_dense = w_dense / jnp.sum(w_dense, axis=0, keepdims=True) * ROUTED_SCALE

    chosen_bf = chosen.astype(BF16)
    lower = (lax.broadcasted_iota(jnp.int32, (n_e, n_e), 1)
             < lax.broadcasted_iota(jnp.int32, (n_e, n_e), 0)).astype(BF16)
    slot_idx = jnp.dot(lower, chosen_bf, preferred_element_type=F32)
    upper = (lax.broadcasted_iota(jnp.int32, (tm, tm), 0)
             < lax.broadcasted_iota(jnp.int32, (tm, tm), 1)).astype(BF16)
    rank = jnp.dot(chosen_bf, upper, preferred_element_type=F32) + carry_ref[...]
    carry_ref[...] += jnp.sum(chosen, axis=1, keepdims=True)

    slots = []
    for j in range(TOP_K):
        ind = jnp.where(slot_idx == float(j), chosen, 0.0)
        slots.append((jnp.sum(ind * flat, axis=0, keepdims=True),
                      jnp.sum(ind * w_dense, axis=0, keepdims=True),
                      jnp.sum(ind * rank, axis=0, keepdims=True)))
    return slots


def _merge_kernel(att_ref, rf_ref, rb_ref, gate_ref, x_ref, g1_ref, sh2_ref, sc2_ref, ga_ref, gr_ref,
                  wout_ref, n2g_ref, rwt_ref, rbias_ref, x1_ref, h2p_ref, se_ref, sw_ref, sr_ref, cnt_ref,
                  carry, *, att_scale):
    @pl.when(jnp.logical_and(pl.program_id(0) == 0, pl.program_id(1) == 0))
    def _():
        carry[...] = jnp.zeros_like(carry)

    att = att_ref[0].astype(F32)
    ret = rf_ref[0].astype(F32) + rb_ref[0].astype(F32)
    gate = gate_ref[0].astype(F32)
    parts = []
    for h in range(A_HEADS):
        a = att[:, h * 128:(h + 1) * 128]
        parts.append(_rmsnorm_rows(a, ga_ref[...]) * att_scale)
    for h in range(R_HEADS):
        r = _rmsnorm_rows(ret[:, h * R_DV:(h + 1) * R_DV], gr_ref[...])
        g = gate[:, h * R_DV:(h + 1) * R_DV]
        parts.append(g * jax.nn.sigmoid(g) * r)
    merged = jnp.concatenate(parts, axis=1).astype(BF16)
    y = jnp.dot(merged, wout_ref[...], preferred_element_type=F32)
    x1 = x_ref[0] + g1_ref[0] * y
    x1_ref[0] = x1
    h2 = _rmsnorm_rows(x1, n2g_ref[...]) * (1.0 + sc2_ref[0]) + sh2_ref[0]
    _store_token_rows(h2p_ref, 0, h2.shape[0], h2)
    logits_t = lax.dot_general(rwt_ref[...], h2.astype(BF16), _NT, preferred_element_type=F32)
    slots = _route_tile(logits_t, rbias_ref[...], carry)
    for j, (e_j, w_j, r_j) in enumerate(slots):
        se_ref[j:j + 1, :] = e_j.astype(jnp.int32)
        sw_ref[j:j + 1, :] = w_j
        sr_ref[j:j + 1, :] = r_j.astype(jnp.int32)
    cnt_ref[...] = carry[...]


def _merge(att, rf, rb, px, x, g1, sh2, sc2, ga, gr, wout_bf, n2g, rwt_bf, rbias, att_scale, tm):
    b, s, d = x.shape
    nt = s // tm
    chunked = pl.BlockSpec((ROW_CHUNKS * tm, LANES), lambda bi, i: (bi * nt + i, 0))
    chunked_shape = jax.ShapeDtypeStruct((b * s * ROW_CHUNKS, LANES), U32)
    row = lambda bi, i: (bi, i, 0)
    tok = lambda bi, i: (0, bi * nt + i)
    per_b = pl.BlockSpec((1, 1, d), lambda bi, i: (bi, 0, 0))
    const2 = lambda shape: pl.BlockSpec(shape, lambda bi, i: (0, 0))
    return pl.pallas_call(
        functools.partial(_merge_kernel, att_scale=att_scale),
        grid=(b, nt),
        in_specs=[pl.BlockSpec((1, tm, 512), row), pl.BlockSpec((1, tm, 512), row),
                  pl.BlockSpec((1, tm, 512), row),
                  pl.BlockSpec((1, tm, 512), lambda bi, i: (bi, i, P_RG // 512)),
                  pl.BlockSpec((1, tm, d), row), per_b, per_b, per_b,
                  const2((1, 128)), const2((1, R_DV)), const2((d, d)), const2((1, d)),
                  const2((N_EXPERTS, d)), const2((N_EXPERTS, 1))],
        out_specs=[pl.BlockSpec((1, tm, d), row), chunked,
                   pl.BlockSpec((TOP_K, tm), tok), pl.BlockSpec((TOP_K, tm), tok),
                   pl.BlockSpec((TOP_K, tm), tok), const2((N_EXPERTS, 1))],
        out_shape=[jax.ShapeDtypeStruct((b, s, d), F32), chunked_shape,
                   jax.ShapeDtypeStruct((TOP_K, b * s), jnp.int32),
                   jax.ShapeDtypeStruct((TOP_K, b * s), F32),
                   jax.ShapeDtypeStruct((TOP_K, b * s), jnp.int32),
                   jax.ShapeDtypeStruct((N_EXPERTS, 1), F32)],
        scratch_shapes=[pltpu.VMEM((N_EXPERTS, 1), F32)],
        compiler_params=_cparams(("arbitrary", "arbitrary")),
        name="mixer_merge",
    )(att, rf, rb, px, x, g1, sh2, sc2, ga, gr, wout_bf, n2g, rwt_bf, rbias)


def _dispatch_kernel(dest_ref, h_ref, xs_ref, buf, sem):
    rows = h_ref.shape[0]
    tm = rows // ROW_CHUNKS
    i = pl.program_id(0)
    n = pl.num_programs(0)
    slot = i % 2

    def wait_slot(s):
        for _ in range(TOP_K):
            pltpu.make_async_copy(buf.at[s], xs_ref.at[pl.ds(0, rows)], sem.at[s]).wait()

    @pl.when(i >= 2)
    def _():
        wait_slot(slot)

    buf[slot] = h_ref[...]

    def body(t, carry):
        src = buf.at[slot, pl.ds(pl.multiple_of(t * ROW_CHUNKS, ROW_CHUNKS), ROW_CHUNKS)]
        for j in range(TOP_K):
            d = pl.multiple_of(dest_ref[t * TOP_K + j], ROW_CHUNKS)
            pltpu.make_async_copy(src, xs_ref.at[pl.ds(d, ROW_CHUNKS)], sem.at[slot]).start(priority=j % 2)
        return carry

    lax.fori_loop(0, tm, body, 0)

    @pl.when(i == n - 1)
    def _():
        wait_slot(slot)

    @pl.when(jnp.logical_and(i == n - 1, n >= 2))
    def _():
        wait_slot(1 - slot)


def _dispatch(dest_rows, h2p, tm):
    n_rows = h2p.shape[0]
    n = n_rows // (ROW_CHUNKS * tm)
    return pl.pallas_call(
        _dispatch_kernel,
        grid=(n,),
        in_specs=[pl.BlockSpec((tm * TOP_K,), lambda i: (i,), memory_space=pltpu.SMEM),
                  pl.BlockSpec((ROW_CHUNKS * tm, LANES), lambda i: (i, 0))],
        out_specs=pl.BlockSpec(memory_space=pl.ANY),
        out_shape=jax.ShapeDtypeStruct((n_rows * TOP_K, LANES), U32),
        scratch_shapes=[pltpu.VMEM((2, ROW_CHUNKS * tm, LANES), U32), pltpu.SemaphoreType.DMA((2,))],
        compiler_params=_cparams(("arbitrary",)),
        name="moe_dispatch",
    )(dest_rows, h2p)


def _expert_kernel(iblk_ref, iexp_ref, ilo_ref, ihi_ref, xs_ref, w1_ref, w3_ref, w2_ref, ys_ref,
                   acc, hm, w1b, w3b, w2b):
    r = xs_ref.shape[0] // ROW_CHUNKS
    i = pl.program_id(0)
    n_items = pl.num_programs(0) - 1
    cur = jnp.minimum(i, n_items - 1)
    cur_prev = jnp.maximum(cur - 1, 0)
    dn = jnp.maximum(i - 1, 0)
    dn_prev = jnp.maximum(i - 2, 0)
    blk = iblk_ref[cur]
    lo = ilo_ref[cur]
    hi = ihi_ref[cur]
    dn_first = jnp.logical_or(i <= 1, iblk_ref[dn_prev] != iblk_ref[dn])
    slot = i % 2

    @pl.when(i == 0)
    def _():
        acc[...] = jnp.zeros_like(acc)
        hm[...] = jnp.zeros_like(hm)

    @pl.when(jnp.logical_or(i == 0, iexp_ref[cur_prev] != iexp_ref[cur]))
    def _():
        w1b[...] = w1_ref[0].astype(BF16)
        w3b[...] = w3_ref[0].astype(BF16)

    @pl.when(jnp.logical_or(i <= 1, iexp_ref[dn_prev] != iexp_ref[dn]))
    def _():
        w2b[...] = w2_ref[0].astype(BF16)

    n_half = 4
    rh = r // n_half

    def run(hm_rd, hm_wr):
        for hf in range(n_half):
            sl = slice(hf * rh, (hf + 1) * rh)
            y = jnp.dot(hm_rd[sl, :], w2b[...], preferred_element_type=F32)
            y = jnp.where(dn_first, y, acc[sl, :] + y)
            acc[sl, :] = y
            _store_token_rows(ys_ref, hf * rh, rh, y)

            rows = blk * r + hf * rh + lax.broadcasted_iota(jnp.int32, (rh, 1), 0)
            valid = jnp.logical_and(rows >= lo, rows < hi)
            x = _rows_to_features(_load_token_rows(xs_ref, hf * rh, rh))
            x = jnp.where(valid, x, 0.0).astype(BF16)
            a = jnp.dot(x, w1b[...], preferred_element_type=F32)
            bb = jnp.dot(x, w3b[...], preferred_element_type=F32)
            hm_wr[sl, :] = (a * jax.nn.sigmoid(a) * bb).astype(BF16)

    @pl.when(slot == 0)
    def _():
        run(hm.at[1], hm.at[0])

    @pl.when(slot == 1)
    def _():
        run(hm.at[0], hm.at[1])


def _experts(iblk, iexp, ilo, ihi, xs, w1, w3, w2, r):
    na, kw = xs.shape
    rows = r * ROW_CHUNKS
    n_items = iblk.shape[0]
    d = w1.shape[1]
    de = w1.shape[2]
    up = lambda i: jnp.minimum(i, n_items - 1)
    down = lambda i: jnp.maximum(i - 1, 0)
    grid_spec = pltpu.PrefetchScalarGridSpec(
        num_scalar_prefetch=4,
        grid=(n_items + 1,),
        in_specs=[pl.BlockSpec((rows, kw), lambda i, ib, ie, il, ih: (ib[up(i)], 0)),
                  pl.BlockSpec((1, d, de), lambda i, ib, ie, il, ih: (ie[up(i)], 0, 0)),
                  pl.BlockSpec((1, d, de), lambda i, ib, ie, il, ih: (ie[up(i)], 0, 0)),
                  pl.BlockSpec((1, de, d), lambda i, ib, ie, il, ih: (ie[down(i)], 0, 0))],
        out_specs=pl.BlockSpec((rows, kw), lambda i, ib, ie, il, ih: (ib[down(i)], 0)),
        scratch_shapes=[pltpu.VMEM((r, d), F32), pltpu.VMEM((2, r, de), BF16),
                        pltpu.VMEM((d, de), BF16), pltpu.VMEM((d, de), BF16), pltpu.VMEM((de, d), BF16)],
    )
    return pl.pallas_call(
        _expert_kernel,
        grid_spec=grid_spec,
        out_shape=jax.ShapeDtypeStruct((na, kw), U32),
        compiler_params=_cparams(("arbitrary",)),
        name="moe_experts",
    )(iblk, iexp, ilo, ihi, xs, w1, w3, w2)


def _combine_kernel(dest_ref, dnext_ref, x1_ref, h2p_ref, wts_ref, g2_ref, sw1_ref, sw3_ref, sw2_ref,
                    fg_ref, ys_ref, o_ref, gbuf, sem):
    tm = x1_ref.shape[0]
    i = pl.program_id(0)
    n = pl.num_programs(0)
    slot = i % 2

    rows = ROW_CHUNKS * tm

    def gather_row(dref, s, t, off, j):
        d = pl.multiple_of(dref[t * TOP_K + j], ROW_CHUNKS)
        pltpu.make_async_copy(ys_ref.at[pl.ds(d, ROW_CHUNKS)], gbuf.at[s, j, pl.ds(off, ROW_CHUNKS)],
                              sem.at[s]).start(priority=j % 2)

    def wait_slot(s):
        for _ in range(TOP_K):
            pltpu.make_async_copy(ys_ref.at[pl.ds(0, rows)], gbuf.at[s, 0], sem.at[s]).wait()

    @pl.when(i == 0)
    def _():
        def body(t, carry):
            for j in range(TOP_K):
                gather_row(dest_ref, 0, t, pl.multiple_of(t * ROW_CHUNKS, ROW_CHUNKS), j)
            return carry
        lax.fori_loop(0, tm, body, 0)

    wait_slot(slot)
    for t in range(tm):
        for j in range(TOP_K):
            gather_row(dnext_ref, 1 - slot, t, t * ROW_CHUNKS, j)

    wts = wts_ref[...]
    moe = None
    for j in range(TOP_K):
        wj = wts[:, j:j + 1]
        yj = [(lo * wj, hi * wj) for lo, hi in _load_token_rows(gbuf.at[slot, j], 0, tm)]
        moe = yj if moe is None else [(a + lo, b + hi) for (a, b), (lo, hi) in zip(moe, yj)]
    h = _rows_to_features(_load_token_rows(h2p_ref, 0, tm)).astype(BF16)
    a = jnp.dot(h, sw1_ref[...], preferred_element_type=F32)
    bb = jnp.dot(h, sw3_ref[...], preferred_element_type=F32)
    shared = jnp.dot((a * jax.nn.sigmoid(a) * bb).astype(BF16), sw2_ref[...], preferred_element_type=F32)
    ffn = _rows_to_features(moe) + shared
    xo = x1_ref[...] + g2_ref[0] * ffn
    o_ref[...] = _rmsnorm_rows(xo, fg_ref[...])

    @pl.when(i == n - 1)
    def _():
        wait_slot(1 - slot)


def _combine(dest_rows, x1, h2p, wts, g2, sw1, sw3, sw2, fg, ys, tiles_per_batch, tm):
    t, d = x1.shape
    n = t // tm
    de = sw1.shape[1]
    const2 = lambda shape: pl.BlockSpec(shape, lambda i: (0, 0))
    return pl.pallas_call(
        _combine_kernel,
        grid=(n,),
        in_specs=[pl.BlockSpec((tm * TOP_K,), lambda i: (i,), memory_space=pltpu.SMEM),
                  pl.BlockSpec((tm * TOP_K,), lambda i: (jnp.minimum(i + 1, n - 1),),
                               memory_space=pltpu.SMEM),
                  pl.BlockSpec((tm, d), lambda i: (i, 0)),
                  pl.BlockSpec((ROW_CHUNKS * tm, LANES), lambda i: (i, 0)),
                  pl.BlockSpec((tm, TOP_K), lambda i: (i, 0)),
                  pl.BlockSpec((1, 1, d), lambda i: (i // tiles_per_batch, 0, 0)),
                  const2((d, de)), const2((d, de)), const2((de, d)), const2((1, d)),
                  pl.BlockSpec(memory_space=pl.ANY)],
        out_specs=pl.BlockSpec((tm, d), lambda i: (i, 0)),
        out_shape=jax.ShapeDtypeStruct((t, d), F32),
        scratch_shapes=[pltpu.VMEM((2, TOP_K, ROW_CHUNKS * tm, LANES), U32),
                        pltpu.SemaphoreType.DMA((2,))],
        compiler_params=_cparams(("arbitrary",)),
        name="moe_combine",
    )(dest_rows, dest_rows, x1, h2p, wts, g2, sw1, sw3, sw2, fg, ys)


def _route_meta(counts, slot_e, slot_r, block_rows):
    t = slot_e.shape[1]
    ends = jnp.cumsum(counts)
    starts = ends - counts
    experts = jnp.arange(N_EXPERTS, dtype=jnp.int32)
    dest_t = slot_r + jnp.sum(jnp.where(slot_e[:, :, None] == experts, starts, 0), axis=-1)

    n_blk = t * TOP_K // block_rows
    n_items = n_blk + N_EXPERTS - 1
    first_blk = starts // block_rows
    last_blk = (ends - 1) // block_rows
    n_e = jnp.where(counts > 0, last_blk - first_blk + 1, 0)
    item_end = jnp.cumsum(n_e)
    item_off = item_end - n_e
    total = item_end[-1]
    idx = jnp.arange(n_items, dtype=jnp.int32)
    e_of = jnp.minimum(jnp.sum(item_end[None, :] <= idx[:, None], axis=1), N_EXPERTS - 1).astype(jnp.int32)
    onehot = e_of[:, None] == experts[None, :]
    lut = lambda table: jnp.sum(jnp.where(onehot, table[None, :], 0), axis=1)
    live = idx < total
    blk = lut(first_blk) + (idx - lut(item_off))
    lo = jnp.maximum(lut(starts), blk * block_rows)
    hi = jnp.minimum(lut(ends), (blk + 1) * block_rows)
    last_e = jnp.sum(jnp.where(idx == total - 1, e_of, 0))
    iblk = jnp.where(live, blk, n_blk - 1).astype(jnp.int32)
    iexp = jnp.where(live, e_of, last_e).astype(jnp.int32)
    ilo = jnp.where(live, lo, 0).astype(jnp.int32)
    ihi = jnp.where(live, hi, 0).astype(jnp.int32)
    return dest_t, iblk, iexp, ilo, ihi


def _rope_tables(n_tokens):
    n_rows = n_tokens // GRID_W
    rows = np.repeat(np.arange(n_rows, dtype=np.float32), GRID_W)
    cols = np.tile(np.arange(GRID_W, dtype=np.float32), n_rows)
    n_freq = HEAD_DIM // 4
    inv_freq = (np.float32(ROPE_BASE) ** (-np.arange(n_freq, dtype=np.float32) / n_freq)).astype(np.float32)
    ar = rows[:, None] * inv_freq
    ac = cols[:, None] * inv_freq
    cos64 = np.concatenate([np.cos(ar), np.cos(ar), np.cos(ac), np.cos(ac)], axis=1)
    sin64 = np.concatenate([-np.sin(ar), np.sin(ar), -np.sin(ac), np.sin(ac)], axis=1)
    return np.tile(cos64, (1, 2)).astype(np.float32), np.tile(sin64, (1, 2)).astype(np.float32)


def kernel(x, c, ctx, c_ctx, ada_w, ada_b, norm1_g, w_in, lambda_q1, lambda_k1, lambda_q2, lambda_k2,
           dattn_norm_g, ret_decay_fwd, ret_decay_bwd, ret_norm_g, w_out, norm2_g, router_w, router_bias,
           exp_w1, exp_w3, exp_w2, shared_w1, shared_w3, shared_w2, final_norm_g):
    b, s, d = x.shape
    lc = ctx.shape[1]
    depth = ada_w.shape[0]
    assert depth == 1, "single-layer problem: the context stream is never updated"
    l = 0
    lam_init = 0.8 - 0.6 * math.exp(-0.3 * l)

    c8 = jnp.zeros((8, d), F32).at[:b].set(c).at[b].set(c_ctx)
    mod = _modulation(c8, ada_w[l], ada_b[l])
    sh1, sc1, g1, sh2, sc2, g2 = [mod[:b, k * d:(k + 1) * d].reshape(b, 1, d) for k in range(6)]
    csh1 = jnp.broadcast_to(mod[b, 0:d].reshape(1, 1, d), (b, 1, d))
    csc1 = jnp.broadcast_to(mod[b, d:2 * d].reshape(1, 1, d), (b, 1, d))

    w_l = w_in[l]
    w_tok = jnp.concatenate([w_l[:, 512:1024], w_l[:, 1536:3072]], axis=1).astype(BF16)
    w_feat = jnp.concatenate([w_l[:, 0:512], w_l[:, 1024:1536]], axis=1).T.astype(BF16)
    n1g = norm1_g[l].reshape(1, d)
    cos, sin = _rope_tables(s)
    ones_c = jnp.ones((lc, LANES), F32)
    zeros_c = jnp.zeros((lc, LANES), F32)
    tiles = _tile_plan(s)
    px, qt, vtx = _inproj(x, n1g, sc1, sh1, cos, sin, np.ascontiguousarray(cos.T),
                          np.ascontiguousarray(sin.T), w_tok, w_feat, tm=tiles.proj_tokens)
    pc, _, vtc = _inproj(ctx, n1g, csc1, csh1, ones_c, zeros_c, ones_c.T, zeros_c.T, w_tok, w_feat, tm=lc)

    lam = (jnp.exp(jnp.sum(lambda_q1[l] * lambda_k1[l])) - jnp.exp(jnp.sum(lambda_q2[l] * lambda_k2[l]))
           ).astype(F32) + lam_init
    att = _attention(lam.reshape(1), qt, px, vtx, pc, vtc, tq=tiles.attn_queries, tkc=tiles.attn_keys)

    dmat, qdec, kdec, gch = _retention_tables(ret_decay_fwd[l], ret_decay_bwd[l])
    ret_f, ret_b = _retention(px, pc, dmat, qdec, kdec, gch, chunks_per_step=tiles.ret_chunks)

    x1, h2p, slot_e, slot_w, slot_r, counts = _merge(
        att, ret_f, ret_b, px, x, g1, sh2, sc2,
        dattn_norm_g[l].reshape(1, 2 * HEAD_DIM), ret_norm_g[l].reshape(1, R_DV),
        w_out[l].astype(BF16), norm2_g[l].reshape(1, d), router_w[l].T.astype(BF16),
        router_bias[l].astype(F32).reshape(N_EXPERTS, 1), 1.0 - lam_init, tm=tiles.proj_tokens)

    t = b * s
    dest_t, iblk, iexp, ilo, ihi = _route_meta(counts[:, 0].astype(jnp.int32), slot_e, slot_r,
                                               tiles.expert_rows)
    dest_rows = (dest_t.T * ROW_CHUNKS).reshape(t * TOP_K)
    xs = _dispatch(dest_rows, h2p, tm=tiles.moe_tokens)
    ys = _experts(iblk, iexp, ilo, ihi, xs, exp_w1[l], exp_w3[l], exp_w2[l], tiles.expert_rows)
    out = _combine(dest_rows, x1.reshape(t, d), h2p, slot_w.T, g2, shared_w1[l].astype(BF16),
                   shared_w3[l].astype(BF16), shared_w2[l].astype(BF16), final_norm_g.reshape(1, d),
                   ys, s // tiles.moe_tokens, tiles.moe_tokens)
    return out.reshape(b, s, d)
```

```python
import functools
import math
from typing import NamedTuple

import jax
import jax.numpy as jnp
import numpy as np
from jax import lax
from jax.experimental import pallas as pl
from jax.experimental.pallas import tpu as pltpu

F32 = jnp.float32
BF16 = jnp.bfloat16
U32 = jnp.uint32

D_MODEL = 1024
GRID_W = 64
HEAD_DIM = 64
A_HEADS = 4
R_HEADS = 4
R_DK = 64
R_DV = 128
RET_CHUNK = 128
ROPE_BASE = 10000.0
N_EXPERTS = 64
N_GROUPS = 8
TOPK_GROUPS = 4
TOP_K = 8
D_EXPERT = 256
ROUTED_SCALE = 2.5
NORM_EPS = 1e-6
LANES = 128

P_KZ = 0
P_RQ = 1024
P_RK = 1280
P_RV = 1536
P_RG = 2048
P_COLS = 2560
VT_ROWS = 144
LOG2E = 1.4426950408889634

VMEM_LIMIT = 56 * 1024 * 1024


class _Tiles(NamedTuple):
    proj_tokens: int
    attn_queries: int
    attn_keys: int
    ret_chunks: int
    dispatch_tokens: int
    combine_tokens: int
    expert_rows: int


def _tile_plan(seq_len):
    return _Tiles(proj_tokens=min(512, seq_len), attn_queries=min(2048, seq_len),
                  attn_keys=min(512, seq_len // 4), ret_chunks=min(8, seq_len // RET_CHUNK),
                  dispatch_tokens=256, combine_tokens=min(512, seq_len), expert_rows=1024)


def _cparams(sem):
    return pltpu.CompilerParams(dimension_semantics=sem, vmem_limit_bytes=VMEM_LIMIT)


def _mod_kernel(c_ref, w_ref, b_ref, o_ref):
    c = c_ref[...]
    s = c * jax.nn.sigmoid(c)
    o_ref[...] = jnp.dot(s, w_ref[...], preferred_element_type=F32,
                         precision=lax.Precision.HIGHEST) + b_ref[...]


def _modulation(c8, ada_w, ada_b):
    d = c8.shape[1]
    n = ada_w.shape[1]
    tn = 512
    return pl.pallas_call(
        _mod_kernel,
        grid=(n // tn,),
        in_specs=[pl.BlockSpec((8, d), lambda j: (0, 0)),
                  pl.BlockSpec((d, tn), lambda j: (0, j)),
                  pl.BlockSpec((1, tn), lambda j: (0, j))],
        out_specs=pl.BlockSpec((8, tn), lambda j: (0, j)),
        out_shape=jax.ShapeDtypeStruct((8, n), F32),
        compiler_params=_cparams(("arbitrary",)),
        name="modulation",
    )(c8, ada_w, ada_b.reshape(1, n))


def _rmsnorm_rows(x, g):
    return x * lax.rsqrt(jnp.mean(x * x, axis=-1, keepdims=True) + NORM_EPS) * g


def _inproj_kernel(x_ref, g_ref, sc_ref, sh_ref, cos_ref, sin_ref, cost_ref, sint_ref, w_ref, wt_ref,
                   p_ref, qt_ref, vt_ref):
    x = x_ref[0]
    tm = x.shape[0]
    h = _rmsnorm_rows(x, g_ref[...]) * (1.0 + sc_ref[0]) + sh_ref[0]
    hb = h.astype(BF16)
    p = jnp.dot(hb, w_ref[...], preferred_element_type=F32)
    pt = lax.dot_general(wt_ref[...], hb, _NT, preferred_element_type=F32)

    cos = cos_ref[...]
    sin = sin_ref[...]
    lane = lax.broadcasted_iota(jnp.int32, (tm, LANES), 1)
    first_half = (lane % 32) < 16
    low64 = lane < 64

    def rope(xs):
        partner = jnp.where(first_half, pltpu.roll(xs, LANES - 16, 1), pltpu.roll(xs, 16, 1))
        return xs * cos + partner * sin

    zero = jnp.zeros((tm, LANES), BF16)
    for hh in range(A_HEADS):
        k = rope(p[:, hh * 128:(hh + 1) * 128]).astype(BF16)
        p_ref[0, :, P_KZ + hh * 256:P_KZ + hh * 256 + 128] = jnp.where(low64, k, zero)
        p_ref[0, :, P_KZ + hh * 256 + 128:P_KZ + (hh + 1) * 256] = jnp.where(low64, zero, k)
    for cc in range(2):
        rq = rope(p[:, 512 + cc * 128:512 + (cc + 1) * 128])
        p_ref[0, :, P_RQ + cc * 128:P_RQ + (cc + 1) * 128] = rq.astype(BF16)
        rk = rope(p[:, 768 + cc * 128:768 + (cc + 1) * 128]) * (R_DK ** -0.5)
        p_ref[0, :, P_RK + cc * 128:P_RK + (cc + 1) * 128] = rk.astype(BF16)
    p_ref[0, :, P_RV:P_RV + 512] = p[:, 1024:1536].astype(BF16)
    p_ref[0, :, P_RG:P_RG + 512] = p[:, 1536:2048].astype(BF16)

    cost = cost_ref[...]
    sint = sint_ref[...]
    q_scale = (HEAD_DIM ** -0.5) * LOG2E
    ones_rows = jnp.where(lax.broadcasted_iota(jnp.int32, (VT_ROWS - 128, tm), 0) == 0, 1.0, 0.0)
    for hh in range(A_HEADS):
        xq = pt[hh * 128:(hh + 1) * 128, :]
        groups = [xq[16 * gi:16 * (gi + 1), :] for gi in range(8)]
        partner = jnp.concatenate([groups[gi ^ 1] for gi in range(8)], axis=0)
        qt_ref[0, hh] = ((xq * cost + partner * sint) * q_scale).astype(BF16)
        vt_ref[0, hh, 0:128, :] = pt[512 + hh * 128:512 + (hh + 1) * 128, :].astype(BF16)
        vt_ref[0, hh, 128:VT_ROWS, :] = ones_rows.astype(BF16)


def _inproj(x, g, sc, sh, cos, sin, cost, sint, w_bf, wt_bf, tm):
    b, s, d = x.shape
    n_tok = w_bf.shape[1]
    n_t = wt_bf.shape[0]
    const2 = lambda shape: pl.BlockSpec(shape, lambda bi, i: (0, 0))
    per_b = pl.BlockSpec((1, 1, d), lambda bi, i: (bi, 0, 0))
    return pl.pallas_call(
        _inproj_kernel,
        grid=(b, s // tm),
        in_specs=[pl.BlockSpec((1, tm, d), lambda bi, i: (bi, i, 0)),
                  const2((1, d)), per_b, per_b,
                  pl.BlockSpec((tm, LANES), lambda bi, i: (i, 0)),
                  pl.BlockSpec((tm, LANES), lambda bi, i: (i, 0)),
                  pl.BlockSpec((LANES, tm), lambda bi, i: (0, i)),
                  pl.BlockSpec((LANES, tm), lambda bi, i: (0, i)),
                  const2((d, n_tok)), const2((n_t, d))],
        out_specs=[pl.BlockSpec((1, tm, P_COLS), lambda bi, i: (bi, i, 0)),
                   pl.BlockSpec((1, A_HEADS, LANES, tm), lambda bi, i: (bi, 0, 0, i)),
                   pl.BlockSpec((1, A_HEADS, VT_ROWS, tm), lambda bi, i: (bi, 0, 0, i))],
        out_shape=[jax.ShapeDtypeStruct((b, s, P_COLS), BF16),
                   jax.ShapeDtypeStruct((b, A_HEADS, LANES, s), BF16),
                   jax.ShapeDtypeStruct((b, A_HEADS, VT_ROWS, s), BF16)],
        compiler_params=_cparams(("arbitrary", "arbitrary")),
        name="inproj",
    )(x, g, sc, sh, cos, sin, cost, sint, w_bf, wt_bf)


_NT = (((1,), (1,)), ((), ()))


def _attn_kernel(lam_ref, qt_ref, kzc_ref, vtc_ref, kzx_ref, vtx_ref, o_ref,
                 acc, s_a, s_b, e_a, e_b, *, tkc):
    qt = qt_ref[0, 0]
    tq = qt.shape[1]
    n_chunks = kzx_ref.shape[1] // tkc
    assert n_chunks % 2 == 0 and n_chunks >= 4

    def scores(kz, s_ref):
        for mp in range(2):
            s_ref[mp] = jnp.dot(kz[:, mp * LANES:(mp + 1) * LANES], qt, preferred_element_type=F32)

    def softmax(s_ref, e_ref, ms):
        new_ms, alphas = [], []
        for mp in range(2):
            st = s_ref[mp]
            m_new = jnp.maximum(ms[mp], jnp.max(st, axis=0, keepdims=True))
            alphas.append(jnp.exp2(ms[mp] - m_new))
            e_ref[mp] = jnp.exp2(st - m_new).astype(BF16)
            new_ms.append(m_new)
        return tuple(new_ms), tuple(alphas)

    def values(vt, e_ref, alphas):
        for mp in range(2):
            acc[mp] = alphas[mp] * acc[mp] + jnp.dot(vt, e_ref[mp], preferred_element_type=F32)

    def kchunk(c):
        return kzx_ref[0, pl.ds(pl.multiple_of(c * tkc, tkc), tkc), :]

    def vchunk(c):
        return vtx_ref[0, 0, :, pl.ds(pl.multiple_of(c * tkc, tkc), tkc)]

    ms = []
    for mp in range(2):
        st = jnp.dot(kzc_ref[0, :, mp * LANES:(mp + 1) * LANES], qt, preferred_element_type=F32)
        m_new = jnp.max(st, axis=0, keepdims=True)
        acc[mp] = jnp.dot(vtc_ref[0, 0], jnp.exp2(st - m_new).astype(BF16), preferred_element_type=F32)
        ms.append(m_new)
    ms = tuple(ms)

    scores(kchunk(0), s_a)
    scores(kchunk(1), s_b)
    ms, al_a = softmax(s_a, e_a, ms)

    def body(j, carry):
        ms, al_a = carry
        scores(kchunk(2 * j + 2), s_a)
        ms, al_b = softmax(s_b, e_b, ms)
        values(vchunk(2 * j), e_a, al_a)
        scores(kchunk(2 * j + 3), s_b)
        ms, al_a = softmax(s_a, e_a, ms)
        values(vchunk(2 * j + 1), e_b, al_b)
        return ms, al_a

    ms, al_a = lax.fori_loop(0, n_chunks // 2 - 1, body, (ms, al_a))
    ms, al_b = softmax(s_b, e_b, ms)
    values(vchunk(n_chunks - 2), e_a, al_a)
    values(vchunk(n_chunks - 1), e_b, al_b)

    a1 = acc[0]
    a2 = acc[1]
    o1 = a1[:LANES, :] / a1[LANES:LANES + 1, :]
    o2 = a2[:LANES, :] / a2[LANES:LANES + 1, :]
    o_ref[0] = (o1 - lam_ref[0] * o2).T.astype(BF16)


def _attention(lam, qt, px, vtx, pc, vtc, tq, tkc):
    b, s, _ = px.shape
    lc = pc.shape[1]
    kern = functools.partial(_attn_kernel, tkc=tkc)
    return pl.pallas_call(
        kern,
        grid=(b, A_HEADS, s // tq),
        in_specs=[pl.BlockSpec(memory_space=pltpu.SMEM),
                  pl.BlockSpec((1, 1, LANES, tq), lambda bi, h, i: (bi, h, 0, i)),
                  pl.BlockSpec((1, lc, 256), lambda bi, h, i: (bi, 0, P_KZ // 256 + h)),
                  pl.BlockSpec((1, 1, VT_ROWS, lc), lambda bi, h, i: (bi, h, 0, 0)),
                  pl.BlockSpec((1, s, 256), lambda bi, h, i: (bi, 0, P_KZ // 256 + h)),
                  pl.BlockSpec((1, 1, VT_ROWS, s), lambda bi, h, i: (bi, h, 0, 0))],
        out_specs=pl.BlockSpec((1, tq, LANES), lambda bi, h, i: (bi, i, h)),
        out_shape=jax.ShapeDtypeStruct((b, s, A_HEADS * LANES), BF16),
        scratch_shapes=[pltpu.VMEM((2, VT_ROWS, tq), F32),
                        pltpu.VMEM((2, tkc, tq), F32), pltpu.VMEM((2, tkc, tq), F32),
                        pltpu.VMEM((2, tkc, tq), BF16), pltpu.VMEM((2, tkc, tq), BF16)],
        compiler_params=_cparams(("arbitrary", "arbitrary", "arbitrary")),
        name="diff_attention",
    )(lam, qt, pc, vtc, px, vtx)


_TN = (((0,), (0,)), ((), ()))


def _ret_kernel(qf_ref, kf_ref, vf_ref, qb_ref, kb_ref, vb_ref, kc_ref, vc_ref,
                dmat_ref, qdec_ref, kdec_ref, gch_ref, of_ref, ob_ref, sf, sb):
    c = RET_CHUNK
    n = pl.program_id(1)
    col_head = lax.broadcasted_iota(jnp.int32, (c, R_HEADS * R_DK), 1) // R_DK
    bd_mask = (lax.broadcasted_iota(jnp.int32, (R_HEADS * R_DK, R_HEADS * R_DV), 0) // R_DK
               == lax.broadcasted_iota(jnp.int32, (R_HEADS * R_DK, R_HEADS * R_DV), 1) // R_DV)

    def state_update(s_ref, k, v, kdec, gch):
        kd = (k.astype(F32) * kdec).astype(BF16)
        kv = lax.dot_general(kd, v, _TN, preferred_element_type=F32)
        s_ref[...] = gch * s_ref[...] + jnp.where(bd_mask, kv, 0.0)

    def chunk(s_ref, q, k, v, dmat, qdec, kdec, gch):
        zero = jnp.zeros_like(k)
        kblk = jnp.concatenate([jnp.where(col_head == h, k, zero) for h in range(R_HEADS)], axis=0)
        sc = lax.dot_general(q, kblk, _NT, preferred_element_type=F32) * dmat
        p = sc.astype(BF16)
        intra = jnp.concatenate(
            [jnp.dot(p[:, h * c:(h + 1) * c], v[:, h * R_DV:(h + 1) * R_DV], preferred_element_type=F32)
             for h in range(R_HEADS)], axis=1)
        qd = (q.astype(F32) * qdec).astype(BF16)
        cross = jnp.dot(qd, s_ref[...].astype(BF16), preferred_element_type=F32)
        out = intra + cross
        state_update(s_ref, k, v, kdec, gch)
        return out

    @pl.when(n == 0)
    def _():
        sf[...] = jnp.zeros_like(sf)
        sb[...] = jnp.zeros_like(sb)
        n_ctx = kc_ref.shape[1] // c
        for cc in range(n_ctx):
            state_update(sf, kc_ref[0, cc * c:(cc + 1) * c, :], vc_ref[0, cc * c:(cc + 1) * c, :],
                         kdec_ref[0], gch_ref[0])
        for cc in reversed(range(n_ctx)):
            state_update(sb, kc_ref[0, cc * c:(cc + 1) * c, :], vc_ref[0, cc * c:(cc + 1) * c, :],
                         kdec_ref[1], gch_ref[1])

    chunks_per_step = qf_ref.shape[1] // c
    for cc in range(chunks_per_step):
        rows = slice(cc * c, (cc + 1) * c)
        of_ref[0, rows, :] = chunk(sf, qf_ref[0, rows, :], kf_ref[0, rows, :], vf_ref[0, rows, :], dmat_ref[0],
                                   qdec_ref[0], kdec_ref[0], gch_ref[0]).astype(BF16)
    for cc in reversed(range(chunks_per_step)):
        rows = slice(cc * c, (cc + 1) * c)
        ob_ref[0, rows, :] = chunk(sb, qb_ref[0, rows, :], kb_ref[0, rows, :], vb_ref[0, rows, :], dmat_ref[1],
                                   qdec_ref[1], kdec_ref[1], gch_ref[1]).astype(BF16)


def _retention(px, pc, dmat, qdec, kdec, gch, chunks_per_step):
    b, s, _ = px.shape
    lc = pc.shape[1]
    c = RET_CHUNK
    rows = c * chunks_per_step
    nch = s // rows
    qk_w = R_HEADS * R_DK
    v_w = R_HEADS * R_DV
    fwd = lambda bi, n: (bi, n, 0)
    spec_q = lambda im: pl.BlockSpec((1, rows, qk_w), lambda bi, n: im(bi, n)[:2] + (P_RQ // qk_w,))
    spec_k = lambda im: pl.BlockSpec((1, rows, qk_w), lambda bi, n: im(bi, n)[:2] + (P_RK // qk_w,))
    spec_v = lambda im: pl.BlockSpec((1, rows, v_w), lambda bi, n: im(bi, n)[:2] + (P_RV // v_w,))
    bwd = lambda bi, n: (bi, nch - 1 - n, 0)
    const3 = lambda shape: pl.BlockSpec(shape, lambda bi, n: (0, 0, 0))
    return pl.pallas_call(
        _ret_kernel,
        grid=(b, nch),
        in_specs=[spec_q(fwd), spec_k(fwd), spec_v(fwd), spec_q(bwd), spec_k(bwd), spec_v(bwd),
                  pl.BlockSpec((1, lc, qk_w), lambda bi, n: (bi, 0, P_RK // qk_w)),
                  pl.BlockSpec((1, lc, v_w), lambda bi, n: (bi, 0, P_RV // v_w)),
                  const3((2, c, R_HEADS * c)), const3((2, c, qk_w)), const3((2, c, qk_w)),
                  const3((2, 1, v_w))],
        out_specs=[pl.BlockSpec((1, rows, v_w), fwd), pl.BlockSpec((1, rows, v_w), bwd)],
        out_shape=[jax.ShapeDtypeStruct((b, s, v_w), BF16), jax.ShapeDtypeStruct((b, s, v_w), BF16)],
        scratch_shapes=[pltpu.VMEM((qk_w, v_w), F32), pltpu.VMEM((qk_w, v_w), F32)],
        compiler_params=_cparams(("arbitrary", "arbitrary")),
        name="retention",
    )(px, px, px, px, px, px, pc, pc, dmat, qdec, kdec, gch)


def _retention_tables(decay_fwd, decay_bwd):
    c = RET_CHUNK
    pos = jnp.arange(c, dtype=F32)
    rel = pos[:, None] - pos[None, :]

    def tables(logits, backward):
        lg = jax.nn.log_sigmoid(logits.astype(F32))
        r = -rel if backward else rel
        dm = jnp.where(r[None] >= 0, jnp.exp(jnp.maximum(r, 0.0)[None] * lg[:, None, None]), 0.0)
        dm = jnp.transpose(dm, (1, 0, 2)).reshape(c, R_HEADS * c)
        qpow = (c - pos) if backward else (pos + 1)
        kpow = pos if backward else (c - 1 - pos)
        qd = jnp.repeat(jnp.exp(qpow[:, None] * lg[None, :]), R_DK, axis=1)
        kd = jnp.repeat(jnp.exp(kpow[:, None] * lg[None, :]), R_DK, axis=1)
        gc = jnp.repeat(jnp.exp(c * lg), R_DV)[None, :]
        return dm, qd, kd, gc

    tf = tables(decay_fwd, False)
    tb = tables(decay_bwd, True)
    return tuple(jnp.stack([a, bb]) for a, bb in zip(tf, tb))


def _pack_pairs(x):
    k = x.shape[1] // 2
    lo = lax.bitcast_convert_type(x[:, :k].astype(BF16).astype(F32), U32)
    hi = lax.bitcast_convert_type(x[:, k:].astype(BF16).astype(F32), U32)
    return (lo >> 16) | (hi & jnp.uint32(0xFFFF0000))


def _unpack_pairs(w):
    lo = lax.bitcast_convert_type(w << 16, F32)
    hi = lax.bitcast_convert_type(w & jnp.uint32(0xFFFF0000), F32)
    return lo, hi


ROW_CHUNKS = (D_MODEL // 2) // LANES


def _store_token_rows(ref, first_token, n_tokens, x):
    packed = _pack_pairs(x)
    for c in range(ROW_CHUNKS):
        ref[pl.ds(first_token * ROW_CHUNKS + c, n_tokens, stride=ROW_CHUNKS), :] = (
            packed[:, c * LANES:(c + 1) * LANES])


def _load_token_rows(ref, first_token, n_tokens):
    return [_unpack_pairs(ref[pl.ds(first_token * ROW_CHUNKS + c, n_tokens, stride=ROW_CHUNKS), :])
            for c in range(ROW_CHUNKS)]


def _rows_to_features(chunks):
    return jnp.concatenate([lo for lo, _ in chunks] + [hi for _, hi in chunks], axis=1)


def _route_tile(logits_t, bias_col, carry_ref):
    n_e, tm = logits_t.shape
    per = n_e // N_GROUPS
    neg = -jnp.inf
    scores = jax.nn.sigmoid(logits_t)
    sel = scores + bias_col
    sub = lax.broadcasted_iota(jnp.int32, (per, tm), 0).astype(F32)
    sel_g = [sel[g * per:(g + 1) * per, :] for g in range(N_GROUPS)]
    flat_g = [sub + float(g * per) for g in range(N_GROUPS)]

    gs = []
    for g in range(N_GROUPS):
        top1 = jnp.max(sel_g[g], axis=0, keepdims=True)
        idx1 = jnp.min(jnp.where(sel_g[g] == top1, sub, float(per)), axis=0, keepdims=True)
        top2 = jnp.max(jnp.where(sub == idx1, neg, sel_g[g]), axis=0, keepdims=True)
        gs.append(top1 + top2)
    gsel = [jnp.zeros((1, tm), F32) for _ in range(N_GROUPS)]
    for _ in range(TOPK_GROUPS):
        m = functools.reduce(jnp.maximum, gs)
        found = jnp.zeros((1, tm), F32)
        for g in range(N_GROUPS):
            hit = jnp.logical_and(gs[g] == m, found == 0.0)
            found = jnp.where(hit, 1.0, found)
            gsel[g] = jnp.where(hit, 1.0, gsel[g])
            gs[g] = jnp.where(hit, neg, gs[g])

    masked = [jnp.where(gsel[g] > 0.0, sel_g[g], neg) for g in range(N_GROUPS)]
    chosen = [jnp.zeros((per, tm), F32) for _ in range(N_GROUPS)]
    for _ in range(TOP_K):
        m = functools.reduce(jnp.maximum, [jnp.max(mg, axis=0, keepdims=True) for mg in masked])
        idx = functools.reduce(jnp.minimum, [
            jnp.min(jnp.where(masked[g] == m, flat_g[g], float(n_e)), axis=0, keepdims=True)
            for g in range(N_GROUPS)])
        for g in range(N_GROUPS):
            hit = flat_g[g] == idx
            chosen[g] = jnp.where(hit, 1.0, chosen[g])
            masked[g] = jnp.where(hit, neg, masked[g])
    chosen = jnp.concatenate(chosen, axis=0)
    flat = jnp.concatenate(flat_g, axis=0)

    w_dense = scores * chosen
    w_dense = w_dense / jnp.sum(w_dense, axis=0, keepdims=True) * ROUTED_SCALE

    chosen_bf = chosen.astype(BF16)
    lower = (lax.broadcasted_iota(jnp.int32, (n_e, n_e), 1)
             < lax.broadcasted_iota(jnp.int32, (n_e, n_e), 0)).astype(BF16)
    slot_idx = jnp.dot(lower, chosen_bf, preferred_element_type=F32)
    upper = (lax.broadcasted_iota(jnp.int32, (tm, tm), 0)
             < lax.broadcasted_iota(jnp.int32, (tm, tm), 1)).astype(BF16)
    rank = jnp.dot(chosen_bf, upper, preferred_element_type=F32) + carry_ref[...]
    carry_ref[...] += jnp.sum(chosen, axis=1, keepdims=True)

    slots = []
    for j in range(TOP_K):
        ind = jnp.where(slot_idx == float(j), chosen, 0.0)
        slots.append((jnp.sum(ind * flat, axis=0, keepdims=True),
                      jnp.sum(ind * w_dense, axis=0, keepdims=True),
                      jnp.sum(ind * rank, axis=0, keepdims=True)))
    return slots


def _merge_kernel(att_ref, rf_ref, rb_ref, gate_ref, x_ref, g1_ref, sh2_ref, sc2_ref, ga_ref, gr_ref,
                  wout_ref, n2g_ref, rwt_ref, rbias_ref, x1_ref, h2p_ref, se_ref, sw_ref, sr_ref, cnt_ref,
                  carry, *, att_scale):
    @pl.when(jnp.logical_and(pl.program_id(0) == 0, pl.program_id(1) == 0))
    def _():
        carry[...] = jnp.zeros_like(carry)

    att = att_ref[0].astype(F32)
    ret = rf_ref[0].astype(F32) + rb_ref[0].astype(F32)
    gate = gate_ref[0].astype(F32)
    parts = []
    for h in range(A_HEADS):
        a = att[:, h * 128:(h + 1) * 128]
        parts.append(_rmsnorm_rows(a, ga_ref[...]) * att_scale)
    for h in range(R_HEADS):
        r = _rmsnorm_rows(ret[:, h * R_DV:(h + 1) * R_DV], gr_ref[...])
        g = gate[:, h * R_DV:(h + 1) * R_DV]
        parts.append(g * jax.nn.sigmoid(g) * r)
    merged = jnp.concatenate(parts, axis=1).astype(BF16)
    y = jnp.dot(merged, wout_ref[...], preferred_element_type=F32)
    x1 = x_ref[0] + g1_ref[0] * y
    x1_ref[0] = x1
    h2 = _rmsnorm_rows(x1, n2g_ref[...]) * (1.0 + sc2_ref[0]) + sh2_ref[0]
    _store_token_rows(h2p_ref, 0, h2.shape[0], h2)
    logits_t = lax.dot_general(rwt_ref[...], h2.astype(BF16), _NT, preferred_element_type=F32)
    slots = _route_tile(logits_t, rbias_ref[...], carry)
    for j, (e_j, w_j, r_j) in enumerate(slots):
        se_ref[j:j + 1, :] = e_j.astype(jnp.int32)
        sw_ref[j:j + 1, :] = w_j
        sr_ref[j:j + 1, :] = r_j.astype(jnp.int32)
    cnt_ref[...] = carry[...]


def _merge(att, rf, rb, px, x, g1, sh2, sc2, ga, gr, wout_bf, n2g, rwt_bf, rbias, att_scale, tm):
    b, s, d = x.shape
    nt = s // tm
    chunked = pl.BlockSpec((ROW_CHUNKS * tm, LANES), lambda bi, i: (bi * nt + i, 0))
    chunked_shape = jax.ShapeDtypeStruct((b * s * ROW_CHUNKS, LANES), U32)
    row = lambda bi, i: (bi, i, 0)
    tok = lambda bi, i: (0, bi * nt + i)
    per_b = pl.BlockSpec((1, 1, d), lambda bi, i: (bi, 0, 0))
    const2 = lambda shape: pl.BlockSpec(shape, lambda bi, i: (0, 0))
    return pl.pallas_call(
        functools.partial(_merge_kernel, att_scale=att_scale),
        grid=(b, nt),
        in_specs=[pl.BlockSpec((1, tm, 512), row), pl.BlockSpec((1, tm, 512), row),
                  pl.BlockSpec((1, tm, 512), row),
                  pl.BlockSpec((1, tm, 512), lambda bi, i: (bi, i, P_RG // 512)),
                  pl.BlockSpec((1, tm, d), row), per_b, per_b, per_b,
                  const2((1, 128)), const2((1, R_DV)), const2((d, d)), const2((1, d)),
                  const2((N_EXPERTS, d)), const2((N_EXPERTS, 1))],
        out_specs=[pl.BlockSpec((1, tm, d), row), chunked,
                   pl.BlockSpec((TOP_K, tm), tok), pl.BlockSpec((TOP_K, tm), tok),
                   pl.BlockSpec((TOP_K, tm), tok), const2((N_EXPERTS, 1))],
        out_shape=[jax.ShapeDtypeStruct((b, s, d), F32), chunked_shape,
                   jax.ShapeDtypeStruct((TOP_K, b * s), jnp.int32),
                   jax.ShapeDtypeStruct((TOP_K, b * s), F32),
                   jax.ShapeDtypeStruct((TOP_K, b * s), jnp.int32),
                   jax.ShapeDtypeStruct((N_EXPERTS, 1), F32)],
        scratch_shapes=[pltpu.VMEM((N_EXPERTS, 1), F32)],
        compiler_params=_cparams(("arbitrary", "arbitrary")),
        name="mixer_merge",
    )(att, rf, rb, px, x, g1, sh2, sc2, ga, gr, wout_bf, n2g, rwt_bf, rbias)


def _dispatch_kernel(dest_ref, h_ref, xs_ref, buf, sem):
    rows = h_ref.shape[0]
    tm = rows // ROW_CHUNKS
    i = pl.program_id(0)
    n = pl.num_programs(0)
    slot = i % 2

    def wait_slot(s):
        for _ in range(TOP_K):
            pltpu.make_async_copy(buf.at[s], xs_ref.at[pl.ds(0, rows)], sem.at[s]).wait()

    @pl.when(i >= 2)
    def _():
        wait_slot(slot)

    buf[slot] = h_ref[...]

    def body(t, carry):
        src = buf.at[slot, pl.ds(pl.multiple_of(t * ROW_CHUNKS, ROW_CHUNKS), ROW_CHUNKS)]
        for j in range(TOP_K):
            d = pl.multiple_of(dest_ref[t * TOP_K + j], ROW_CHUNKS)
            pltpu.make_async_copy(src, xs_ref.at[pl.ds(d, ROW_CHUNKS)], sem.at[slot]).start(priority=j % 2)
        return carry

    lax.fori_loop(0, tm, body, 0)

    @pl.when(i == n - 1)
    def _():
        wait_slot(slot)

    @pl.when(jnp.logical_and(i == n - 1, n >= 2))
    def _():
        wait_slot(1 - slot)


def _dispatch(dest_rows, h2p, tm):
    n_rows = h2p.shape[0]
    n = n_rows // (ROW_CHUNKS * tm)
    return pl.pallas_call(
        _dispatch_kernel,
        grid=(n,),
        in_specs=[pl.BlockSpec((tm * TOP_K,), lambda i: (i,), memory_space=pltpu.SMEM),
                  pl.BlockSpec((ROW_CHUNKS * tm, LANES), lambda i: (i, 0))],
        out_specs=pl.BlockSpec(memory_space=pl.ANY),
        out_shape=jax.ShapeDtypeStruct((n_rows * TOP_K, LANES), U32),
        scratch_shapes=[pltpu.VMEM((2, ROW_CHUNKS * tm, LANES), U32), pltpu.SemaphoreType.DMA((2,))],
        compiler_params=_cparams(("arbitrary",)),
        name="moe_dispatch",
    )(dest_rows, h2p)


def _expert_kernel(iblk_ref, iexp_ref, ilo_ref, ihi_ref, xs_ref, w1_ref, w3_ref, w2_ref, ys_ref,
                   acc, hm, w1b, w3b, w2b):
    r = xs_ref.shape[0] // ROW_CHUNKS
    i = pl.program_id(0)
    n_items = pl.num_programs(0) - 1
    cur = jnp.minimum(i, n_items - 1)
    cur_prev = jnp.maximum(cur - 1, 0)
    dn = jnp.maximum(i - 1, 0)
    dn_prev = jnp.maximum(i - 2, 0)
    blk = iblk_ref[cur]
    lo = ilo_ref[cur]
    hi = ihi_ref[cur]
    dn_first = jnp.logical_or(i <= 1, iblk_ref[dn_prev] != iblk_ref[dn])
    slot = i % 2

    @pl.when(i == 0)
    def _():
        acc[...] = jnp.zeros_like(acc)
        hm[...] = jnp.zeros_like(hm)

    @pl.when(jnp.logical_or(i == 0, iexp_ref[cur_prev] != iexp_ref[cur]))
    def _():
        w1b[...] = w1_ref[0].astype(BF16)
        w3b[...] = w3_ref[0].astype(BF16)

    @pl.when(jnp.logical_or(i <= 1, iexp_ref[dn_prev] != iexp_ref[dn]))
    def _():
        w2b[...] = w2_ref[0].astype(BF16)

    n_half = 4
    rh = r // n_half

    def run(hm_rd, hm_wr):
        for hf in range(n_half):
            sl = slice(hf * rh, (hf + 1) * rh)
            y = jnp.dot(hm_rd[sl, :], w2b[...], preferred_element_type=F32)
            y = jnp.where(dn_first, y, acc[sl, :] + y)
            acc[sl, :] = y
            _store_token_rows(ys_ref, hf * rh, rh, y)

            rows = blk * r + hf * rh + lax.broadcasted_iota(jnp.int32, (rh, 1), 0)
            valid = jnp.logical_and(rows >= lo, rows < hi)
            x = _rows_to_features(_load_token_rows(xs_ref, hf * rh, rh))
            x = jnp.where(valid, x, 0.0).astype(BF16)
            a = jnp.dot(x, w1b[...], preferred_element_type=F32)
            bb = jnp.dot(x, w3b[...], preferred_element_type=F32)
            hm_wr[sl, :] = (a * jax.nn.sigmoid(a) * bb).astype(BF16)

    @pl.when(slot == 0)
    def _():
        run(hm.at[1], hm.at[0])

    @pl.when(slot == 1)
    def _():
        run(hm.at[0], hm.at[1])


def _experts(iblk, iexp, ilo, ihi, xs, w1, w3, w2, r):
    na, kw = xs.shape
    rows = r * ROW_CHUNKS
    n_items = iblk.shape[0]
    d = w1.shape[1]
    de = w1.shape[2]
    up = lambda i: jnp.minimum(i, n_items - 1)
    down = lambda i: jnp.maximum(i - 1, 0)
    grid_spec = pltpu.PrefetchScalarGridSpec(
        num_scalar_prefetch=4,
        grid=(n_items + 1,),
        in_specs=[pl.BlockSpec((rows, kw), lambda i, ib, ie, il, ih: (ib[up(i)], 0)),
                  pl.BlockSpec((1, d, de), lambda i, ib, ie, il, ih: (ie[up(i)], 0, 0)),
                  pl.BlockSpec((1, d, de), lambda i, ib, ie, il, ih: (ie[up(i)], 0, 0)),
                  pl.BlockSpec((1, de, d), lambda i, ib, ie, il, ih: (ie[down(i)], 0, 0))],
        out_specs=pl.BlockSpec((rows, kw), lambda i, ib, ie, il, ih: (ib[down(i)], 0)),
        scratch_shapes=[pltpu.VMEM((r, d), F32), pltpu.VMEM((2, r, de), BF16),
                        pltpu.VMEM((d, de), BF16), pltpu.VMEM((d, de), BF16), pltpu.VMEM((de, d), BF16)],
    )
    return pl.pallas_call(
        _expert_kernel,
        grid_spec=grid_spec,
        out_shape=jax.ShapeDtypeStruct((na, kw), U32),
        compiler_params=_cparams(("arbitrary",)),
        name="moe_experts",
    )(iblk, iexp, ilo, ihi, xs, w1, w3, w2)


def _combine_kernel(dest_ref, dnext_ref, x1_ref, h2p_ref, wts_ref, g2_ref, sw1_ref, sw3_ref, sw2_ref,
                    fg_ref, ys_ref, o_ref, gbuf, sem):
    tm = x1_ref.shape[0]
    i = pl.program_id(0)
    n = pl.num_programs(0)
    slot = i % 2

    rows = ROW_CHUNKS * tm

    def gather_row(dref, s, t, off, j):
        d = pl.multiple_of(dref[t * TOP_K + j], ROW_CHUNKS)
        pltpu.make_async_copy(ys_ref.at[pl.ds(d, ROW_CHUNKS)], gbuf.at[s, j, pl.ds(off, ROW_CHUNKS)],
                              sem.at[s]).start(priority=j % 2)

    def wait_slot(s):
        for _ in range(TOP_K):
            pltpu.make_async_copy(ys_ref.at[pl.ds(0, rows)], gbuf.at[s, 0], sem.at[s]).wait()

    @pl.when(i == 0)
    def _():
        def body(t, carry):
            for j in range(TOP_K):
                gather_row(dest_ref, 0, t, pl.multiple_of(t * ROW_CHUNKS, ROW_CHUNKS), j)
            return carry
        lax.fori_loop(0, tm, body, 0)

    wait_slot(slot)
    for t in range(tm):
        for j in range(TOP_K):
            gather_row(dnext_ref, 1 - slot, t, t * ROW_CHUNKS, j)

    wts = wts_ref[...]
    moe = None
    for j in range(TOP_K):
        wj = wts[:, j:j + 1]
        yj = [(lo * wj, hi * wj) for lo, hi in _load_token_rows(gbuf.at[slot, j], 0, tm)]
        moe = yj if moe is None else [(a + lo, b + hi) for (a, b), (lo, hi) in zip(moe, yj)]
    h = _rows_to_features(_load_token_rows(h2p_ref, 0, tm)).astype(BF16)
    a = jnp.dot(h, sw1_ref[...], preferred_element_type=F32)
    bb = jnp.dot(h, sw3_ref[...], preferred_element_type=F32)
    shared = jnp.dot((a * jax.nn.sigmoid(a) * bb).astype(BF16), sw2_ref[...], preferred_element_type=F32)
    ffn = _rows_to_features(moe) + shared
    xo = x1_ref[...] + g2_ref[0] * ffn
    o_ref[...] = _rmsnorm_rows(xo, fg_ref[...])

    @pl.when(i == n - 1)
    def _():
        wait_slot(1 - slot)


def _combine(dest_rows, x1, h2p, wts, g2, sw1, sw3, sw2, fg, ys, tiles_per_batch, tm):
    t, d = x1.shape
    n = t // tm
    de = sw1.shape[1]
    const2 = lambda shape: pl.BlockSpec(shape, lambda i: (0, 0))
    return pl.pallas_call(
        _combine_kernel,
        grid=(n,),
        in_specs=[pl.BlockSpec((tm * TOP_K,), lambda i: (i,), memory_space=pltpu.SMEM),
                  pl.BlockSpec((tm * TOP_K,), lambda i: (jnp.minimum(i + 1, n - 1),),
                               memory_space=pltpu.SMEM),
                  pl.BlockSpec((tm, d), lambda i: (i, 0)),
                  pl.BlockSpec((ROW_CHUNKS * tm, LANES), lambda i: (i, 0)),
                  pl.BlockSpec((tm, TOP_K), lambda i: (i, 0)),
                  pl.BlockSpec((1, 1, d), lambda i: (i // tiles_per_batch, 0, 0)),
                  const2((d, de)), const2((d, de)), const2((de, d)), const2((1, d)),
                  pl.BlockSpec(memory_space=pl.ANY)],
        out_specs=pl.BlockSpec((tm, d), lambda i: (i, 0)),
        out_shape=jax.ShapeDtypeStruct((t, d), F32),
        scratch_shapes=[pltpu.VMEM((2, TOP_K, ROW_CHUNKS * tm, LANES), U32),
                        pltpu.SemaphoreType.DMA((2,))],
        compiler_params=_cparams(("arbitrary",)),
        name="moe_combine",
    )(dest_rows, dest_rows, x1, h2p, wts, g2, sw1, sw3, sw2, fg, ys)


def _route_meta(counts, slot_e, slot_r, block_rows):
    t = slot_e.shape[1]
    ends = jnp.cumsum(counts)
    starts = ends - counts
    experts = jnp.arange(N_EXPERTS, dtype=jnp.int32)
    dest_t = slot_r + jnp.sum(jnp.where(slot_e[:, :, None] == experts, starts, 0), axis=-1)

    n_blk = t * TOP_K // block_rows
    n_items = n_blk + N_EXPERTS - 1
    first_blk = starts // block_rows
    last_blk = (ends - 1) // block_rows
    n_e = jnp.where(counts > 0, last_blk - first_blk + 1, 0)
    item_end = jnp.cumsum(n_e)
    item_off = item_end - n_e
    total = item_end[-1]
    idx = jnp.arange(n_items, dtype=jnp.int32)
    e_of = jnp.minimum(jnp.sum(item_end[None, :] <= idx[:, None], axis=1), N_EXPERTS - 1).astype(jnp.int32)
    onehot = e_of[:, None] == experts[None, :]
    lut = lambda table: jnp.sum(jnp.where(onehot, table[None, :], 0), axis=1)
    live = idx < total
    blk = lut(first_blk) + (idx - lut(item_off))
    lo = jnp.maximum(lut(starts), blk * block_rows)
    hi = jnp.minimum(lut(ends), (blk + 1) * block_rows)
    last_e = jnp.sum(jnp.where(idx == total - 1, e_of, 0))
    iblk = jnp.where(live, blk, n_blk - 1).astype(jnp.int32)
    iexp = jnp.where(live, e_of, last_e).astype(jnp.int32)
    ilo = jnp.where(live, lo, 0).astype(jnp.int32)
    ihi = jnp.where(live, hi, 0).astype(jnp.int32)
    return dest_t, iblk, iexp, ilo, ihi


def _rope_tables(n_tokens):
    n_rows = n_tokens // GRID_W
    rows = np.repeat(np.arange(n_rows, dtype=np.float32), GRID_W)
    cols = np.tile(np.arange(GRID_W, dtype=np.float32), n_rows)
    n_freq = HEAD_DIM // 4
    inv_freq = (np.float32(ROPE_BASE) ** (-np.arange(n_freq, dtype=np.float32) / n_freq)).astype(np.float32)
    ar = rows[:, None] * inv_freq
    ac = cols[:, None] * inv_freq
    cos64 = np.concatenate([np.cos(ar), np.cos(ar), np.cos(ac), np.cos(ac)], axis=1)
    sin64 = np.concatenate([-np.sin(ar), np.sin(ar), -np.sin(ac), np.sin(ac)], axis=1)
    return np.tile(cos64, (1, 2)).astype(np.float32), np.tile(sin64, (1, 2)).astype(np.float32)


def kernel(x, c, ctx, c_ctx, ada_w, ada_b, norm1_g, w_in, lambda_q1, lambda_k1, lambda_q2, lambda_k2,
           dattn_norm_g, ret_decay_fwd, ret_decay_bwd, ret_norm_g, w_out, norm2_g, router_w, router_bias,
           exp_w1, exp_w3, exp_w2, shared_w1, shared_w3, shared_w2, final_norm_g):
    b, s, d = x.shape
    lc = ctx.shape[1]
    depth = ada_w.shape[0]
    assert depth == 1, "single-layer problem: the context stream is never updated"
    l = 0
    lam_init = 0.8 - 0.6 * math.exp(-0.3 * l)

    c8 = jnp.zeros((8, d), F32).at[:b].set(c).at[b].set(c_ctx)
    mod = _modulation(c8, ada_w[l], ada_b[l])
    sh1, sc1, g1, sh2, sc2, g2 = [mod[:b, k * d:(k + 1) * d].reshape(b, 1, d) for k in range(6)]
    csh1 = jnp.broadcast_to(mod[b, 0:d].reshape(1, 1, d), (b, 1, d))
    csc1 = jnp.broadcast_to(mod[b, d:2 * d].reshape(1, 1, d), (b, 1, d))

    w_l = w_in[l]
    w_tok = jnp.concatenate([w_l[:, 512:1024], w_l[:, 1536:3072]], axis=1).astype(BF16)
    w_feat = jnp.concatenate([w_l[:, 0:512], w_l[:, 1024:1536]], axis=1).T.astype(BF16)
    n1g = norm1_g[l].reshape(1, d)
    cos, sin = _rope_tables(s)
    ones_c = jnp.ones((lc, LANES), F32)
    zeros_c = jnp.zeros((lc, LANES), F32)
    tiles = _tile_plan(s)
    px, qt, vtx = _inproj(x, n1g, sc1, sh1, cos, sin, np.ascontiguousarray(cos.T),
                          np.ascontiguousarray(sin.T), w_tok, w_feat, tm=tiles.proj_tokens)
    pc, _, vtc = _inproj(ctx, n1g, csc1, csh1, ones_c, zeros_c, ones_c.T, zeros_c.T, w_tok, w_feat, tm=lc)

    lam = (jnp.exp(jnp.sum(lambda_q1[l] * lambda_k1[l])) - jnp.exp(jnp.sum(lambda_q2[l] * lambda_k2[l]))
           ).astype(F32) + lam_init
    att = _attention(lam.reshape(1), qt, px, vtx, pc, vtc, tq=tiles.attn_queries, tkc=tiles.attn_keys)

    dmat, qdec, kdec, gch = _retention_tables(ret_decay_fwd[l], ret_decay_bwd[l])
    ret_f, ret_b = _retention(px, pc, dmat, qdec, kdec, gch, chunks_per_step=tiles.ret_chunks)

    x1, h2p, slot_e, slot_w, slot_r, counts = _merge(
        att, ret_f, ret_b, px, x, g1, sh2, sc2,
        dattn_norm_g[l].reshape(1, 2 * HEAD_DIM), ret_norm_g[l].reshape(1, R_DV),
        w_out[l].astype(BF16), norm2_g[l].reshape(1, d), router_w[l].T.astype(BF16),
        router_bias[l].astype(F32).reshape(N_EXPERTS, 1), 1.0 - lam_init, tm=tiles.proj_tokens)

    t = b * s
    dest_t, iblk, iexp, ilo, ihi = _route_meta(counts[:, 0].astype(jnp.int32), slot_e, slot_r,
                                               tiles.expert_rows)
    dest_rows = (dest_t.T * ROW_CHUNKS).reshape(t * TOP_K)
    xs = _dispatch(dest_rows, h2p, tm=tiles.dispatch_tokens)
    ys = _experts(iblk, iexp, ilo, ihi, xs, exp_w1[l], exp_w3[l], exp_w2[l], tiles.expert_rows)
    out = _combine(dest_rows, x1.reshape(t, d), h2p, slot_w.T, g2, shared_w1[l].astype(BF16),
                   shared_w3[l].astype(BF16), shared_w2[l].astype(BF16), final_norm_g.reshape(1, d),
                   ys, s // tiles.combine_tokens, tiles.combine_tokens)
    return out.reshape(b, s, d)
```

```python
import functools
import math
from typing import NamedTuple

import jax
import jax.numpy as jnp
import numpy as np
from jax import lax
from jax.experimental import pallas as pl
from jax.experimental.pallas import tpu as pltpu

F32 = jnp.float32
BF16 = jnp.bfloat16
U32 = jnp.uint32

D_MODEL = 1024
GRID_W = 64
HEAD_DIM = 64
A_HEADS = 4
R_HEADS = 4
R_DK = 64
R_DV = 128
RET_CHUNK = 128
ROPE_BASE = 10000.0
N_EXPERTS = 64
N_GROUPS = 8
TOPK_GROUPS = 4
TOP_K = 8
D_EXPERT = 256
ROUTED_SCALE = 2.5
NORM_EPS = 1e-6
LANES = 128

P_KZ = 0
P_RQ = 1024
P_RK = 1280
P_RV = 1536
P_RG = 2048
P_COLS = 2560
VT_ROWS = 144
LOG2E = 1.4426950408889634

VMEM_LIMIT = 56 * 1024 * 1024


class _Tiles(NamedTuple):
    proj_tokens: int
    attn_queries: int
    attn_keys: int
    ret_chunks: int
    dispatch_tokens: int
    combine_tokens: int
    expert_rows: int


def _tile_plan(seq_len):
    return _Tiles(proj_tokens=min(512, seq_len), attn_queries=min(2048, seq_len),
                  attn_keys=min(512, seq_len // 4), ret_chunks=min(8, seq_len // RET_CHUNK),
                  dispatch_tokens=256, combine_tokens=256, expert_rows=1024)


def _cparams(sem):
    return pltpu.CompilerParams(dimension_semantics=sem, vmem_limit_bytes=VMEM_LIMIT)


def _mod_kernel(c_ref, w_ref, b_ref, o_ref):
    c = c_ref[...]
    s = c * jax.nn.sigmoid(c)
    o_ref[...] = jnp.dot(s, w_ref[...], preferred_element_type=F32,
                         precision=lax.Precision.HIGHEST) + b_ref[...]


def _modulation(c8, ada_w, ada_b):
    d = c8.shape[1]
    n = ada_w.shape[1]
    tn = 512
    return pl.pallas_call(
        _mod_kernel,
        grid=(n // tn,),
        in_specs=[pl.BlockSpec((8, d), lambda j: (0, 0)),
                  pl.BlockSpec((d, tn), lambda j: (0, j)),
                  pl.BlockSpec((1, tn), lambda j: (0, j))],
        out_specs=pl.BlockSpec((8, tn), lambda j: (0, j)),
        out_shape=jax.ShapeDtypeStruct((8, n), F32),
        compiler_params=_cparams(("arbitrary",)),
        name="modulation",
    )(c8, ada_w, ada_b.reshape(1, n))


def _rmsnorm_rows(x, g):
    return x * lax.rsqrt(jnp.mean(x * x, axis=-1, keepdims=True) + NORM_EPS) * g


def _inproj_kernel(x_ref, g_ref, sc_ref, sh_ref, cos_ref, sin_ref, cost_ref, sint_ref, w_ref, wt_ref,
                   p_ref, qt_ref, vt_ref):
    x = x_ref[0]
    tm = x.shape[0]
    h = _rmsnorm_rows(x, g_ref[...]) * (1.0 + sc_ref[0]) + sh_ref[0]
    hb = h.astype(BF16)
    p = jnp.dot(hb, w_ref[...], preferred_element_type=F32)
    pt = lax.dot_general(wt_ref[...], hb, _NT, preferred_element_type=F32)

    cos = cos_ref[...]
    sin = sin_ref[...]
    lane = lax.broadcasted_iota(jnp.int32, (tm, LANES), 1)
    first_half = (lane % 32) < 16
    low64 = lane < 64

    def rope(xs):
        partner = jnp.where(first_half, pltpu.roll(xs, LANES - 16, 1), pltpu.roll(xs, 16, 1))
        return xs * cos + partner * sin

    zero = jnp.zeros((tm, LANES), BF16)
    for hh in range(A_HEADS):
        k = rope(p[:, hh * 128:(hh + 1) * 128]).astype(BF16)
        p_ref[0, :, P_KZ + hh * 256:P_KZ + hh * 256 + 128] = jnp.where(low64, k, zero)
        p_ref[0, :, P_KZ + hh * 256 + 128:P_KZ + (hh + 1) * 256] = jnp.where(low64, zero, k)
    for cc in range(2):
        rq = rope(p[:, 512 + cc * 128:512 + (cc + 1) * 128])
        p_ref[0, :, P_RQ + cc * 128:P_RQ + (cc + 1) * 128] = rq.astype(BF16)
        rk = rope(p[:, 768 + cc * 128:768 + (cc + 1) * 128]) * (R_DK ** -0.5)
        p_ref[0, :, P_RK + cc * 128:P_RK + (cc + 1) * 128] = rk.astype(BF16)
    p_ref[0, :, P_RV:P_RV + 512] = p[:, 1024:1536].astype(BF16)
    p_ref[0, :, P_RG:P_RG + 512] = p[:, 1536:2048].astype(BF16)

    cost = cost_ref[...]
    sint = sint_ref[...]
    q_scale = (HEAD_DIM ** -0.5) * LOG2E
    ones_rows = jnp.where(lax.broadcasted_iota(jnp.int32, (VT_ROWS - 128, tm), 0) == 0, 1.0, 0.0)
    for hh in range(A_HEADS):
        xq = pt[hh * 128:(hh + 1) * 128, :]
        groups = [xq[16 * gi:16 * (gi + 1), :] for gi in range(8)]
        partner = jnp.concatenate([groups[gi ^ 1] for gi in range(8)], axis=0)
        qt_ref[0, hh] = ((xq * cost + partner * sint) * q_scale).astype(BF16)
        vt_ref[0, hh, 0:128, :] = pt[512 + hh * 128:512 + (hh + 1) * 128, :].astype(BF16)
        vt_ref[0, hh, 128:VT_ROWS, :] = ones_rows.astype(BF16)


def _inproj(x, g, sc, sh, cos, sin, cost, sint, w_bf, wt_bf, tm):
    b, s, d = x.shape
    n_tok = w_bf.shape[1]
    n_t = wt_bf.shape[0]
    const2 = lambda shape: pl.BlockSpec(shape, lambda bi, i: (0, 0))
    per_b = pl.BlockSpec((1, 1, d), lambda bi, i: (bi, 0, 0))
    return pl.pallas_call(
        _inproj_kernel,
        grid=(b, s // tm),
        in_specs=[pl.BlockSpec((1, tm, d), lambda bi, i: (bi, i, 0)),
                  const2((1, d)), per_b, per_b,
                  pl.BlockSpec((tm, LANES), lambda bi, i: (i, 0)),
                  pl.BlockSpec((tm, LANES), lambda bi, i: (i, 0)),
                  pl.BlockSpec((LANES, tm), lambda bi, i: (0, i)),
                  pl.BlockSpec((LANES, tm), lambda bi, i: (0, i)),
                  const2((d, n_tok)), const2((n_t, d))],
        out_specs=[pl.BlockSpec((1, tm, P_COLS), lambda bi, i: (bi, i, 0)),
                   pl.BlockSpec((1, A_HEADS, LANES, tm), lambda bi, i: (bi, 0, 0, i)),
                   pl.BlockSpec((1, A_HEADS, VT_ROWS, tm), lambda bi, i: (bi, 0, 0, i))],
        out_shape=[jax.ShapeDtypeStruct((b, s, P_COLS), BF16),
                   jax.ShapeDtypeStruct((b, A_HEADS, LANES, s), BF16),
                   jax.ShapeDtypeStruct((b, A_HEADS, VT_ROWS, s), BF16)],
        compiler_params=_cparams(("arbitrary", "arbitrary")),
        name="inproj",
    )(x, g, sc, sh, cos, sin, cost, sint, w_bf, wt_bf)


_NT = (((1,), (1,)), ((), ()))


def _attn_kernel(lam_ref, qt_ref, kzc_ref, vtc_ref, kzx_ref, vtx_ref, o_ref,
                 acc, s_a, s_b, e_a, e_b, *, tkc):
    qt = qt_ref[0, 0]
    tq = qt.shape[1]
    n_chunks = kzx_ref.shape[1] // tkc
    assert n_chunks % 2 == 0 and n_chunks >= 4

    def scores(kz, s_ref):
        for mp in range(2):
            s_ref[mp] = jnp.dot(kz[:, mp * LANES:(mp + 1) * LANES], qt, preferred_element_type=F32)

    def softmax(s_ref, e_ref, ms):
        new_ms, alphas = [], []
        for mp in range(2):
            st = s_ref[mp]
            m_new = jnp.maximum(ms[mp], jnp.max(st, axis=0, keepdims=True))
            alphas.append(jnp.exp2(ms[mp] - m_new))
            e_ref[mp] = jnp.exp2(st - m_new).astype(BF16)
            new_ms.append(m_new)
        return tuple(new_ms), tuple(alphas)

    def values(vt, e_ref, alphas):
        for mp in range(2):
            acc[mp] = alphas[mp] * acc[mp] + jnp.dot(vt, e_ref[mp], preferred_element_type=F32)

    def kchunk(c):
        return kzx_ref[0, pl.ds(pl.multiple_of(c * tkc, tkc), tkc), :]

    def vchunk(c):
        return vtx_ref[0, 0, :, pl.ds(pl.multiple_of(c * tkc, tkc), tkc)]

    ms = []
    for mp in range(2):
        st = jnp.dot(kzc_ref[0, :, mp * LANES:(mp + 1) * LANES], qt, preferred_element_type=F32)
        m_new = jnp.max(st, axis=0, keepdims=True)
        acc[mp] = jnp.dot(vtc_ref[0, 0], jnp.exp2(st - m_new).astype(BF16), preferred_element_type=F32)
        ms.append(m_new)
    ms = tuple(ms)

    scores(kchunk(0), s_a)
    scores(kchunk(1), s_b)
    ms, al_a = softmax(s_a, e_a, ms)

    def body(j, carry):
        ms, al_a = carry
        scores(kchunk(2 * j + 2), s_a)
        ms, al_b = softmax(s_b, e_b, ms)
        values(vchunk(2 * j), e_a, al_a)
        scores(kchunk(2 * j + 3), s_b)
        ms, al_a = softmax(s_a, e_a, ms)
        values(vchunk(2 * j + 1), e_b, al_b)
        return ms, al_a

    ms, al_a = lax.fori_loop(0, n_chunks // 2 - 1, body, (ms, al_a))
    ms, al_b = softmax(s_b, e_b, ms)
    values(vchunk(n_chunks - 2), e_a, al_a)
    values(vchunk(n_chunks - 1), e_b, al_b)

    a1 = acc[0]
    a2 = acc[1]
    o1 = a1[:LANES, :] / a1[LANES:LANES + 1, :]
    o2 = a2[:LANES, :] / a2[LANES:LANES + 1, :]
    o_ref[0] = (o1 - lam_ref[0] * o2).T.astype(BF16)


def _attention(lam, qt, px, vtx, pc, vtc, tq, tkc):
    b, s, _ = px.shape
    lc = pc.shape[1]
    kern = functools.partial(_attn_kernel, tkc=tkc)
    return pl.pallas_call(
        kern,
        grid=(b, A_HEADS, s // tq),
        in_specs=[pl.BlockSpec(memory_space=pltpu.SMEM),
                  pl.BlockSpec((1, 1, LANES, tq), lambda bi, h, i: (bi, h, 0, i)),
                  pl.BlockSpec((1, lc, 256), lambda bi, h, i: (bi, 0, P_KZ // 256 + h)),
                  pl.BlockSpec((1, 1, VT_ROWS, lc), lambda bi, h, i: (bi, h, 0, 0)),
                  pl.BlockSpec((1, s, 256), lambda bi, h, i: (bi, 0, P_KZ // 256 + h)),
                  pl.BlockSpec((1, 1, VT_ROWS, s), lambda bi, h, i: (bi, h, 0, 0))],
        out_specs=pl.BlockSpec((1, tq, LANES), lambda bi, h, i: (bi, i, h)),
        out_shape=jax.ShapeDtypeStruct((b, s, A_HEADS * LANES), BF16),
        scratch_shapes=[pltpu.VMEM((2, VT_ROWS, tq), F32),
                        pltpu.VMEM((2, tkc, tq), F32), pltpu.VMEM((2, tkc, tq), F32),
                        pltpu.VMEM((2, tkc, tq), BF16), pltpu.VMEM((2, tkc, tq), BF16)],
        compiler_params=_cparams(("arbitrary", "arbitrary", "arbitrary")),
        name="diff_attention",
    )(lam, qt, pc, vtc, px, vtx)


_TN = (((0,), (0,)), ((), ()))


def _ret_kernel(qf_ref, kf_ref, vf_ref, qb_ref, kb_ref, vb_ref, kc_ref, vc_ref,
                dmat_ref, qdec_ref, kdec_ref, gch_ref, of_ref, ob_ref, sf, sb):
    c = RET_CHUNK
    n = pl.program_id(1)
    col_head = lax.broadcasted_iota(jnp.int32, (c, R_HEADS * R_DK), 1) // R_DK
    bd_mask = (lax.broadcasted_iota(jnp.int32, (R_HEADS * R_DK, R_HEADS * R_DV), 0) // R_DK
               == lax.broadcasted_iota(jnp.int32, (R_HEADS * R_DK, R_HEADS * R_DV), 1) // R_DV)

    def state_update(s_ref, k, v, kdec, gch):
        kd = (k.astype(F32) * kdec).astype(BF16)
        kv = lax.dot_general(kd, v, _TN, preferred_element_type=F32)
        s_ref[...] = gch * s_ref[...] + jnp.where(bd_mask, kv, 0.0)

    def chunk(s_ref, q, k, v, dmat, qdec, kdec, gch):
        zero = jnp.zeros_like(k)
        kblk = jnp.concatenate([jnp.where(col_head == h, k, zero) for h in range(R_HEADS)], axis=0)
        sc = lax.dot_general(q, kblk, _NT, preferred_element_type=F32) * dmat
        p = sc.astype(BF16)
        intra = jnp.concatenate(
            [jnp.dot(p[:, h * c:(h + 1) * c], v[:, h * R_DV:(h + 1) * R_DV], preferred_element_type=F32)
             for h in range(R_HEADS)], axis=1)
        qd = (q.astype(F32) * qdec).astype(BF16)
        cross = jnp.dot(qd, s_ref[...].astype(BF16), preferred_element_type=F32)
        out = intra + cross
        state_update(s_ref, k, v, kdec, gch)
        return out

    @pl.when(n == 0)
    def _():
        sf[...] = jnp.zeros_like(sf)
        sb[...] = jnp.zeros_like(sb)
        n_ctx = kc_ref.shape[1] // c
        for cc in range(n_ctx):
            state_update(sf, kc_ref[0, cc * c:(cc + 1) * c, :], vc_ref[0, cc * c:(cc + 1) * c, :],
                         kdec_ref[0], gch_ref[0])
        for cc in reversed(range(n_ctx)):
            state_update(sb, kc_ref[0, cc * c:(cc + 1) * c, :], vc_ref[0, cc * c:(cc + 1) * c, :],
                         kdec_ref[1], gch_ref[1])

    chunks_per_step = qf_ref.shape[1] // c
    for cc in range(chunks_per_step):
        rows = slice(cc * c, (cc + 1) * c)
        of_ref[0, rows, :] = chunk(sf, qf_ref[0, rows, :], kf_ref[0, rows, :], vf_ref[0, rows, :], dmat_ref[0],
                                   qdec_ref[0], kdec_ref[0], gch_ref[0]).astype(BF16)
    for cc in reversed(range(chunks_per_step)):
        rows = slice(cc * c, (cc + 1) * c)
        ob_ref[0, rows, :] = chunk(sb, qb_ref[0, rows, :], kb_ref[0, rows, :], vb_ref[0, rows, :], dmat_ref[1],
                                   qdec_ref[1], kdec_ref[1], gch_ref[1]).astype(BF16)


def _retention(px, pc, dmat, qdec, kdec, gch, chunks_per_step):
    b, s, _ = px.shape
    lc = pc.shape[1]
    c = RET_CHUNK
    rows = c * chunks_per_step
    nch = s // rows
    qk_w = R_HEADS * R_DK
    v_w = R_HEADS * R_DV
    fwd = lambda bi, n: (bi, n, 0)
    spec_q = lambda im: pl.BlockSpec((1, rows, qk_w), lambda bi, n: im(bi, n)[:2] + (P_RQ // qk_w,))
    spec_k = lambda im: pl.BlockSpec((1, rows, qk_w), lambda bi, n: im(bi, n)[:2] + (P_RK // qk_w,))
    spec_v = lambda im: pl.BlockSpec((1, rows, v_w), lambda bi, n: im(bi, n)[:2] + (P_RV // v_w,))
    bwd = lambda bi, n: (bi, nch - 1 - n, 0)
    const3 = lambda shape: pl.BlockSpec(shape, lambda bi, n: (0, 0, 0))
    return pl.pallas_call(
        _ret_kernel,
        grid=(b, nch),
        in_specs=[spec_q(fwd), spec_k(fwd), spec_v(fwd), spec_q(bwd), spec_k(bwd), spec_v(bwd),
                  pl.BlockSpec((1, lc, qk_w), lambda bi, n: (bi, 0, P_RK // qk_w)),
                  pl.BlockSpec((1, lc, v_w), lambda bi, n: (bi, 0, P_RV // v_w)),
                  const3((2, c, R_HEADS * c)), const3((2, c, qk_w)), const3((2, c, qk_w)),
                  const3((2, 1, v_w))],
        out_specs=[pl.BlockSpec((1, rows, v_w), fwd), pl.BlockSpec((1, rows, v_w), bwd)],
        out_shape=[jax.ShapeDtypeStruct((b, s, v_w), BF16), jax.ShapeDtypeStruct((b, s, v_w), BF16)],
        scratch_shapes=[pltpu.VMEM((qk_w, v_w), F32), pltpu.VMEM((qk_w, v_w), F32)],
        compiler_params=_cparams(("arbitrary", "arbitrary")),
        name="retention",
    )(px, px, px, px, px, px, pc, pc, dmat, qdec, kdec, gch)


def _retention_tables(decay_fwd, decay_bwd):
    c = RET_CHUNK
    pos = jnp.arange(c, dtype=F32)
    rel = pos[:, None] - pos[None, :]

    def tables(logits, backward):
        lg = jax.nn.log_sigmoid(logits.astype(F32))
        r = -rel if backward else rel
        dm = jnp.where(r[None] >= 0, jnp.exp(jnp.maximum(r, 0.0)[None] * lg[:, None, None]), 0.0)
        dm = jnp.transpose(dm, (1, 0, 2)).reshape(c, R_HEADS * c)
        qpow = (c - pos) if backward else (pos + 1)
        kpow = pos if backward else (c - 1 - pos)
        qd = jnp.repeat(jnp.exp(qpow[:, None] * lg[None, :]), R_DK, axis=1)
        kd = jnp.repeat(jnp.exp(kpow[:, None] * lg[None, :]), R_DK, axis=1)
        gc = jnp.repeat(jnp.exp(c * lg), R_DV)[None, :]
        return dm, qd, kd, gc

    tf = tables(decay_fwd, False)
    tb = tables(decay_bwd, True)
    return tuple(jnp.stack([a, bb]) for a, bb in zip(tf, tb))


def _pack_pairs(x):
    k = x.shape[1] // 2
    lo = lax.bitcast_convert_type(x[:, :k].astype(BF16).astype(F32), U32)
    hi = lax.bitcast_convert_type(x[:, k:].astype(BF16).astype(F32), U32)
    return (lo >> 16) | (hi & jnp.uint32(0xFFFF0000))


def _unpack_pairs(w):
    lo = lax.bitcast_convert_type(w << 16, F32)
    hi = lax.bitcast_convert_type(w & jnp.uint32(0xFFFF0000), F32)
    return lo, hi


ROW_CHUNKS = (D_MODEL // 2) // LANES


def _store_token_rows(ref, first_token, n_tokens, x):
    packed = _pack_pairs(x)
    for c in range(ROW_CHUNKS):
        ref[pl.ds(first_token * ROW_CHUNKS + c, n_tokens, stride=ROW_CHUNKS), :] = (
            packed[:, c * LANES:(c + 1) * LANES])


def _load_token_rows(ref, first_token, n_tokens):
    return [_unpack_pairs(ref[pl.ds(first_token * ROW_CHUNKS + c, n_tokens, stride=ROW_CHUNKS), :])
            for c in range(ROW_CHUNKS)]


def _rows_to_features(chunks):
    return jnp.concatenate([lo for lo, _ in chunks] + [hi for _, hi in chunks], axis=1)


def _route_tile(logits_t, bias_col, carry_ref):
    n_e, tm = logits_t.shape
    per = n_e // N_GROUPS
    neg = -jnp.inf
    scores = jax.nn.sigmoid(logits_t)
    sel = scores + bias_col
    sub = lax.broadcasted_iota(jnp.int32, (per, tm), 0).astype(F32)
    sel_g = [sel[g * per:(g + 1) * per, :] for g in range(N_GROUPS)]
    flat_g = [sub + float(g * per) for g in range(N_GROUPS)]

    gs = []
    for g in range(N_GROUPS):
        top1 = jnp.max(sel_g[g], axis=0, keepdims=True)
        idx1 = jnp.min(jnp.where(sel_g[g] == top1, sub, float(per)), axis=0, keepdims=True)
        top2 = jnp.max(jnp.where(sub == idx1, neg, sel_g[g]), axis=0, keepdims=True)
        gs.append(top1 + top2)
    gsel = [jnp.zeros((1, tm), F32) for _ in range(N_GROUPS)]
    for _ in range(TOPK_GROUPS):
        m = functools.reduce(jnp.maximum, gs)
        found = jnp.zeros((1, tm), F32)
        for g in range(N_GROUPS):
            hit = jnp.logical_and(gs[g] == m, found == 0.0)
            found = jnp.where(hit, 1.0, found)
            gsel[g] = jnp.where(hit, 1.0, gsel[g])
            gs[g] = jnp.where(hit, neg, gs[g])

    masked = [jnp.where(gsel[g] > 0.0, sel_g[g], neg) for g in range(N_GROUPS)]
    chosen = [jnp.zeros((per, tm), F32) for _ in range(N_GROUPS)]
    for _ in range(TOP_K):
        m = functools.reduce(jnp.maximum, [jnp.max(mg, axis=0, keepdims=True) for mg in masked])
        idx = functools.reduce(jnp.minimum, [
            jnp.min(jnp.where(masked[g] == m, flat_g[g], float(n_e)), axis=0, keepdims=True)
            for g in range(N_GROUPS)])
        for g in range(N_GROUPS):
            hit = flat_g[g] == idx
            chosen[g] = jnp.where(hit, 1.0, chosen[g])
            masked[g] = jnp.where(hit, neg, masked[g])
    chosen = jnp.concatenate(chosen, axis=0)
    flat = jnp.concatenate(flat_g, axis=0)

    w_dense = scores * chosen
    w_dense = w_dense / jnp.sum(w_dense, axis=0, keepdims=True) * ROUTED_SCALE

    chosen_bf = chosen.astype(BF16)
    lower = (lax.broadcasted_iota(jnp.int32, (n_e, n_e), 1)
             < lax.broadcasted_iota(jnp.int32, (n_e, n_e), 0)).astype(BF16)
    slot_idx = jnp.dot(lower, chosen_bf, preferred_element_type=F32)
    upper = (lax.broadcasted_iota(jnp.int32, (tm, tm), 0)
             < lax.broadcasted_iota(jnp.int32, (tm, tm), 1)).astype(BF16)
    rank = jnp.dot(chosen_bf, upper, preferred_element_type=F32) + carry_ref[...]
    carry_ref[...] += jnp.sum(chosen, axis=1, keepdims=True)

    slots = []
    for j in range(TOP_K):
        ind = jnp.where(slot_idx == float(j), chosen, 0.0)
        slots.append((jnp.sum(ind * flat, axis=0, keepdims=True),
                      jnp.sum(ind * w_dense, axis=0, keepdims=True),
                      jnp.sum(ind * rank, axis=0, keepdims=True)))
    return slots


def _merge_kernel(att_ref, rf_ref, rb_ref, gate_ref, x_ref, g1_ref, sh2_ref, sc2_ref, ga_ref, gr_ref,
                  wout_ref, n2g_ref, rwt_ref, rbias_ref, x1_ref, h2p_ref, se_ref, sw_ref, sr_ref, cnt_ref,
                  carry, *, att_scale):
    @pl.when(jnp.logical_and(pl.program_id(0) == 0, pl.program_id(1) == 0))
    def _():
        carry[...] = jnp.zeros_like(carry)

    att = att_ref[0].astype(F32)
    ret = rf_ref[0].astype(F32) + rb_ref[0].astype(F32)
    gate = gate_ref[0].astype(F32)
    parts = []
    for h in range(A_HEADS):
        a = att[:, h * 128:(h + 1) * 128]
        parts.append(_rmsnorm_rows(a, ga_ref[...]) * att_scale)
    for h in range(R_HEADS):
        r = _rmsnorm_rows(ret[:, h * R_DV:(h + 1) * R_DV], gr_ref[...])
        g = gate[:, h * R_DV:(h + 1) * R_DV]
        parts.append(g * jax.nn.sigmoid(g) * r)
    merged = jnp.concatenate(parts, axis=1).astype(BF16)
    y = jnp.dot(merged, wout_ref[...], preferred_element_type=F32)
    x1 = x_ref[0] + g1_ref[0] * y
    x1_ref[0] = x1
    h2 = _rmsnorm_rows(x1, n2g_ref[...]) * (1.0 + sc2_ref[0]) + sh2_ref[0]
    _store_token_rows(h2p_ref, 0, h2.shape[0], h2)
    logits_t = lax.dot_general(rwt_ref[...], h2.astype(BF16), _NT, preferred_element_type=F32)
    slots = _route_tile(logits_t, rbias_ref[...], carry)
    for j, (e_j, w_j, r_j) in enumerate(slots):
        se_ref[j:j + 1, :] = e_j.astype(jnp.int32)
        sw_ref[j:j + 1, :] = w_j
        sr_ref[j:j + 1, :] = r_j.astype(jnp.int32)
    cnt_ref[...] = carry[...]


def _merge(att, rf, rb, px, x, g1, sh2, sc2, ga, gr, wout_bf, n2g, rwt_bf, rbias, att_scale, tm):
    b, s, d = x.shape
    nt = s // tm
    chunked = pl.BlockSpec((ROW_CHUNKS * tm, LANES), lambda bi, i: (bi * nt + i, 0))
    chunked_shape = jax.ShapeDtypeStruct((b * s * ROW_CHUNKS, LANES), U32)
    row = lambda bi, i: (bi, i, 0)
    tok = lambda bi, i: (0, bi * nt + i)
    per_b = pl.BlockSpec((1, 1, d), lambda bi, i: (bi, 0, 0))
    const2 = lambda shape: pl.BlockSpec(shape, lambda bi, i: (0, 0))
    return pl.pallas_call(
        functools.partial(_merge_kernel, att_scale=att_scale),
        grid=(b, nt),
        in_specs=[pl.BlockSpec((1, tm, 512), row), pl.BlockSpec((1, tm, 512), row),
                  pl.BlockSpec((1, tm, 512), row),
                  pl.BlockSpec((1, tm, 512), lambda bi, i: (bi, i, P_RG // 512)),
                  pl.BlockSpec((1, tm, d), row), per_b, per_b, per_b,
                  const2((1, 128)), const2((1, R_DV)), const2((d, d)), const2((1, d)),
                  const2((N_EXPERTS, d)), const2((N_EXPERTS, 1))],
        out_specs=[pl.BlockSpec((1, tm, d), row), chunked,
                   pl.BlockSpec((TOP_K, tm), tok), pl.BlockSpec((TOP_K, tm), tok),
                   pl.BlockSpec((TOP_K, tm), tok), const2((N_EXPERTS, 1))],
        out_shape=[jax.ShapeDtypeStruct((b, s, d), F32), chunked_shape,
                   jax.ShapeDtypeStruct((TOP_K, b * s), jnp.int32),
                   jax.ShapeDtypeStruct((TOP_K, b * s), F32),
                   jax.ShapeDtypeStruct((TOP_K, b * s), jnp.int32),
                   jax.ShapeDtypeStruct((N_EXPERTS, 1), F32)],
        scratch_shapes=[pltpu.VMEM((N_EXPERTS, 1), F32)],
        compiler_params=_cparams(("arbitrary", "arbitrary")),
        name="mixer_merge",
    )(att, rf, rb, px, x, g1, sh2, sc2, ga, gr, wout_bf, n2g, rwt_bf, rbias)


def _dispatch_kernel(dest_ref, h_ref, xs_ref, buf, sem):
    rows = h_ref.shape[0]
    tm = rows // ROW_CHUNKS
    i = pl.program_id(0)
    n = pl.num_programs(0)
    slot = i % 2

    def wait_slot(s):
        for _ in range(TOP_K):
            pltpu.make_async_copy(buf.at[s], xs_ref.at[pl.ds(0, rows)], sem.at[s]).wait()

    @pl.when(i >= 2)
    def _():
        wait_slot(slot)

    buf[slot] = h_ref[...]

    def body(t, carry):
        src = buf.at[slot, pl.ds(pl.multiple_of(t * ROW_CHUNKS, ROW_CHUNKS), ROW_CHUNKS)]
        for j in range(TOP_K):
            d = pl.multiple_of(dest_ref[t * TOP_K + j], ROW_CHUNKS)
            pltpu.make_async_copy(src, xs_ref.at[pl.ds(d, ROW_CHUNKS)], sem.at[slot]).start(priority=j % 2)
        return carry

    lax.fori_loop(0, tm, body, 0)

    @pl.when(i == n - 1)
    def _():
        wait_slot(slot)

    @pl.when(jnp.logical_and(i == n - 1, n >= 2))
    def _():
        wait_slot(1 - slot)


def _dispatch(dest_rows, h2p, tm):
    n_rows = h2p.shape[0]
    n = n_rows // (ROW_CHUNKS * tm)
    return pl.pallas_call(
        _dispatch_kernel,
        grid=(n,),
        in_specs=[pl.BlockSpec((tm * TOP_K,), lambda i: (i,), memory_space=pltpu.SMEM),
                  pl.BlockSpec((ROW_CHUNKS * tm, LANES), lambda i: (i, 0))],
        out_specs=pl.BlockSpec(memory_space=pl.ANY),
        out_shape=jax.ShapeDtypeStruct((n_rows * TOP_K, LANES), U32),
        scratch_shapes=[pltpu.VMEM((2, ROW_CHUNKS * tm, LANES), U32), pltpu.SemaphoreType.DMA((2,))],
        compiler_params=_cparams(("arbitrary",)),
        name="moe_dispatch",
    )(dest_rows, h2p)


def _expert_kernel(iblk_ref, iexp_ref, ilo_ref, ihi_ref, xs_ref, w1_ref, w3_ref, w2_ref, ys_ref,
                   acc, hm, w1b, w3b, w2b):
    r = xs_ref.shape[0] // ROW_CHUNKS
    i = pl.program_id(0)
    n_items = pl.num_programs(0) - 1
    cur = jnp.minimum(i, n_items - 1)
    cur_prev = jnp.maximum(cur - 1, 0)
    dn = jnp.maximum(i - 1, 0)
    dn_prev = jnp.maximum(i - 2, 0)
    blk = iblk_ref[cur]
    lo = ilo_ref[cur]
    hi = ihi_ref[cur]
    dn_first = jnp.logical_or(i <= 1, iblk_ref[dn_prev] != iblk_ref[dn])
    slot = i % 2

    @pl.when(i == 0)
    def _():
        acc[...] = jnp.zeros_like(acc)
        hm[...] = jnp.zeros_like(hm)

    @pl.when(jnp.logical_or(i == 0, iexp_ref[cur_prev] != iexp_ref[cur]))
    def _():
        w1b[...] = w1_ref[0].astype(BF16)
        w3b[...] = w3_ref[0].astype(BF16)

    @pl.when(jnp.logical_or(i <= 1, iexp_ref[dn_prev] != iexp_ref[dn]))
    def _():
        w2b[...] = w2_ref[0].astype(BF16)

    n_half = 4
    rh = r // n_half

    def run(hm_rd, hm_wr):
        for hf in range(n_half):
            sl = slice(hf * rh, (hf + 1) * rh)
            y = jnp.dot(hm_rd[sl, :], w2b[...], preferred_element_type=F32)
            y = jnp.where(dn_first, y, acc[sl, :] + y)
            acc[sl, :] = y
            _store_token_rows(ys_ref, hf * rh, rh, y)

            rows = blk * r + hf * rh + lax.broadcasted_iota(jnp.int32, (rh, 1), 0)
            valid = jnp.logical_and(rows >= lo, rows < hi)
            x = _rows_to_features(_load_token_rows(xs_ref, hf * rh, rh))
            x = jnp.where(valid, x, 0.0).astype(BF16)
            a = jnp.dot(x, w1b[...], preferred_element_type=F32)
            bb = jnp.dot(x, w3b[...], preferred_element_type=F32)
            hm_wr[sl, :] = (a * jax.nn.sigmoid(a) * bb).astype(BF16)

    @pl.when(slot == 0)
    def _():
        run(hm.at[1], hm.at[0])

    @pl.when(slot == 1)
    def _():
        run(hm.at[0], hm.at[1])


def _experts(iblk, iexp, ilo, ihi, xs, w1, w3, w2, r):
    na, kw = xs.shape
    rows = r * ROW_CHUNKS
    n_items = iblk.shape[0]
    d = w1.shape[1]
    de = w1.shape[2]
    up = lambda i: jnp.minimum(i, n_items - 1)
    down = lambda i: jnp.maximum(i - 1, 0)
    grid_spec = pltpu.PrefetchScalarGridSpec(
        num_scalar_prefetch=4,
        grid=(n_items + 1,),
        in_specs=[pl.BlockSpec((rows, kw), lambda i, ib, ie, il, ih: (ib[up(i)], 0)),
                  pl.BlockSpec((1, d, de), lambda i, ib, ie, il, ih: (ie[up(i)], 0, 0)),
                  pl.BlockSpec((1, d, de), lambda i, ib, ie, il, ih: (ie[up(i)], 0, 0)),
                  pl.BlockSpec((1, de, d), lambda i, ib, ie, il, ih: (ie[down(i)], 0, 0))],
        out_specs=pl.BlockSpec((rows, kw), lambda i, ib, ie, il, ih: (ib[down(i)], 0)),
        scratch_shapes=[pltpu.VMEM((r, d), F32), pltpu.VMEM((2, r, de), BF16),
                        pltpu.VMEM((d, de), BF16), pltpu.VMEM((d, de), BF16), pltpu.VMEM((de, d), BF16)],
    )
    return pl.pallas_call(
        _expert_kernel,
        grid_spec=grid_spec,
        out_shape=jax.ShapeDtypeStruct((na, kw), U32),
        compiler_params=_cparams(("arbitrary",)),
        name="moe_experts",
    )(iblk, iexp, ilo, ihi, xs, w1, w3, w2)


def _combine_kernel(dest_ref, dnext_ref, x1_ref, h2p_ref, wts_ref, g2_ref, sw1_ref, sw3_ref, sw2_ref,
                    fg_ref, ys_ref, o_ref, gbuf, sem):
    tm = x1_ref.shape[0]
    i = pl.program_id(0)
    n = pl.num_programs(0)
    slot = i % 2

    rows = ROW_CHUNKS * tm

    def gather_row(dref, s, t, off, j):
        d = pl.multiple_of(dref[t * TOP_K + j], ROW_CHUNKS)
        pltpu.make_async_copy(ys_ref.at[pl.ds(d, ROW_CHUNKS)], gbuf.at[s, j, pl.ds(off, ROW_CHUNKS)],
                              sem.at[s]).start(priority=j % 2)

    def wait_slot(s):
        for _ in range(TOP_K):
            pltpu.make_async_copy(ys_ref.at[pl.ds(0, rows)], gbuf.at[s, 0], sem.at[s]).wait()

    @pl.when(i == 0)
    def _():
        def body(t, carry):
            for j in range(TOP_K):
                gather_row(dest_ref, 0, t, pl.multiple_of(t * ROW_CHUNKS, ROW_CHUNKS), j)
            return carry
        lax.fori_loop(0, tm, body, 0)

    wait_slot(slot)
    for t in range(tm):
        for j in range(TOP_K):
            gather_row(dnext_ref, 1 - slot, t, t * ROW_CHUNKS, j)

    wts = wts_ref[...]
    moe = None
    for j in range(TOP_K):
        wj = wts[:, j:j + 1]
        yj = [(lo * wj, hi * wj) for lo, hi in _load_token_rows(gbuf.at[slot, j], 0, tm)]
        moe = yj if moe is None else [(a + lo, b + hi) for (a, b), (lo, hi) in zip(moe, yj)]
    h = _rows_to_features(_load_token_rows(h2p_ref, 0, tm)).astype(BF16)
    a = jnp.dot(h, sw1_ref[...], preferred_element_type=F32)
    bb = jnp.dot(h, sw3_ref[...], preferred_element_type=F32)
    shared = jnp.dot((a * jax.nn.sigmoid(a) * bb).astype(BF16), sw2_ref[...], preferred_element_type=F32)
    ffn = _rows_to_features(moe) + shared
    xo = x1_ref[...] + g2_ref[0] * ffn
    o_ref[...] = _rmsnorm_rows(xo, fg_ref[...])

    @pl.when(i == n - 1)
    def _():
        wait_slot(1 - slot)


def _combine(dest_rows, x1, h2p, wts, g2, sw1, sw3, sw2, fg, ys, tiles_per_batch, tm):
    t, d = x1.shape
    n = t // tm
    de = sw1.shape[1]
    const2 = lambda shape: pl.BlockSpec(shape, lambda i: (0, 0))
    return pl.pallas_call(
        _combine_kernel,
        grid=(n,),
        in_specs=[pl.BlockSpec((tm * TOP_K,), lambda i: (i,), memory_space=pltpu.SMEM),
                  pl.BlockSpec((tm * TOP_K,), lambda i: (jnp.minimum(i + 1, n - 1),),
                               memory_space=pltpu.SMEM),
                  pl.BlockSpec((tm, d), lambda i: (i, 0)),
                  pl.BlockSpec((ROW_CHUNKS * tm, LANES), lambda i: (i, 0)),
                  pl.BlockSpec((tm, TOP_K), lambda i: (i, 0)),
                  pl.BlockSpec((1, 1, d), lambda i: (i // tiles_per_batch, 0, 0)),
                  const2((d, de)), const2((d, de)), const2((de, d)), const2((1, d)),
                  pl.BlockSpec(memory_space=pl.ANY)],
        out_specs=pl.BlockSpec((tm, d), lambda i: (i, 0)),
        out_shape=jax.ShapeDtypeStruct((t, d), F32),
        scratch_shapes=[pltpu.VMEM((2, TOP_K, ROW_CHUNKS * tm, LANES), U32),
                        pltpu.SemaphoreType.DMA((2,))],
        compiler_params=_cparams(("arbitrary",)),
        name="moe_combine",
    )(dest_rows, dest_rows, x1, h2p, wts, g2, sw1, sw3, sw2, fg, ys)


def _route_meta(counts, slot_e, slot_r, block_rows):
    t = slot_e.shape[1]
    ends = jnp.cumsum(counts)
    starts = ends - counts
    experts = jnp.arange(N_EXPERTS, dtype=jnp.int32)
    dest_t = slot_r + jnp.sum(jnp.where(slot_e[:, :, None] == experts, starts, 0), axis=-1)

    n_blk = t * TOP_K // block_rows
    n_items = n_blk + N_EXPERTS - 1
    first_blk = starts // block_rows
    last_blk = (ends - 1) // block_rows
    n_e = jnp.where(counts > 0, last_blk - first_blk + 1, 0)
    item_end = jnp.cumsum(n_e)
    item_off = item_end - n_e
    total = item_end[-1]
    idx = jnp.arange(n_items, dtype=jnp.int32)
    e_of = jnp.minimum(jnp.sum(item_end[None, :] <= idx[:, None], axis=1), N_EXPERTS - 1).astype(jnp.int32)
    onehot = e_of[:, None] == experts[None, :]
    lut = lambda table: jnp.sum(jnp.where(onehot, table[None, :], 0), axis=1)
    live = idx < total
    blk = lut(first_blk) + (idx - lut(item_off))
    lo = jnp.maximum(lut(starts), blk * block_rows)
    hi = jnp.minimum(lut(ends), (blk + 1) * block_rows)
    last_e = jnp.sum(jnp.where(idx == total - 1, e_of, 0))
    iblk = jnp.where(live, blk, n_blk - 1).astype(jnp.int32)
    iexp = jnp.where(live, e_of, last_e).astype(jnp.int32)
    ilo = jnp.where(live, lo, 0).astype(jnp.int32)
    ihi = jnp.where(live, hi, 0).astype(jnp.int32)
    return dest_t, iblk, iexp, ilo, ihi


def _rope_tables(n_tokens):
    n_rows = n_tokens // GRID_W
    rows = np.repeat(np.arange(n_rows, dtype=np.float32), GRID_W)
    cols = np.tile(np.arange(GRID_W, dtype=np.float32), n_rows)
    n_freq = HEAD_DIM // 4
    inv_freq = (np.float32(ROPE_BASE) ** (-np.arange(n_freq, dtype=np.float32) / n_freq)).astype(np.float32)
    ar = rows[:, None] * inv_freq
    ac = cols[:, None] * inv_freq
    cos64 = np.concatenate([np.cos(ar), np.cos(ar), np.cos(ac), np.cos(ac)], axis=1)
    sin64 = np.concatenate([-np.sin(ar), np.sin(ar), -np.sin(ac), np.sin(ac)], axis=1)
    return np.tile(cos64, (1, 2)).astype(np.float32), np.tile(sin64, (1, 2)).astype(np.float32)


def kernel(x, c, ctx, c_ctx, ada_w, ada_b, norm1_g, w_in, lambda_q1, lambda_k1, lambda_q2, lambda_k2,
           dattn_norm_g, ret_decay_fwd, ret_decay_bwd, ret_norm_g, w_out, norm2_g, router_w, router_bias,
           exp_w1, exp_w3, exp_w2, shared_w1, shared_w3, shared_w2, final_norm_g):
    b, s, d = x.shape
    lc = ctx.shape[1]
    depth = ada_w.shape[0]
    assert depth == 1, "single-layer problem: the context stream is never updated"
    l = 0
    lam_init = 0.8 - 0.6 * math.exp(-0.3 * l)

    c8 = jnp.zeros((8, d), F32).at[:b].set(c).at[b].set(c_ctx)
    mod = _modulation(c8, ada_w[l], ada_b[l])
    sh1, sc1, g1, sh2, sc2, g2 = [mod[:b, k * d:(k + 1) * d].reshape(b, 1, d) for k in range(6)]
    csh1 = jnp.broadcast_to(mod[b, 0:d].reshape(1, 1, d), (b, 1, d))
    csc1 = jnp.broadcast_to(mod[b, d:2 * d].reshape(1, 1, d), (b, 1, d))

    w_l = w_in[l]
    w_tok = jnp.concatenate([w_l[:, 512:1024], w_l[:, 1536:3072]], axis=1).astype(BF16)
    w_feat = jnp.concatenate([w_l[:, 0:512], w_l[:, 1024:1536]], axis=1).T.astype(BF16)
    n1g = norm1_g[l].reshape(1, d)
    cos, sin = _rope_tables(s)
    ones_c = jnp.ones((lc, LANES), F32)
    zeros_c = jnp.zeros((lc, LANES), F32)
    tiles = _tile_plan(s)
    px, qt, vtx = _inproj(x, n1g, sc1, sh1, cos, sin, np.ascontiguousarray(cos.T),
                          np.ascontiguousarray(sin.T), w_tok, w_feat, tm=tiles.proj_tokens)
    pc, _, vtc = _inproj(ctx, n1g, csc1, csh1, ones_c, zeros_c, ones_c.T, zeros_c.T, w_tok, w_feat, tm=lc)

    lam = (jnp.exp(jnp.sum(lambda_q1[l] * lambda_k1[l])) - jnp.exp(jnp.sum(lambda_q2[l] * lambda_k2[l]))
           ).astype(F32) + lam_init
    att = _attention(lam.reshape(1), qt, px, vtx, pc, vtc, tq=tiles.attn_queries, tkc=tiles.attn_keys)

    dmat, qdec, kdec, gch = _retention_tables(ret_decay_fwd[l], ret_decay_bwd[l])
    ret_f, ret_b = _retention(px, pc, dmat, qdec, kdec, gch, chunks_per_step=tiles.ret_chunks)

    x1, h2p, slot_e, slot_w, slot_r, counts = _merge(
        att, ret_f, ret_b, px, x, g1, sh2, sc2,
        dattn_norm_g[l].reshape(1, 2 * HEAD_DIM), ret_norm_g[l].reshape(1, R_DV),
        w_out[l].astype(BF16), norm2_g[l].reshape(1, d), router_w[l].T.astype(BF16),
        router_bias[l].astype(F32).reshape(N_EXPERTS, 1), 1.0 - lam_init, tm=tiles.proj_tokens)

    t = b * s
    dest_t, iblk, iexp, ilo, ihi = _route_meta(counts[:, 0].astype(jnp.int32), slot_e, slot_r,
                                               tiles.expert_rows)
    dest_rows = (dest_t.T * ROW_CHUNKS).reshape(t * TOP_K)
    xs = _dispatch(dest_rows, h2p, tm=tiles.dispatch_tokens)
    ys = _experts(iblk, iexp, ilo, ihi, xs, exp_w1[l], exp_w3[l], exp_w2[l], tiles.expert_rows)
    out = _combine(dest_rows, x1.reshape(t, d), h2p, slot_w.T, g2, shared_w1[l].astype(BF16),
                   shared_w3[l].astype(BF16), shared_w2[l].astype(BF16), final_norm_g.reshape(1, d),
                   ys, s // tiles.combine_tokens, tiles.combine_tokens)
    return out.reshape(b, s, d)
```
